```python
import jax, jax.numpy as jnp
from jax import lax
import numpy as np

D_MODEL = 2048
BATCH = 8
SEQ = 8192
DEPTH = 4

N_MIXERS = 2
N_A_LAYERS = (DEPTH + 1) // 2
N_B_LAYERS = DEPTH // 2
EPS = 1e-6

EXPAND = 2
CHUNK = 128
A_WIDTH = EXPAND * D_MODEL
A_GROUPS = 16
A_GROUP_DIM = A_WIDTH // A_GROUPS

B_HEAD_DIM = 128
B_HEADS = D_MODEL // B_HEAD_DIM
B_WIDTH = B_HEADS * B_HEAD_DIM
Q_BLOCK = 128
FORGET_BIAS_MEAN = 3.0

kernel_name = "hybrid_gmlp_fox_interleaved"


def rms_norm(x, g):
    xf = x.astype(jnp.float32)
    y = xf * lax.rsqrt(jnp.mean(xf * xf, axis=-1, keepdims=True) + EPS)
    return (y * g.astype(jnp.float32)).astype(x.dtype)


def spatial_gating_layer(x, norm_g, w_in, v_norm_g, w_s, b_s, w_out):
    b, s, _ = x.shape
    h = rms_norm(x, norm_g)
    u, v, z = jnp.split(h @ w_in, 3, axis=-1)
    u = jax.nn.gelu(u)
    v = rms_norm(jax.nn.gelu(v), v_norm_g)
    n_chunks = s // CHUNK
    v = v.reshape(b, n_chunks, CHUNK, A_GROUPS, A_GROUP_DIM)
    causal = jnp.tril(jnp.ones((CHUNK, CHUNK), dtype=bool))
    w_causal = jnp.where(causal[None], w_s, 0)
    mixed = jnp.einsum('gts,bcsgd->bctgd', w_causal, v) + b_s.T[None, None, :, :, None]
    mixed = mixed.reshape(b, s, A_WIDTH)
    y = u * mixed * jax.nn.silu(z)
    return y @ w_out


def forgetting_attention_layer(x, norm_g, w_in, f_bias, q_norm_g, k_norm_g, w_out):
    b, s, _ = x.shape
    h = rms_norm(x, norm_g)
    proj = h @ w_in
    q, k, v, z, f_logit = jnp.split(
        proj, [B_WIDTH, 2 * B_WIDTH, 3 * B_WIDTH, 4 * B_WIDTH], axis=-1)
    q = rms_norm(q.reshape(b, s, B_HEADS, B_HEAD_DIM), q_norm_g)
    k = rms_norm(k.reshape(b, s, B_HEADS, B_HEAD_DIM), k_norm_g)
    v = v.reshape(b, s, B_HEADS, B_HEAD_DIM)
    log_f = jax.nn.log_sigmoid((f_logit + f_bias).astype(jnp.float32))
    cum = jnp.cumsum(log_f, axis=1).transpose(0, 2, 1)
    scale = B_HEAD_DIM ** -0.5
    q_t = q.transpose(0, 2, 1, 3)
    k_t = k.transpose(0, 2, 1, 3)
    v_t = v.transpose(0, 2, 1, 3)
    n_blocks = s // Q_BLOCK
    q_blocks = q_t.reshape(b, B_HEADS, n_blocks, Q_BLOCK, B_HEAD_DIM).transpose(2, 0, 1, 3, 4)
    c_blocks = cum.reshape(b, B_HEADS, n_blocks, Q_BLOCK).transpose(2, 0, 1, 3)
    blk_idx = jnp.arange(n_blocks, dtype=jnp.int32)
    k_pos = jnp.arange(s, dtype=jnp.int32)

    def attend_block(args):
        q_blk, c_blk, i = args
        logits = jnp.einsum('bhqd,bhkd->bhqk', q_blk, k_t,
                            preferred_element_type=jnp.float32) * scale
        logits = logits + c_blk[..., :, None] - cum[:, :, None, :]
        q_pos = i * Q_BLOCK + jnp.arange(Q_BLOCK, dtype=jnp.int32)
        mask = k_pos[None, :] <= q_pos[:, None]
        logits = jnp.where(mask, logits, -jnp.inf)
        p = jax.nn.softmax(logits, axis=-1).astype(v_t.dtype)
        return jnp.einsum('bhqk,bhkd->bhqd', p, v_t)

    out = lax.map(attend_block, (q_blocks, c_blocks, blk_idx))
    out = out.transpose(1, 0, 3, 2, 4).reshape(b, s, B_WIDTH)
    y = out * jax.nn.silu(z)
    return y @ w_out


def _fwd_setup_inputs(seed: int = 0) -> dict:
    key = jax.random.key(seed)
    ks = jax.random.split(key, 13)
    f32 = jnp.float32
    x = jax.random.normal(ks[0], (BATCH, SEQ, D_MODEL), f32)
    a_norm_g = 1.0 + 0.05 * jax.random.normal(ks[1], (N_A_LAYERS, D_MODEL), f32)
    a_w_in = jax.random.normal(ks[2], (N_A_LAYERS, D_MODEL, 3 * A_WIDTH), f32) * D_MODEL ** -0.5
    a_v_norm_g = 1.0 + 0.05 * jax.random.normal(ks[3], (N_A_LAYERS, A_WIDTH), f32)
    a_w_s = jax.random.normal(ks[4], (N_A_LAYERS, A_GROUPS, CHUNK, CHUNK), f32) * CHUNK ** -0.5
    a_b_s = 1.0 + 0.1 * jax.random.normal(ks[5], (N_A_LAYERS, A_GROUPS, CHUNK), f32)
    a_w_out = jax.random.normal(ks[6], (N_A_LAYERS, A_WIDTH, D_MODEL), f32) * A_WIDTH ** -0.5
    b_norm_g = 1.0 + 0.05 * jax.random.normal(ks[7], (N_B_LAYERS, D_MODEL), f32)
    b_w_in = jax.random.normal(ks[8], (N_B_LAYERS, D_MODEL, 4 * B_WIDTH + B_HEADS), f32) * D_MODEL ** -0.5
    b_f_bias = FORGET_BIAS_MEAN + 0.5 * jax.random.normal(ks[9], (N_B_LAYERS, B_HEADS), f32)
    b_q_norm_g = 1.0 + 0.05 * jax.random.normal(ks[10], (N_B_LAYERS, B_HEAD_DIM), f32)
    b_k_norm_g = 1.0 + 0.05 * jax.random.normal(ks[11], (N_B_LAYERS, B_HEAD_DIM), f32)
    b_w_out = jax.random.normal(ks[12], (N_B_LAYERS, B_WIDTH, D_MODEL), f32) * B_WIDTH ** -0.5
    return {"x": x, "a_norm_g": a_norm_g, "a_w_in": a_w_in, "a_v_norm_g": a_v_norm_g,
            "a_w_s": a_w_s, "a_b_s": a_b_s, "a_w_out": a_w_out,
            "b_norm_g": b_norm_g, "b_w_in": b_w_in, "b_f_bias": b_f_bias,
            "b_q_norm_g": b_q_norm_g, "b_k_norm_g": b_k_norm_g, "b_w_out": b_w_out}


def _fwd_reference(x, a_norm_g, a_w_in, a_v_norm_g, a_w_s, a_b_s, a_w_out,
              b_norm_g, b_w_in, b_f_bias, b_q_norm_g, b_k_norm_g, b_w_out):
    for i in range(DEPTH):
        j = i // N_MIXERS
        if i % N_MIXERS == 0:
            x = x + spatial_gating_layer(x, a_norm_g[j], a_w_in[j], a_v_norm_g[j],
                                         a_w_s[j], a_b_s[j], a_w_out[j])
        else:
            x = x + forgetting_attention_layer(x, b_norm_g[j], b_w_in[j], b_f_bias[j],
                                               b_q_norm_g[j], b_k_norm_g[j], b_w_out[j])
    return x


import jax as _jax
import jax.numpy as _jnp

TWIN_FORMAT = 'train_step'
FWD_PARAMS = ['x', 'a_norm_g', 'a_w_in', 'a_v_norm_g', 'a_w_s', 'a_b_s', 'a_w_out', 'b_norm_g', 'b_w_in', 'b_f_bias', 'b_q_norm_g', 'b_k_norm_g', 'b_w_out']
TWIN_WEIGHTS = ['a_norm_g', 'a_w_in', 'a_v_norm_g', 'a_w_s', 'a_b_s', 'a_w_out', 'b_norm_g', 'b_w_in', 'b_f_bias', 'b_q_norm_g', 'b_k_norm_g', 'b_w_out']
TWIN_DIFF_INPUT = 'x'
TWIN_INPUTS = ['x', 'a_norm_g', 'a_w_in', 'a_v_norm_g', 'a_w_s', 'a_b_s', 'a_w_out', 'b_norm_g', 'b_w_in', 'b_f_bias', 'b_q_norm_g', 'b_k_norm_g', 'b_w_out', 'loss_target', 'm_a_norm_g', 'm_a_w_in', 'm_a_v_norm_g', 'm_a_w_s', 'm_a_b_s', 'm_a_w_out', 'm_b_norm_g', 'm_b_w_in', 'm_b_f_bias', 'm_b_q_norm_g', 'm_b_k_norm_g', 'm_b_w_out', 'v_a_norm_g', 'v_a_w_in', 'v_a_v_norm_g', 'v_a_w_s', 'v_a_b_s', 'v_a_w_out', 'v_b_norm_g', 'v_b_w_in', 'v_b_f_bias', 'v_b_q_norm_g', 'v_b_k_norm_g', 'v_b_w_out']
TWIN_OUTPUTS = ['loss', 'grad_x', 'grad_a_norm_g', 'grad_a_w_in', 'grad_a_v_norm_g', 'grad_a_w_s', 'grad_a_b_s', 'grad_a_w_out', 'grad_b_norm_g', 'grad_b_w_in', 'grad_b_f_bias', 'grad_b_q_norm_g', 'grad_b_k_norm_g', 'grad_b_w_out', 'delta_a_norm_g', 'delta_a_w_in', 'delta_a_v_norm_g', 'delta_a_w_s', 'delta_a_b_s', 'delta_a_w_out', 'delta_b_norm_g', 'delta_b_w_in', 'delta_b_f_bias', 'delta_b_q_norm_g', 'delta_b_k_norm_g', 'delta_b_w_out', 'new_m_a_norm_g', 'new_m_a_w_in', 'new_m_a_v_norm_g', 'new_m_a_w_s', 'new_m_a_b_s', 'new_m_a_w_out', 'new_m_b_norm_g', 'new_m_b_w_in', 'new_m_b_f_bias', 'new_m_b_q_norm_g', 'new_m_b_k_norm_g', 'new_m_b_w_out', 'new_v_a_norm_g', 'new_v_a_w_in', 'new_v_a_v_norm_g', 'new_v_a_w_s', 'new_v_a_b_s', 'new_v_a_w_out', 'new_v_b_norm_g', 'new_v_b_w_in', 'new_v_b_f_bias', 'new_v_b_q_norm_g', 'new_v_b_k_norm_g', 'new_v_b_w_out']
TWIN_LEAF_KINDS = {'loss': 'loss', 'grad_x': 'grad_x', 'grad_a_norm_g': 'grad_w', 'grad_a_w_in': 'grad_w', 'grad_a_v_norm_g': 'grad_w', 'grad_a_w_s': 'grad_w', 'grad_a_b_s': 'grad_w', 'grad_a_w_out': 'grad_w', 'grad_b_norm_g': 'grad_w', 'grad_b_w_in': 'grad_w', 'grad_b_f_bias': 'grad_w', 'grad_b_q_norm_g': 'grad_w', 'grad_b_k_norm_g': 'grad_w', 'grad_b_w_out': 'grad_w', 'delta_a_norm_g': 'delta_w', 'delta_a_w_in': 'delta_w', 'delta_a_v_norm_g': 'delta_w', 'delta_a_w_s': 'delta_w', 'delta_a_b_s': 'delta_w', 'delta_a_w_out': 'delta_w', 'delta_b_norm_g': 'delta_w', 'delta_b_w_in': 'delta_w', 'delta_b_f_bias': 'delta_w', 'delta_b_q_norm_g': 'delta_w', 'delta_b_k_norm_g': 'delta_w', 'delta_b_w_out': 'delta_w', 'new_m_a_norm_g': 'new_m', 'new_m_a_w_in': 'new_m', 'new_m_a_v_norm_g': 'new_m', 'new_m_a_w_s': 'new_m', 'new_m_a_b_s': 'new_m', 'new_m_a_w_out': 'new_m', 'new_m_b_norm_g': 'new_m', 'new_m_b_w_in': 'new_m', 'new_m_b_f_bias': 'new_m', 'new_m_b_q_norm_g': 'new_m', 'new_m_b_k_norm_g': 'new_m', 'new_m_b_w_out': 'new_m', 'new_v_a_norm_g': 'new_v', 'new_v_a_w_in': 'new_v', 'new_v_a_v_norm_g': 'new_v', 'new_v_a_w_s': 'new_v', 'new_v_a_b_s': 'new_v', 'new_v_a_w_out': 'new_v', 'new_v_b_norm_g': 'new_v', 'new_v_b_w_in': 'new_v', 'new_v_b_f_bias': 'new_v', 'new_v_b_q_norm_g': 'new_v', 'new_v_b_k_norm_g': 'new_v', 'new_v_b_w_out': 'new_v'}


def _forward(args):
    return _fwd_reference(*[args[k] for k in FWD_PARAMS])


def _output_shape():
    def fwd():
        inp = _fwd_setup_inputs(0)
        return _fwd_reference(*[inp[k] for k in FWD_PARAMS])
    out = _jax.eval_shape(fwd)
    return out.shape, out.dtype

N_MICROBATCH = 1
ADAM_LR = 0.001
ADAM_B1 = 0.9
ADAM_B2 = 0.999
ADAM_EPS = 1e-08
ADAM_WD = 0.01
ADAM_STEP = 10
PER_EXAMPLE_BATCH_AXIS = {'x': 0, 'loss_target': 0}
SHARED_INPUTS = []
_WEIGHT_DTYPES = {'a_norm_g': _jnp.float32, 'a_w_in': _jnp.float32, 'a_v_norm_g': _jnp.float32, 'a_w_s': _jnp.float32, 'a_b_s': _jnp.float32, 'a_w_out': _jnp.float32, 'b_norm_g': _jnp.float32, 'b_w_in': _jnp.float32, 'b_f_bias': _jnp.float32, 'b_q_norm_g': _jnp.float32, 'b_k_norm_g': _jnp.float32, 'b_w_out': _jnp.float32}
MOMENT_SCALE = {'a_norm_g': 1.734188e+01, 'a_w_in': 1.551961e-01, 'a_v_norm_g': 1.165812e+00, 'a_w_s': 1.552756e+00, 'a_b_s': 5.131409e+00, 'a_w_out': 3.341769e-01, 'b_norm_g': 2.213291e+00, 'b_w_in': 1.033590e-01, 'b_f_bias': 4.482831e+01, 'b_q_norm_g': 7.609563e+00, 'b_k_norm_g': 7.576061e+00, 'b_w_out': 5.918755e-02}


def _to_microbatches(a, axis):
    t = _jnp.moveaxis(a, axis, 0)
    t = t.reshape((N_MICROBATCH, t.shape[0] // N_MICROBATCH) + t.shape[1:])
    return _jnp.moveaxis(t, 1, axis + 1)


def setup_inputs(seed: int = 0) -> dict:
    inp = _fwd_setup_inputs(seed)
    key = _jax.random.fold_in(_jax.random.key(seed), 7919)
    shape, _ = _output_shape()
    out = dict(inp)
    out["loss_target"] = _jax.random.normal(_jax.random.fold_in(key, 0), shape, _jnp.float32)
    for i, name in enumerate(TWIN_WEIGHTS):
        w = inp[name].astype(_jnp.float32)
        if MOMENT_SCALE is None:
            s = _jnp.sqrt(_jnp.mean(_jnp.square(w)) + 1e-30)
        else:
            s = MOMENT_SCALE[name]
        km, kv = _jax.random.split(_jax.random.fold_in(key, i + 1))
        out[name] = w
        out["m_" + name] = s * _jax.random.normal(km, w.shape, _jnp.float32)
        out["v_" + name] = (s * s) * _jax.random.uniform(kv, w.shape, _jnp.float32, 0.5, 1.5)
    if N_MICROBATCH > 1:
        for name, axis in PER_EXAMPLE_BATCH_AXIS.items():
            out[name] = _to_microbatches(out[name], axis)
    return {'x': out['x'], 'a_norm_g': out['a_norm_g'], 'a_w_in': out['a_w_in'], 'a_v_norm_g': out['a_v_norm_g'], 'a_w_s': out['a_w_s'], 'a_b_s': out['a_b_s'], 'a_w_out': out['a_w_out'], 'b_norm_g': out['b_norm_g'], 'b_w_in': out['b_w_in'], 'b_f_bias': out['b_f_bias'], 'b_q_norm_g': out['b_q_norm_g'], 'b_k_norm_g': out['b_k_norm_g'], 'b_w_out': out['b_w_out'], 'loss_target': out['loss_target'], 'm_a_norm_g': out['m_a_norm_g'], 'm_a_w_in': out['m_a_w_in'], 'm_a_v_norm_g': out['m_a_v_norm_g'], 'm_a_w_s': out['m_a_w_s'], 'm_a_b_s': out['m_a_b_s'], 'm_a_w_out': out['m_a_w_out'], 'm_b_norm_g': out['m_b_norm_g'], 'm_b_w_in': out['m_b_w_in'], 'm_b_f_bias': out['m_b_f_bias'], 'm_b_q_norm_g': out['m_b_q_norm_g'], 'm_b_k_norm_g': out['m_b_k_norm_g'], 'm_b_w_out': out['m_b_w_out'], 'v_a_norm_g': out['v_a_norm_g'], 'v_a_w_in': out['v_a_w_in'], 'v_a_v_norm_g': out['v_a_v_norm_g'], 'v_a_w_s': out['v_a_w_s'], 'v_a_b_s': out['v_a_b_s'], 'v_a_w_out': out['v_a_w_out'], 'v_b_norm_g': out['v_b_norm_g'], 'v_b_w_in': out['v_b_w_in'], 'v_b_f_bias': out['v_b_f_bias'], 'v_b_q_norm_g': out['v_b_q_norm_g'], 'v_b_k_norm_g': out['v_b_k_norm_g'], 'v_b_w_out': out['v_b_w_out']}


def _loss(weights, diff, rest, loss_target):
    with _jax.named_scope("forward"):
        args = {**rest, TWIN_DIFF_INPUT: diff, **{k: w.astype(_WEIGHT_DTYPES[k]) for k, w in weights.items()}}
        y = _forward(args)
    with _jax.named_scope("loss_head"):
        err = _jnp.square(y.astype(_jnp.float32) - loss_target)
        return 0.5 * _jnp.sum(_jnp.mean(err, axis=-1)) if err.ndim else 0.5 * err


def _adamw(w, g, m, v):
    m = ADAM_B1 * m + (1.0 - ADAM_B1) * g
    v = ADAM_B2 * v + (1.0 - ADAM_B2) * _jnp.square(g)
    m_hat = m / (1.0 - ADAM_B1 ** ADAM_STEP)
    v_hat = v / (1.0 - ADAM_B2 ** ADAM_STEP)
    delta = -ADAM_LR * (m_hat / (_jnp.sqrt(v_hat) + ADAM_EPS) + ADAM_WD * w)
    return delta, m, v


def reference(x, a_norm_g, a_w_in, a_v_norm_g, a_w_s, a_b_s, a_w_out, b_norm_g, b_w_in, b_f_bias, b_q_norm_g, b_k_norm_g, b_w_out, loss_target, m_a_norm_g, m_a_w_in, m_a_v_norm_g, m_a_w_s, m_a_b_s, m_a_w_out, m_b_norm_g, m_b_w_in, m_b_f_bias, m_b_q_norm_g, m_b_k_norm_g, m_b_w_out, v_a_norm_g, v_a_w_in, v_a_v_norm_g, v_a_w_s, v_a_b_s, v_a_w_out, v_b_norm_g, v_b_w_in, v_b_f_bias, v_b_q_norm_g, v_b_k_norm_g, v_b_w_out):
    given = dict(x=x, a_norm_g=a_norm_g, a_w_in=a_w_in, a_v_norm_g=a_v_norm_g, a_w_s=a_w_s, a_b_s=a_b_s, a_w_out=a_w_out, b_norm_g=b_norm_g, b_w_in=b_w_in, b_f_bias=b_f_bias, b_q_norm_g=b_q_norm_g, b_k_norm_g=b_k_norm_g, b_w_out=b_w_out, loss_target=loss_target, m_a_norm_g=m_a_norm_g, m_a_w_in=m_a_w_in, m_a_v_norm_g=m_a_v_norm_g, m_a_w_s=m_a_w_s, m_a_b_s=m_a_b_s, m_a_w_out=m_a_w_out, m_b_norm_g=m_b_norm_g, m_b_w_in=m_b_w_in, m_b_f_bias=m_b_f_bias, m_b_q_norm_g=m_b_q_norm_g, m_b_k_norm_g=m_b_k_norm_g, m_b_w_out=m_b_w_out, v_a_norm_g=v_a_norm_g, v_a_w_in=v_a_w_in, v_a_v_norm_g=v_a_v_norm_g, v_a_w_s=v_a_w_s, v_a_b_s=v_a_b_s, v_a_w_out=v_a_w_out, v_b_norm_g=v_b_norm_g, v_b_w_in=v_b_w_in, v_b_f_bias=v_b_f_bias, v_b_q_norm_g=v_b_q_norm_g, v_b_k_norm_g=v_b_k_norm_g, v_b_w_out=v_b_w_out)
    weights = {n: given[n] for n in TWIN_WEIGHTS}
    shared = {n: given[n] for n in SHARED_INPUTS}
    per_example = {n: given[n] for n in ['x']}
    grad_fn = _jax.value_and_grad(_loss, argnums=(0, 1))

    def one_microbatch(ex, loss_target):
        ex = dict(ex)
        diff = ex.pop(TWIN_DIFF_INPUT)
        return grad_fn(weights, diff, {**shared, **ex}, loss_target)

    if N_MICROBATCH == 1:
        loss, (grad_w, grad_x) = one_microbatch(per_example, given["loss_target"])
    else:
        def body(carry, xs):
            loss_sum, grad_sum = carry
            l_k, (gw_k, gx_k) = one_microbatch(xs[0], xs[1])
            with _jax.named_scope("update"):
                return (loss_sum + l_k, _jax.tree.map(_jnp.add, grad_sum, gw_k)), gx_k

        init = (_jnp.zeros((), _jnp.float32), _jax.tree.map(_jnp.zeros_like, weights))
        (loss, grad_w), grad_x = _jax.lax.scan(body, init, (per_example, given["loss_target"]))
    with _jax.named_scope("update"):
        delta_w, new_m, new_v = {}, {}, {}
        for n in TWIN_WEIGHTS:
            delta_w[n], new_m[n], new_v[n] = _adamw(weights[n], grad_w[n], given["m_" + n], given["v_" + n])
    return (loss, grad_x, *[grad_w[n] for n in TWIN_WEIGHTS], *[delta_w[n] for n in TWIN_WEIGHTS],
            *[new_m[n] for n in TWIN_WEIGHTS], *[new_v[n] for n in TWIN_WEIGHTS])
```

```python
import functools
import math

import numpy as np
import jax
import jax.numpy as jnp
from jax import lax
from jax.experimental import pallas as pl
from jax.experimental.pallas import tpu as pltpu

F32 = jnp.float32
BF16 = jnp.bfloat16
MESH = pl.DeviceIdType.MESH

EPS = 1e-6
CHUNK = 128
HEAD = 128
LANES = 128
N_CHIPS = 4
VMEM_LIMIT = 56 * 1024 * 1024

ADAM_LR = 0.001
ADAM_B1 = 0.9
ADAM_B2 = 0.999
ADAM_EPS = 1e-08
ADAM_WD = 0.01
ADAM_STEP = 10

_NT = (((1,), (1,)), ((), ()))
_TN = (((0,), (0,)), ((), ()))
_GELU_C = math.sqrt(2.0 / math.pi)

HBM_SPEC = pl.BlockSpec(memory_space=pltpu.HBM)


def _params(*sem):
    return pltpu.CompilerParams(dimension_semantics=sem, vmem_limit_bytes=VMEM_LIMIT)


def _tile(dim, pref, unit=LANES):
    t = (min(pref, dim) // unit) * unit
    while t >= unit:
        if dim % t == 0:
            return t
        t -= unit
    return dim


def _gelu(x):
    return 0.5 * x * (1.0 + jnp.tanh(_GELU_C * (x + 0.044715 * (x * x * x))))


def _gelu_and_grad(x):
    x2 = x * x
    t = jnp.tanh(_GELU_C * (x + 0.044715 * (x2 * x)))
    val = 0.5 * x * (1.0 + t)
    grad = 0.5 * (1.0 + t) + 0.5 * x * (1.0 - t * t) * (_GELU_C * (1.0 + 3.0 * 0.044715 * x2))
    return val, grad


def _sigmoid(x):
    return 1.0 / (1.0 + jnp.exp(-x))


class _View:
    def __init__(self, arr, kind="2d", lead=()):
        self.arr, self.kind, self.lead = arr, kind, tuple(lead)
        shp = arr.shape[len(self.lead):]
        if kind == "2d":
            self.R, self.C = shp
        elif kind == "col":
            self.nb, self.R, self.cb = shp
            self.C = self.nb * self.cb
        else:
            self.nb, self.rb, self.C = shp
            self.R = self.nb * self.rb

    def fit(self, tr, tc):
        if self.kind == "col":
            tc = _tile(self.cb, tc)
        elif self.kind == "row":
            tr = _tile(self.rb, tr, unit=8)
        return tr, tc

    def spec(self, tr, tc, rc_of_grid):
        lead = self.lead
        sq = (None,) * len(lead)
        if self.kind == "2d":
            return pl.BlockSpec(sq + (tr, tc), lambda *g: lead + tuple(rc_of_grid(*g)))
        if self.kind == "col":
            q = self.cb // tc

            def im(*g):
                r, c = rc_of_grid(*g)
                return lead + (c // q, r, c % q)

            return pl.BlockSpec(sq + (None, tr, tc), im)
        q = self.rb // tr

        def im(*g):
            r, c = rc_of_grid(*g)
            return lead + (r // q, r % q, c)

        return pl.BlockSpec(sq + (None, tr, tc), im)


def _matmul(name, a, b, *, ta=False, tb=False, out_dtype=F32, tm=1024, tn=1024, tk=1024,
            out_colblocks=None, residual=None):
    M, K = (a.C, a.R) if ta else (a.R, a.C)
    N, K2 = (b.R, b.C) if tb else (b.C, b.R)
    assert K == K2, (name, K, K2)
    tm, tn, tk = _tile(M, tm), _tile(N, tn), _tile(K, tk)
    if ta:
        tk, tm = a.fit(tk, tm)
    else:
        tm, tk = a.fit(tm, tk)
    if tb:
        tn, tk2 = b.fit(tn, tk)
    else:
        tk2, tn = b.fit(tk, tn)
    if tk2 != tk:
        tk = min(tk, tk2)
        if ta:
            tk, tm = a.fit(tk, tm)
        else:
            tm, tk = a.fit(tm, tk)
    if out_colblocks:
        tn = _tile(N // out_colblocks, tn)
    assert M % tm == 0 and N % tn == 0 and K % tk == 0, (name, M, N, K, tm, tn, tk)
    nk = K // tk
    dims = (((0 if ta else 1,), (1 if tb else 0,)), ((), ()))

    def body(*refs):
        if residual is not None:
            a_ref, b_ref, r_ref, o_ref, acc_ref = refs
        else:
            a_ref, b_ref, o_ref, acc_ref = refs
            r_ref = None
        k = pl.program_id(2)
        part = lax.dot_general(a_ref[...], b_ref[...], dims, preferred_element_type=F32)

        def finish(total):
            if r_ref is not None:
                total = total + r_ref[...]
            o_ref[...] = total.astype(out_dtype)

        if nk == 1:
            finish(part)
        else:
            @pl.when(k == 0)
            def _():
                acc_ref[...] = part

            @pl.when(jnp.logical_and(k > 0, k < nk - 1))
            def _():
                acc_ref[...] += part

            @pl.when(k == nk - 1)
            def _():
                finish(acc_ref[...] + part)

    a_spec = a.spec(tk, tm, lambda i, j, k: (k, i)) if ta else a.spec(tm, tk, lambda i, j, k: (i, k))
    b_spec = b.spec(tn, tk, lambda i, j, k: (j, k)) if tb else b.spec(tk, tn, lambda i, j, k: (k, j))
    in_specs, args = [a_spec, b_spec], [a.arr, b.arr]
    if residual is not None:
        in_specs.append(pl.BlockSpec((tm, tn), lambda i, j, k: (i, j)))
        args.append(residual)
    if out_colblocks:
        q = (N // out_colblocks) // tn
        out_shape = jax.ShapeDtypeStruct((out_colblocks, M, N // out_colblocks), out_dtype)
        out_spec = pl.BlockSpec((None, tm, tn), lambda i, j, k: (j // q, i, j % q))
    else:
        out_shape = jax.ShapeDtypeStruct((M, N), out_dtype)
        out_spec = pl.BlockSpec((tm, tn), lambda i, j, k: (i, j))
    acc_shape = (tm, tn) if nk > 1 else (8, LANES)
    return pl.pallas_call(
        body, name=name, grid=(M // tm, N // tn, nk), in_specs=in_specs, out_specs=out_spec,
        out_shape=out_shape, scratch_shapes=[pltpu.VMEM(acc_shape, F32)],
        compiler_params=_params("arbitrary", "arbitrary", "arbitrary"),
    )(*args)


def _rmsnorm_fwd(name, x, gain):
    S, D = x.shape
    tr = _tile(S, 512, unit=8)

    def body(x_ref, g_ref, h_ref):
        xv = x_ref[...]
        r = lax.rsqrt(jnp.mean(xv * xv, axis=-1, keepdims=True) + EPS)
        h_ref[...] = (xv * r * g_ref[...]).astype(BF16)

    return pl.pallas_call(
        body, name=name, grid=(S // tr,),
        in_specs=[pl.BlockSpec((tr, D), lambda i: (i, 0)), pl.BlockSpec((1, D), lambda i: (0, 0))],
        out_specs=pl.BlockSpec((tr, D), lambda i: (i, 0)),
        out_shape=jax.ShapeDtypeStruct((S, D), BF16),
        compiler_params=_params("arbitrary"),
    )(x, gain.reshape(1, D))


def _rmsnorm_bwd(name, x, gain, dh, g_res):
    S, D = x.shape
    tr = _tile(S, 256, unit=8)

    def body(x_ref, g_ref, dh_ref, res_ref, dx_ref, dxb_ref, dg_ref):
        i = pl.program_id(0)
        xv = x_ref[...]
        r = lax.rsqrt(jnp.mean(xv * xv, axis=-1, keepdims=True) + EPS)
        xhat = xv * r
        dhv = dh_ref[...]
        part = jnp.sum(dhv * xhat, axis=0, keepdims=True)

        @pl.when(i == 0)
        def _():
            dg_ref[...] = part

        @pl.when(i > 0)
        def _():
            dg_ref[...] += part

        dxhat = dhv * g_ref[...]
        dx = res_ref[...] + r * (dxhat - xhat * jnp.mean(dxhat * xhat, axis=-1, keepdims=True))
        dx_ref[...] = dx
        dxb_ref[...] = dx.astype(BF16)

    row = pl.BlockSpec((tr, D), lambda i: (i, 0))
    vec = pl.BlockSpec((1, D), lambda i: (0, 0))
    return pl.pallas_call(
        body, name=name, grid=(S // tr,), in_specs=[row, vec, row, row], out_specs=[row, row, vec],
        out_shape=[jax.ShapeDtypeStruct((S, D), F32), jax.ShapeDtypeStruct((S, D), BF16),
                   jax.ShapeDtypeStruct((1, D), F32)],
        compiler_params=_params("arbitrary"),
    )(x, gain.reshape(1, D), dh, g_res)


def _loss_grad(x, target):
    S, D = x.shape
    tr = _tile(S, 512, unit=8)

    def body(x_ref, t_ref, g_ref, gb_ref, l_ref):
        i = pl.program_id(0)
        e = x_ref[...] - t_ref[...]
        g = e * (1.0 / D)
        g_ref[...] = g
        gb_ref[...] = g.astype(BF16)
        part = jnp.sum(e * e, axis=0, keepdims=True)

        @pl.when(i == 0)
        def _():
            l_ref[...] = part

        @pl.when(i > 0)
        def _():
            l_ref[...] += part

    row = pl.BlockSpec((tr, D), lambda i: (i, 0))
    vec = pl.BlockSpec((1, D), lambda i: (0, 0))
    return pl.pallas_call(
        body, name="loss_grad", grid=(S // tr,), in_specs=[row, row], out_specs=[row, row, vec],
        out_shape=[jax.ShapeDtypeStruct((S, D), F32), jax.ShapeDtypeStruct((S, D), BF16),
                   jax.ShapeDtypeStruct((1, D), F32)],
        compiler_params=_params("arbitrary"),
    )(x, target)


def _gate_fwd(name, p, v_gain, wc, bs_t):
    S, W3 = p.shape
    W = W3 // 3
    G = wc.shape[0]
    gd = W // G

    def body(p_ref, gv_ref, wc_ref, bs_ref, y_ref):
        vg = _gelu(p_ref[:, W:2 * W])
        r = lax.rsqrt(jnp.mean(vg * vg, axis=-1, keepdims=True) + EPS)
        vb = (vg * r * gv_ref[...]).astype(BF16)
        zp = p_ref[:, 2 * W:]
        gate = _gelu(p_ref[:, :W]) * (zp * _sigmoid(zp))
        for g in range(G):
            sl = slice(g * gd, (g + 1) * gd)
            mixed = jnp.dot(wc_ref[g], vb[:, sl], preferred_element_type=F32) + bs_ref[:, g:g + 1]
            y_ref[:, sl] = (gate[:, sl] * mixed).astype(BF16)

    return pl.pallas_call(
        body, name=name, grid=(S // CHUNK,),
        in_specs=[pl.BlockSpec((CHUNK, W3), lambda i: (i, 0)), pl.BlockSpec((1, W), lambda i: (0, 0)),
                  pl.BlockSpec((G, CHUNK, CHUNK), lambda i: (0, 0, 0)), pl.BlockSpec((CHUNK, G), lambda i: (0, 0))],
        out_specs=pl.BlockSpec((CHUNK, W), lambda i: (i, 0)),
        out_shape=jax.ShapeDtypeStruct((S, W), BF16),
        compiler_params=_params("arbitrary"),
    )(p, v_gain.reshape(1, W), wc, bs_t)


def _gate_bwd(name, p, dy, v_gain, wc, wc_t, bs_t):
    S, W3 = p.shape
    W = W3 // 3
    G = wc.shape[0]
    gd = W // G

    def body(p_ref, dy_ref, gv_ref, wc_ref, wct_ref, bs_ref, dp_ref, dws_ref, dbs_ref, dgv_ref, dv_scr):
        i = pl.program_id(0)

        @pl.when(i == 0)
        def _():
            dws_ref[...] = jnp.zeros_like(dws_ref)
            dbs_ref[...] = jnp.zeros_like(dbs_ref)
            dgv_ref[...] = jnp.zeros_like(dgv_ref)

        gu, dgu = _gelu_and_grad(p_ref[:, :W])
        vg, dvg_dv = _gelu_and_grad(p_ref[:, W:2 * W])
        zp = p_ref[:, 2 * W:]
        sig = _sigmoid(zp)
        sz = zp * sig
        dsz = sig * (1.0 + zp * (1.0 - sig))
        r = lax.rsqrt(jnp.mean(vg * vg, axis=-1, keepdims=True) + EPS)
        vhat = vg * r
        gv = gv_ref[...]
        vb = (vhat * gv).astype(BF16)
        dy = dy_ref[...].astype(F32)
        lane = lax.broadcasted_iota(jnp.int32, (CHUNK, LANES), 1)
        dbs = jnp.zeros((CHUNK, LANES), F32)
        for g in range(G):
            sl = slice(g * gd, (g + 1) * gd)
            vsl = vb[:, sl]
            mixed = jnp.dot(wc_ref[g], vsl, preferred_element_type=F32) + bs_ref[:, g:g + 1]
            dyg, gug, szg = dy[:, sl], gu[:, sl], sz[:, sl]
            dp_ref[:, sl] = (dyg * mixed * szg * dgu[:, sl]).astype(BF16)
            dp_ref[:, 2 * W + g * gd:2 * W + (g + 1) * gd] = (dyg * gug * mixed * dsz[:, sl]).astype(BF16)
            dm = dyg * gug * szg
            dmb = dm.astype(BF16)
            dws_ref[g] += lax.dot_general(dmb, vsl, _NT, preferred_element_type=F32)
            dbs = dbs + jnp.where(lane == g, jnp.sum(dm, axis=1, keepdims=True), 0.0)
            dv_scr[:, sl] = jnp.dot(wct_ref[g], dmb, preferred_element_type=F32)
        dbs_ref[...] += dbs
        dv = dv_scr[...]
        dgv_ref[...] += jnp.sum(dv * vhat, axis=0, keepdims=True)
        dvhat = dv * gv
        dvg = r * (dvhat - vhat * jnp.mean(dvhat * vhat, axis=-1, keepdims=True))
        dp_ref[:, W:2 * W] = (dvg * dvg_dv).astype(BF16)

    return pl.pallas_call(
        body, name=name, grid=(S // CHUNK,),
        in_specs=[pl.BlockSpec((CHUNK, W3), lambda i: (i, 0)), pl.BlockSpec((CHUNK, W), lambda i: (i, 0)),
                  pl.BlockSpec((1, W), lambda i: (0, 0)),
                  pl.BlockSpec((G, CHUNK, CHUNK), lambda i: (0, 0, 0)),
                  pl.BlockSpec((G, CHUNK, CHUNK), lambda i: (0, 0, 0)),
                  pl.BlockSpec((CHUNK, G), lambda i: (0, 0))],
        out_specs=[pl.BlockSpec((CHUNK, W3), lambda i: (i, 0)),
                   pl.BlockSpec((G, CHUNK, CHUNK), lambda i: (0, 0, 0)),
                   pl.BlockSpec((CHUNK, LANES), lambda i: (0, 0)),
                   pl.BlockSpec((1, W), lambda i: (0, 0))],
        out_shape=[jax.ShapeDtypeStruct((S, W3), BF16), jax.ShapeDtypeStruct((G, CHUNK, CHUNK), F32),
                   jax.ShapeDtypeStruct((CHUNK, LANES), F32), jax.ShapeDtypeStruct((1, W), F32)],
        scratch_shapes=[pltpu.VMEM((CHUNK, W), F32)],
        compiler_params=_params("arbitrary"),
    )(p, dy, v_gain.reshape(1, W), wc, wc_t, bs_t)


def _qkv_prep(name, proj, q_gain, k_gain, H):
    S = proj.shape[0]
    HW = H * HEAD
    tr = _tile(S, 256, unit=8)

    def body(q_ref, k_ref, v_ref, gq_ref, gk_ref, qn_ref, kn_ref, vb_ref):
        for src, gain, dst in ((q_ref, gq_ref, qn_ref), (k_ref, gk_ref, kn_ref)):
            for h in range(H):
                sl = slice(h * HEAD, (h + 1) * HEAD)
                t = src[:, sl]
                r = lax.rsqrt(jnp.mean(t * t, axis=-1, keepdims=True) + EPS)
                dst[:, sl] = (t * r * gain[...]).astype(BF16)
        vb_ref[...] = v_ref[...].astype(BF16)

    col = lambda c: pl.BlockSpec((tr, HW), lambda i: (i, c))
    vec = pl.BlockSpec((1, HEAD), lambda i: (0, 0))
    return pl.pallas_call(
        body, name=name, grid=(S // tr,), in_specs=[col(0), col(1), col(2), vec, vec],
        out_specs=[col(0)] * 3, out_shape=[jax.ShapeDtypeStruct((S, HW), BF16)] * 3,
        compiler_params=_params("arbitrary"),
    )(proj, proj, proj, q_gain.reshape(1, HEAD), k_gain.reshape(1, HEAD))


def _split3(x):
    hi = x.astype(BF16)
    r1 = x - hi.astype(F32)
    mid = r1.astype(BF16)
    lo = (r1 - mid.astype(F32)).astype(BF16)
    return hi, mid, lo


def _tri_sum(tri, x):
    hi, mid, lo = _split3(x)
    d = lambda t: jnp.dot(tri, t, preferred_element_type=F32)
    return d(hi) + (d(mid) + d(lo))


def _log_sigmoid(x):
    return jnp.minimum(x, 0.0) - jnp.log(1.0 + jnp.exp(-jnp.abs(x)))


def _fox_cum(name, f, bias):
    S = f.shape[0]
    tb = _tile(S, 256, unit=8)

    def body(f_ref, b_ref, c_ref):
        rr = lax.broadcasted_iota(jnp.int32, (tb, tb), 0)
        cc = lax.broadcasted_iota(jnp.int32, (tb, tb), 1)
        tri = (rr >= cc).astype(BF16)

        def step(t, carry):
            off = pl.multiple_of(t * tb, tb)
            lf = _log_sigmoid(f_ref[pl.ds(off, tb), :] + b_ref[...])
            c = _tri_sum(tri, lf) + carry
            c_ref[pl.ds(off, tb), :] = c
            return c[tb - 1:tb, :]

        lax.fori_loop(0, S // tb, step, jnp.zeros((1, LANES), F32))

    return pl.pallas_call(
        body, name=name, out_shape=jax.ShapeDtypeStruct((S, LANES), F32),
        in_specs=[pl.BlockSpec(memory_space=pltpu.VMEM)] * 2, out_specs=pl.BlockSpec(memory_space=pltpu.VMEM),
        compiler_params=pltpu.CompilerParams(vmem_limit_bytes=VMEM_LIMIT),
    )(f, bias)


def _fox_cum_bwd(name, dc_key, dc_query, f, bias):
    S = f.shape[0]
    tb = _tile(S, 256, unit=8)
    nb = S // tb

    def body(dck_ref, dcq_ref, f_ref, b_ref, df_ref, db_ref):
        rr = lax.broadcasted_iota(jnp.int32, (tb, tb), 0)
        cc = lax.broadcasted_iota(jnp.int32, (tb, tb), 1)
        tri = (rr <= cc).astype(BF16)

        def step(t, carry):
            tail, dbias = carry
            off = pl.multiple_of((nb - 1 - t) * tb, tb)
            dlf = _tri_sum(tri, dck_ref[pl.ds(off, tb), :] + dcq_ref[pl.ds(off, tb), :]) + tail
            d = dlf * _sigmoid(-(f_ref[pl.ds(off, tb), :] + b_ref[...]))
            df_ref[pl.ds(off, tb), :] = d.astype(BF16)
            return dlf[0:1, :], dbias + jnp.sum(d, axis=0, keepdims=True)

        z = jnp.zeros((1, LANES), F32)
        _, dbias = lax.fori_loop(0, nb, step, (z, z))
        db_ref[...] = dbias

    vm = pl.BlockSpec(memory_space=pltpu.VMEM)
    return pl.pallas_call(
        body, name=name, out_shape=[jax.ShapeDtypeStruct((S, LANES), BF16), jax.ShapeDtypeStruct((1, LANES), F32)],
        in_specs=[vm] * 4, out_specs=[vm] * 2,
        compiler_params=pltpu.CompilerParams(vmem_limit_bytes=VMEM_LIMIT),
    )(dc_key, dc_query, f, bias)


def _attn_fwd(name, qn, kn, vb, c_col, c_blk, proj, H, tq):
    S = qn.shape[0]
    HW = H * HEAD
    nq = S // tq
    scale = HEAD ** -0.5

    def body(q_ref, k_ref, v_ref, cq_ref, ck_ref, z_ref, o_ref, y_ref, lse_ref):
        i = pl.program_id(1)
        q = q_ref[...]
        cq = cq_ref[...]

        def step(j, carry, masked):
            m, l, acc = carry
            off = pl.multiple_of(j * tq, tq)
            k = k_ref[pl.ds(off, tq), :]
            v = v_ref[pl.ds(off, tq), :]
            s = lax.dot_general(q, k, _NT, preferred_element_type=F32) * scale + cq - ck_ref[pl.ds(j, 1), :]
            if masked:
                rr = lax.broadcasted_iota(jnp.int32, (tq, tq), 0)
                cc = lax.broadcasted_iota(jnp.int32, (tq, tq), 1)
                s = jnp.where(rr >= cc, s, -jnp.inf)
            m_new = jnp.maximum(m, jnp.max(s, axis=1, keepdims=True))
            alpha = jnp.exp(m - m_new)
            pr = jnp.exp(s - m_new)
            l = alpha * l + jnp.sum(pr, axis=1, keepdims=True)
            acc = alpha * acc + jnp.dot(pr.astype(BF16), v, preferred_element_type=F32)
            return m_new, l, acc

        init = (jnp.full((tq, 1), -jnp.inf, F32), jnp.zeros((tq, 1), F32), jnp.zeros((tq, HEAD), F32))
        carry = lax.fori_loop(0, i, lambda j, c: step(j, c, False), init)
        m, l, acc = step(i, carry, True)
        o = acc / l
        z = z_ref[...]
        o_ref[...] = o
        y_ref[...] = (o * (z * _sigmoid(z))).astype(BF16)
        lse_ref[...] = m + jnp.log(l)

    qspec = pl.BlockSpec((tq, HEAD), lambda h, i: (i, h))
    kvspec = pl.BlockSpec((S, HEAD), lambda h, i: (0, h))
    colspec = pl.BlockSpec((None, tq, 1), lambda h, i: (h, i, 0))
    return pl.pallas_call(
        body, name=name, grid=(H, nq),
        in_specs=[qspec, kvspec, kvspec, colspec, pl.BlockSpec((None, nq, tq), lambda h, i: (h, 0, 0)),
                  pl.BlockSpec((tq, HEAD), lambda h, i: (i, 3 * H + h))],
        out_specs=[qspec, qspec, colspec],
        out_shape=[jax.ShapeDtypeStruct((S, HW), F32), jax.ShapeDtypeStruct((S, HW), BF16),
                   jax.ShapeDtypeStruct((H, S, 1), F32)],
        compiler_params=_params("arbitrary", "arbitrary"),
    )(qn, kn, vb, c_col, c_blk, proj)


def _attn_bwd_prep(name, dy, o, proj, H):
    S, HW = o.shape
    tr = _tile(S, 256, unit=8)

    def body(dy_ref, o_ref, z_ref, do_ref, dz_ref, dl_ref):
        dy = dy_ref[...].astype(F32)
        z = z_ref[...]
        o = o_ref[...]
        sig = _sigmoid(z)
        do = dy * (z * sig)
        do_ref[...] = do.astype(BF16)
        dz_ref[...] = (dy * o * (sig * (1.0 + z * (1.0 - sig)))).astype(BF16)
        prod = do * o
        lane = lax.broadcasted_iota(jnp.int32, (tr, LANES), 1)
        acc = jnp.zeros((tr, LANES), F32)
        for h in range(H):
            acc = acc + jnp.where(lane == h, jnp.sum(prod[:, h * HEAD:(h + 1) * HEAD], axis=1, keepdims=True), 0.0)
        dl_ref[...] = acc

    row = pl.BlockSpec((tr, HW), lambda i: (i, 0))
    return pl.pallas_call(
        body, name=name, grid=(S // tr,),
        in_specs=[row, row, pl.BlockSpec((tr, HW), lambda i: (i, 3))],
        out_specs=[row, row, pl.BlockSpec((tr, LANES), lambda i: (i, 0))],
        out_shape=[jax.ShapeDtypeStruct((S, HW), BF16), jax.ShapeDtypeStruct((S, HW), BF16),
                   jax.ShapeDtypeStruct((S, LANES), F32)],
        compiler_params=_params("arbitrary"),
    )(dy, o, proj)


def _attn_bwd(name, qn, do, kn, vb, lse_blk, delta_blk, c_blk, c_col, H, tq):
    S = qn.shape[0]
    HW = H * HEAD
    nq = S // tq
    scale = HEAD ** -0.5

    def body(q_ref, do_ref, k_ref, v_ref, lse_ref, dl_ref, cq_ref, ck_ref, dq_ref, dk_ref, dv_ref, dc_ref, dcq_ref):
        j = pl.program_id(1)

        @pl.when(j == 0)
        def _():
            dq_ref[...] = jnp.zeros_like(dq_ref)
            dcq_ref[...] = jnp.zeros_like(dcq_ref)

        k = k_ref[...]
        v = v_ref[...]
        ck = ck_ref[...]

        def step(i, carry, masked):
            dk_acc, dv_acc, dc_acc = carry
            off = pl.multiple_of(i * tq, tq)
            q = q_ref[pl.ds(off, tq), :]
            do = do_ref[pl.ds(off, tq), :]
            st = lax.dot_general(k, q, _NT, preferred_element_type=F32) * scale + cq_ref[pl.ds(i, 1), :] - ck
            st = st - lse_ref[pl.ds(i, 1), :]
            if masked:
                rr = lax.broadcasted_iota(jnp.int32, (tq, tq), 0)
                cc = lax.broadcasted_iota(jnp.int32, (tq, tq), 1)
                st = jnp.where(cc >= rr, st, -jnp.inf)
            pt = jnp.exp(st)
            dv_acc = dv_acc + jnp.dot(pt.astype(BF16), do, preferred_element_type=F32)
            dpt = lax.dot_general(v, do, _NT, preferred_element_type=F32)
            dst = pt * (dpt - dl_ref[pl.ds(i, 1), :])
            dc_acc = dc_acc - jnp.sum(dst, axis=1, keepdims=True)
            dcq_ref[pl.ds(i, 1), :] += jnp.sum(dst, axis=0, keepdims=True)
            dsb = dst.astype(BF16)
            dk_acc = dk_acc + jnp.dot(dsb, q, preferred_element_type=F32)
            dq_ref[pl.ds(off, tq), :] += lax.dot_general(dsb, k, _TN, preferred_element_type=F32) * scale
            return dk_acc, dv_acc, dc_acc

        z = jnp.zeros((tq, HEAD), F32)
        carry = step(j, (z, z, jnp.zeros((tq, 1), F32)), True)
        dk_acc, dv_acc, dc_acc = lax.fori_loop(j + 1, nq, lambda i, c: step(i, c, False), carry)
        dk_ref[...] = dk_acc * scale
        dv_ref[...] = dv_acc
        dc_ref[...] = dc_acc

    full = pl.BlockSpec((S, HEAD), lambda h, j: (0, h))
    blk = pl.BlockSpec((tq, HEAD), lambda h, j: (j, h))
    rows = pl.BlockSpec((None, nq, tq), lambda h, j: (h, 0, 0))
    col = pl.BlockSpec((None, tq, 1), lambda h, j: (h, j, 0))
    return pl.pallas_call(
        body, name=name, grid=(H, nq),
        in_specs=[full, full, blk, blk, rows, rows, rows, col],
        out_specs=[full, blk, blk, col, rows],
        out_shape=[jax.ShapeDtypeStruct((S, HW), F32)] * 3 + [jax.ShapeDtypeStruct((H, S, 1), F32),
                                                               jax.ShapeDtypeStruct((H, nq, tq), F32)],
        compiler_params=_params("arbitrary", "arbitrary"),
    )(qn, do, kn, vb, lse_blk, delta_blk, c_blk, c_col)


def _qk_bwd(name, proj, dq, dk, dv, dz, q_gain, k_gain, H):
    S = proj.shape[0]
    HW = H * HEAD
    tr = _tile(S, 256, unit=8)

    def body(q_ref, k_ref, dq_ref, dk_ref, dv_ref, dz_ref, gq_ref, gk_ref, dp_ref, dgq_ref, dgk_ref):
        i = pl.program_id(0)

        @pl.when(i == 0)
        def _():
            dgq_ref[...] = jnp.zeros_like(dgq_ref)
            dgk_ref[...] = jnp.zeros_like(dgk_ref)

        for n, (src, dsrc, gain, dgain) in enumerate(((q_ref, dq_ref, gq_ref, dgq_ref), (k_ref, dk_ref, gk_ref, dgk_ref))):
            acc = jnp.zeros((1, HEAD), F32)
            for h in range(H):
                sl = slice(h * HEAD, (h + 1) * HEAD)
                t = src[:, sl]
                r = lax.rsqrt(jnp.mean(t * t, axis=-1, keepdims=True) + EPS)
                that = t * r
                dn = dsrc[:, sl]
                acc = acc + jnp.sum(dn * that, axis=0, keepdims=True)
                dhat = dn * gain[...]
                dt = r * (dhat - that * jnp.mean(dhat * that, axis=-1, keepdims=True))
                dp_ref[:, n * HW + h * HEAD:n * HW + (h + 1) * HEAD] = dt.astype(BF16)
            dgain[...] += acc
        dp_ref[:, 2 * HW:3 * HW] = dv_ref[...].astype(BF16)
        dp_ref[:, 3 * HW:] = dz_ref[...]

    col = lambda c: pl.BlockSpec((tr, HW), lambda i: (i, c))
    row = col(0)
    vec = pl.BlockSpec((1, HEAD), lambda i: (0, 0))
    return pl.pallas_call(
        body, name=name, grid=(S // tr,),
        in_specs=[col(0), col(1), row, row, row, row, vec, vec],
        out_specs=[pl.BlockSpec((tr, 4 * HW), lambda i: (i, 0)), vec, vec],
        out_shape=[jax.ShapeDtypeStruct((S, 4 * HW), BF16), jax.ShapeDtypeStruct((1, HEAD), F32),
                   jax.ShapeDtypeStruct((1, HEAD), F32)],
        compiler_params=_params("arbitrary"),
    )(proj, proj, dq, dk, dv, dz, q_gain.reshape(1, HEAD), k_gain.reshape(1, HEAD))


def _row_tile(R, C, budget_bytes=1 << 20):
    cap = max(8, budget_bytes // (4 * C))
    t = (min(cap, R) // 8) * 8
    while t >= 8:
        if R % t == 0:
            return t
        t -= 8
    return R


def _add_selected(name, g0, g1, other, sel):
    R, C = other.shape
    tr = _row_tile(R, C)

    def body(sel_ref, a0_ref, a1_ref, b_ref, o_ref):
        @pl.when(sel_ref[0] == 0)
        def _():
            o_ref[...] = a0_ref[...] + b_ref[...]

        @pl.when(sel_ref[0] != 0)
        def _():
            o_ref[...] = a1_ref[...] + b_ref[...]

    grid_spec = pltpu.PrefetchScalarGridSpec(
        num_scalar_prefetch=1, grid=(R // tr,),
        in_specs=[pl.BlockSpec((tr, C), lambda i, s: (i * (1 - s[0]), 0)),
                  pl.BlockSpec((tr, C), lambda i, s: (i * s[0], 0)),
                  pl.BlockSpec((tr, C), lambda i, s: (i, 0))],
        out_specs=pl.BlockSpec((tr, C), lambda i, s: (i, 0)))
    return pl.pallas_call(
        body, name=name, grid_spec=grid_spec, out_shape=jax.ShapeDtypeStruct((R, C), F32),
        compiler_params=_params("arbitrary"),
    )(sel, g0, g1, other)


def _sum_slots(name, slots):
    n, R, C = slots.shape
    tr = _row_tile(R, C, budget_bytes=(1 << 20) // 2)

    def body(s_ref, o_ref):
        acc = s_ref[0]
        for k in range(1, n):
            acc = acc + s_ref[k]
        o_ref[...] = acc

    return pl.pallas_call(
        body, name=name, grid=(R // tr,),
        in_specs=[pl.BlockSpec((n, tr, C), lambda i: (0, i, 0))],
        out_specs=pl.BlockSpec((tr, C), lambda i: (i, 0)),
        out_shape=jax.ShapeDtypeStruct((R, C), F32),
        compiler_params=_params("arbitrary"),
    )(slots)


def _adamw(name, w, g, m, v):
    R, C = w.shape
    tr = _row_tile(R, C, budget_bytes=(1 << 20) // 2)
    c1 = 1.0 - ADAM_B1 ** ADAM_STEP
    c2 = 1.0 - ADAM_B2 ** ADAM_STEP

    def body(w_ref, g_ref, m_ref, v_ref, d_ref, nm_ref, nv_ref):
        gv = g_ref[...]
        nm = ADAM_B1 * m_ref[...] + (1.0 - ADAM_B1) * gv
        nv = ADAM_B2 * v_ref[...] + (1.0 - ADAM_B2) * (gv * gv)
        m_hat = nm / c1
        v_hat = nv / c2
        d_ref[...] = -ADAM_LR * (m_hat / (jnp.sqrt(v_hat) + ADAM_EPS) + ADAM_WD * w_ref[...])
        nm_ref[...] = nm
        nv_ref[...] = nv

    blk = pl.BlockSpec((tr, C), lambda i: (i, 0))
    return pl.pallas_call(
        body, name=name, grid=(R // tr,), in_specs=[blk] * 4, out_specs=[blk] * 3,
        out_shape=[jax.ShapeDtypeStruct((R, C), F32)] * 3,
        compiler_params=_params("arbitrary"),
    )(w, g, m, v)


def _place():
    x, y, c = lax.axis_index("x"), lax.axis_index("y"), lax.axis_index("c")
    chips = [(1 - x, y), (x, 1 - y), (1 - x, 1 - y)]
    return x, y, c, chips


def _gather_weights(shards):
    nt = len(shards)

    def body(*refs):
        ins, outs = refs[:nt], refs[nt:2 * nt]
        s_send, s_recv, f_send, f_recv, l_sem = refs[2 * nt:]
        x, y, c, chips = _place()
        me = 2 * x + y
        sibling = (x, y, 1 - c)
        ids = [2 * cx + cy for cx, cy in chips]

        local = [pltpu.make_async_copy(ins[t].at[l], outs[t].at[l, me], l_sem.at[2 * t + l])
                 for t in range(nt) for l in range(2)]
        for cp in local:
            cp.start()

        def over_ici(t, k, block):
            return pltpu.make_async_remote_copy(
                src_ref=ins[t].at[c], dst_ref=outs[t].at[c, block], send_sem=s_send.at[3 * t + k],
                recv_sem=s_recv.at[3 * t + k], device_id=(*chips[k], c), device_id_type=MESH)

        def over_d2d(t, k, layer):
            blk = outs[t].at[layer, ids[k]]
            return pltpu.make_async_remote_copy(
                src_ref=blk, dst_ref=blk, send_sem=f_send.at[3 * t + k], recv_sem=f_recv.at[3 * t + k],
                device_id=sibling, device_id_type=MESH)

        pairs = [(t, k) for t in range(nt) for k in range(3)]
        for t, k in pairs:
            over_ici(t, k, me).start()
        for t, k in pairs:
            over_ici(t, k, ids[k]).wait_recv()
            over_d2d(t, k, c).start()
        for t, k in pairs:
            over_d2d(t, k, 1 - c).wait_recv()
        for t, k in pairs:
            over_ici(t, k, me).wait_send()
            over_d2d(t, k, c).wait_send()
        for cp in local:
            cp.wait()

    return pl.pallas_call(
        body, name="gather_weights",
        out_shape=[jax.ShapeDtypeStruct((2, N_CHIPS) + s.shape[1:], s.dtype) for s in shards],
        in_specs=[HBM_SPEC] * nt, out_specs=[HBM_SPEC] * nt,
        scratch_shapes=[pltpu.SemaphoreType.DMA((3 * nt,))] * 4 + [pltpu.SemaphoreType.DMA((2 * nt,))],
    )(*shards)


def _swap_layers(grads):
    nt = len(grads)

    def body(*refs):
        ins, outs = refs[:2 * nt], refs[2 * nt:3 * nt]
        s_send, s_recv = refs[3 * nt:]
        x, y, c, _ = _place()

        def copy(t, layer):
            return pltpu.make_async_remote_copy(
                src_ref=ins[2 * t + layer], dst_ref=outs[t], send_sem=s_send.at[t], recv_sem=s_recv.at[t],
                device_id=(x, y, 1 - c), device_id_type=MESH)

        for layer in range(2):
            @pl.when(c == 1 - layer)
            def _():
                for t in range(nt):
                    copy(t, layer).start()
        for t in range(nt):
            copy(t, 0).wait()

    flat = [g for pair in grads for g in pair]
    return pl.pallas_call(
        body, name="swap_layers",
        out_shape=[jax.ShapeDtypeStruct(pair[0].shape, F32) for pair in grads],
        in_specs=[HBM_SPEC] * (2 * nt), out_specs=[HBM_SPEC] * nt,
        scratch_shapes=[pltpu.SemaphoreType.DMA((nt,))] * 2,
    )(*flat)


def _scatter_chip_sums(sums):
    nt = len(sums)

    def body(*refs):
        ins, outs = refs[:nt], refs[nt:2 * nt]
        s_send, s_recv, l_sem = refs[2 * nt:]
        x, y, c, chips = _place()
        me = 2 * x + y
        ids = [2 * cx + cy for cx, cy in chips]
        local = [pltpu.make_async_copy(ins[t].at[me], outs[t].at[me], l_sem.at[t]) for t in range(nt)]
        for cp in local:
            cp.start()

        def copy(t, k, slot):
            return pltpu.make_async_remote_copy(
                src_ref=ins[t].at[ids[k]], dst_ref=outs[t].at[slot], send_sem=s_send.at[3 * t + k],
                recv_sem=s_recv.at[3 * t + k], device_id=(*chips[k], c), device_id_type=MESH)

        pairs = [(t, k) for t in range(nt) for k in range(3)]
        for t, k in pairs:
            copy(t, k, me).start()
        for t, k in pairs:
            copy(t, k, ids[k]).wait()
        for cp in local:
            cp.wait()

    return pl.pallas_call(
        body, name="scatter_chip_sums",
        out_shape=[jax.ShapeDtypeStruct(s.shape, F32) for s in sums],
        in_specs=[HBM_SPEC] * nt, out_specs=[HBM_SPEC] * nt,
        scratch_shapes=[pltpu.SemaphoreType.DMA((3 * nt,))] * 2 + [pltpu.SemaphoreType.DMA((nt,))],
    )(*sums)


def _share_with_sibling(reduced):
    nt = len(reduced)

    def body(*refs):
        ins, outs = refs[:nt], refs[nt:2 * nt]
        s_send, s_recv, l_sem = refs[2 * nt:]
        x, y, c, _ = _place()
        local = [pltpu.make_async_copy(ins[t], outs[t].at[c], l_sem.at[t]) for t in range(nt)]
        for cp in local:
            cp.start()

        def copy(t, layer):
            return pltpu.make_async_remote_copy(
                src_ref=ins[t], dst_ref=outs[t].at[layer], send_sem=s_send.at[t], recv_sem=s_recv.at[t],
                device_id=(x, y, 1 - c), device_id_type=MESH)

        for t in range(nt):
            copy(t, c).start()
        for t in range(nt):
            copy(t, 1 - c).wait()
        for cp in local:
            cp.wait()

    return pl.pallas_call(
        body, name="share_with_sibling",
        out_shape=[jax.ShapeDtypeStruct((2,) + r.shape, F32) for r in reduced],
        in_specs=[HBM_SPEC] * nt, out_specs=[HBM_SPEC] * nt,
        scratch_shapes=[pltpu.SemaphoreType.DMA((nt,))] * 3,
    )(*reduced)


def _gather_small(buf):
    def body(in_ref, out_ref, s_send, s_recv, l_sem):
        x, y, c, _ = _place()
        flips = [(fx, fy, fc) for fx in (0, 1) for fy in (0, 1) for fc in (0, 1)][1:]

        def peer(f):
            return tuple(1 - a if flip else a for a, flip in zip((x, y, c), f))

        def slot(p):
            return 4 * p[0] + 2 * p[1] + p[2]

        local = pltpu.make_async_copy(in_ref, out_ref.at[slot((x, y, c))], l_sem)
        local.start()

        def copy(k, owner):
            return pltpu.make_async_remote_copy(
                src_ref=in_ref, dst_ref=out_ref.at[slot(owner)], send_sem=s_send.at[k], recv_sem=s_recv.at[k],
                device_id=peer(flips[k]), device_id_type=MESH)

        for k in range(7):
            copy(k, (x, y, c)).start()
        for k in range(7):
            copy(k, peer(flips[k])).wait()
        local.wait()

    return pl.pallas_call(
        body, name="gather_small", out_shape=jax.ShapeDtypeStruct((8,) + buf.shape, F32),
        in_specs=[HBM_SPEC], out_specs=HBM_SPEC,
        scratch_shapes=[pltpu.SemaphoreType.DMA((7,))] * 2 + [pltpu.SemaphoreType.DMA],
    )(buf)


def _rows128(a):
    flat = a.reshape(-1)
    rows = -(-flat.shape[0] // LANES)
    rows8 = -(-rows // 8) * 8
    flat = jnp.pad(flat, (0, rows8 * LANES - flat.shape[0]))
    return flat.reshape(rows8, LANES)


def _pack(parts):
    return jnp.concatenate([_rows128(p) for p in parts], axis=0)


def _unpack(buf, shapes):
    out, r = [], 0
    for shp in shapes:
        n = int(np.prod(shp))
        rows8 = -(-(-(-n // LANES)) // 8) * 8
        out.append(buf[r:r + rows8].reshape(-1)[:n].reshape(shp))
        r += rows8
    return out


def kernel(x, a_norm_g, a_w_in, a_v_norm_g, a_w_s, a_b_s, a_w_out, b_norm_g, b_w_in, b_f_bias, b_q_norm_g, b_k_norm_g, b_w_out, loss_target, m_a_norm_g, m_a_w_in, m_a_v_norm_g, m_a_w_s, m_a_b_s, m_a_w_out, m_b_norm_g, m_b_w_in, m_b_f_bias, m_b_q_norm_g, m_b_k_norm_g, m_b_w_out, v_a_norm_g, v_a_w_in, v_a_v_norm_g, v_a_w_s, v_a_b_s, v_a_w_out, v_b_norm_g, v_b_w_in, v_b_f_bias, v_b_q_norm_g, v_b_k_norm_g, v_b_w_out):
    xs = x[0]
    target = loss_target[0]
    S, D = xs.shape
    n_layers = a_w_in.shape[0]
    assert n_layers == 2
    W = a_v_norm_g.shape[1]
    G = a_w_s.shape[1]
    H = b_f_bias.shape[1]
    HW = H * HEAD
    tq = _tile(S, 256)
    nq = S // tq
    core = lax.axis_index("c")
    chip = 2 * lax.axis_index("x") + lax.axis_index("y")
    sel = core.astype(jnp.int32).reshape(1)

    w_ain, w_aout, w_bin, w_bout, bn_all = _gather_weights(
        [a_w_in.astype(BF16), a_w_out.astype(BF16), b_w_in.astype(BF16), b_w_out.astype(BF16),
         b_norm_g.reshape(n_layers, 1, -1)])
    b_norm_full = bn_all.reshape(n_layers, D)
    cb = b_w_in.shape[2]
    w_bin_full = jnp.transpose(w_bin, (0, 2, 1, 3)).reshape(n_layers, D, N_CHIPS * cb)
    w_bmain = w_bin_full[:, :, :4 * HW]
    w_bf = jnp.pad(w_bin_full[:, :, 4 * HW:], ((0, 0), (0, 0), (0, LANES - H)))
    causal = jnp.tril(jnp.ones((CHUNK, CHUNK), dtype=bool))
    wc = jnp.where(causal[None, None], a_w_s, 0).astype(BF16)
    wc_t = jnp.swapaxes(wc, 2, 3)
    bs_t = jnp.swapaxes(a_b_s, 1, 2)
    f_bias = jnp.pad(b_f_bias, ((0, 0), (0, LANES - H))).reshape(n_layers, 1, LANES)

    def view_ain(l):
        return _View(w_ain, "col", (l,))

    def view_aout(l):
        return _View(w_aout, "row", (l,))

    def view_bout(l):
        return _View(w_bout, "row", (l,))

    saved = []
    cur = xs
    for i in range(2 * n_layers):
        l = i // 2
        if i % 2 == 0:
            h = _rmsnorm_fwd(f"a{l}_norm", cur, a_norm_g[l])
            p = _matmul(f"a{l}_in", _View(h), view_ain(l), tm=1024, tn=1024, tk=2048)
            y = _gate_fwd(f"a{l}_gate", p, a_v_norm_g[l], wc[l], bs_t[l])
            nxt = _matmul(f"a{l}_out", _View(y), view_aout(l), tm=1024, tn=1024, tk=1024, residual=cur)
            saved.append((cur, h, p, y))
        else:
            h = _rmsnorm_fwd(f"b{l}_norm", cur, b_norm_full[l])
            proj = _matmul(f"b{l}_in", _View(h), _View(w_bmain[l]), tm=1024, tn=1024, tk=2048)
            f = _matmul(f"b{l}_inf", _View(h), _View(w_bf[l]), tm=1024, tn=LANES, tk=2048)
            qn, kn, vb = _qkv_prep(f"b{l}_qkv", proj, b_q_norm_g[l], b_k_norm_g[l], H)
            cum = _fox_cum(f"b{l}_cum", f, f_bias[l])
            cum_t = cum[:, :H].T
            c_col, c_blk = cum_t.reshape(H, S, 1), cum_t.reshape(H, nq, tq)
            o, y, lse = _attn_fwd(f"b{l}_attn", qn, kn, vb, c_col, c_blk, proj, H, tq)
            nxt = _matmul(f"b{l}_out", _View(y), view_bout(l), tm=1024, tn=1024, tk=1024, residual=cur)
            saved.append((cur, h, proj, f, qn, kn, vb, c_col, c_blk, o, y, lse))
        cur = nxt

    g, gb, lcols = _loss_grad(cur, target)
    loss = lax.psum(0.5 * jnp.sum(lcols) / D, ("x", "y", "c"))

    big = {"a_w_in": [None] * n_layers, "a_w_out": [None] * n_layers,
           "b_w_in": [None] * n_layers, "b_w_out": [None] * n_layers}
    small = {k: [None] * n_layers for k in
             ("a_norm_g", "a_v_norm_g", "a_w_s", "a_b_s", "b_norm_g", "b_f_bias", "b_q_norm_g", "b_k_norm_g")}
    for i in reversed(range(2 * n_layers)):
        l = i // 2
        if i % 2 == 0:
            x_in, h, p, y = saved[i]
            dy = _matmul(f"a{l}_dy", _View(gb), view_aout(l), tb=True, out_dtype=BF16, tm=1024, tn=1024, tk=2048)
            d_wout = _matmul(f"a{l}_dwout", _View(y), _View(gb), ta=True, tm=2048, tn=1024, tk=512)
            dp, d_ws, d_bs, d_gv = _gate_bwd(f"a{l}_dgate", p, dy, a_v_norm_g[l], wc[l], wc_t[l], bs_t[l])
            dh = _matmul(f"a{l}_dh", _View(dp), view_ain(l), tb=True, tm=1024, tn=1024, tk=1024)
            d_win = _matmul(f"a{l}_dwin", _View(h), _View(dp), ta=True, tm=2048, tn=1024, tk=512,
                            out_colblocks=N_CHIPS)
            g, gb, d_gn = _rmsnorm_bwd(f"a{l}_dnorm", x_in, a_norm_g[l], dh, g)
            big["a_w_in"][l] = d_win
            big["a_w_out"][l] = d_wout.reshape(N_CHIPS, W // N_CHIPS, D)
            small["a_norm_g"][l] = d_gn.reshape(D)
            small["a_v_norm_g"][l] = d_gv.reshape(W)
            small["a_w_s"][l] = jnp.where(causal[None], d_ws, 0.0)
            small["a_b_s"][l] = d_bs[:, :G].T
        else:
            x_in, h, proj, f, qn, kn, vb, c_col, c_blk, o, y, lse = saved[i]
            dy = _matmul(f"b{l}_dy", _View(gb), view_bout(l), tb=True, out_dtype=BF16, tm=1024, tn=1024, tk=2048)
            d_wout = _matmul(f"b{l}_dwout", _View(y), _View(gb), ta=True, tm=2048, tn=1024, tk=512)
            do, dz, delta = _attn_bwd_prep(f"b{l}_dprep", dy, o, proj, H)
            delta_blk = delta[:, :H].T.reshape(H, nq, tq)
            dq, dk, dv, dc, dcq = _attn_bwd(f"b{l}_dattn", qn, do, kn, vb, lse.reshape(H, nq, tq), delta_blk,
                                            c_blk, c_col, H, tq)
            heads_to_lanes = lambda a: jnp.pad(a.reshape(H, S).T, ((0, 0), (0, LANES - H)))
            df, d_fb = _fox_cum_bwd(f"b{l}_dcum", heads_to_lanes(dc), heads_to_lanes(dcq), f, f_bias[l])
            dproj, d_gq, d_gk = _qk_bwd(f"b{l}_dqk", proj, dq, dk, dv, dz, b_q_norm_g[l], b_k_norm_g[l], H)
            dh_f = _matmul(f"b{l}_dhf", _View(df), _View(w_bf[l]), tb=True, tm=1024, tn=1024, tk=LANES)
            dh = _matmul(f"b{l}_dh", _View(dproj), _View(w_bmain[l]), tb=True, tm=1024, tn=1024, tk=1024,
                         residual=dh_f)
            d_wmain = _matmul(f"b{l}_dwin", _View(h), _View(dproj), ta=True, tm=2048, tn=1024, tk=512)
            d_wf = _matmul(f"b{l}_dwinf", _View(h), _View(df), ta=True, tm=2048, tn=LANES, tk=512)
            d_win = jnp.concatenate([d_wmain, d_wf[:, :H]], axis=1)
            g, gb, d_gn = _rmsnorm_bwd(f"b{l}_dnorm", x_in, b_norm_full[l], dh, g)
            big["b_w_in"][l] = jnp.transpose(d_win.reshape(D, N_CHIPS, cb), (1, 0, 2))
            big["b_w_out"][l] = d_wout.reshape(N_CHIPS, HW // N_CHIPS, D)
            small["b_norm_g"][l] = d_gn.reshape(D)
            small["b_f_bias"][l] = d_fb[0, :H]
            small["b_q_norm_g"][l] = d_gq.reshape(HEAD)
            small["b_k_norm_g"][l] = d_gk.reshape(HEAD)
    grad_x = g[None]

    names = ["a_w_in", "a_w_out", "b_w_in", "b_w_out"]
    pairs = [tuple(big[n]) for n in names]
    from_sibling = _swap_layers(pairs)
    chip_sums = []
    for n, pair, other in zip(names, pairs, from_sibling):
        shp = other.shape
        flat = lambda a: a.reshape(shp[0] * shp[1], shp[2])
        chip_sums.append(_add_selected(f"chipsum_{n}", flat(pair[0]), flat(pair[1]), flat(other), sel).reshape(shp))
    slots = _scatter_chip_sums(chip_sums)
    reduced = [_sum_slots(f"reduce_{n}", s) for n, s in zip(names, slots)]
    full = _share_with_sibling(reduced)
    grads = dict(zip(names, full))

    small_names = ["a_norm_g", "a_v_norm_g", "a_w_s", "a_b_s", "b_norm_g", "b_f_bias", "b_q_norm_g", "b_k_norm_g"]
    small_parts = [jnp.stack(small[n]) for n in small_names]
    small_sum = _sum_slots("reduce_small", _gather_small(_pack(small_parts)))
    for n, a in zip(small_names, _unpack(small_sum, [p.shape for p in small_parts])):
        grads[n] = a
    nb = b_norm_g.shape[1]
    grads["b_norm_g"] = lax.dynamic_slice_in_dim(grads["b_norm_g"], chip * nb, nb, axis=1)

    weights = dict(a_norm_g=a_norm_g, a_w_in=a_w_in, a_v_norm_g=a_v_norm_g, a_w_s=a_w_s, a_b_s=a_b_s,
                   a_w_out=a_w_out, b_norm_g=b_norm_g, b_w_in=b_w_in, b_f_bias=b_f_bias,
                   b_q_norm_g=b_q_norm_g, b_k_norm_g=b_k_norm_g, b_w_out=b_w_out)
    mom1 = dict(a_norm_g=m_a_norm_g, a_w_in=m_a_w_in, a_v_norm_g=m_a_v_norm_g, a_w_s=m_a_w_s, a_b_s=m_a_b_s,
                a_w_out=m_a_w_out, b_norm_g=m_b_norm_g, b_w_in=m_b_w_in, b_f_bias=m_b_f_bias,
                b_q_norm_g=m_b_q_norm_g, b_k_norm_g=m_b_k_norm_g, b_w_out=m_b_w_out)
    mom2 = dict(a_norm_g=v_a_norm_g, a_w_in=v_a_w_in, a_v_norm_g=v_a_v_norm_g, a_w_s=v_a_w_s, a_b_s=v_a_b_s,
                a_w_out=v_a_w_out, b_norm_g=v_b_norm_g, b_w_in=v_b_w_in, b_f_bias=v_b_f_bias,
                b_q_norm_g=v_b_q_norm_g, b_k_norm_g=v_b_k_norm_g, b_w_out=v_b_w_out)
    order = ["a_norm_g", "a_w_in", "a_v_norm_g", "a_w_s", "a_b_s", "a_w_out", "b_norm_g", "b_w_in", "b_f_bias",
             "b_q_norm_g", "b_k_norm_g", "b_w_out"]
    delta, new_m, new_v = {}, {}, {}
    for n in names:
        shp = weights[n].shape
        flat = lambda a: a.reshape(shp[0] * shp[1], shp[2])
        d, nm, nv = _adamw(f"adamw_{n}", flat(weights[n]), flat(grads[n]), flat(mom1[n]), flat(mom2[n]))
        delta[n], new_m[n], new_v[n] = d.reshape(shp), nm.reshape(shp), nv.reshape(shp)
    small_shapes = [weights[n].shape for n in small_names]
    pack_w, pack_g, pack_m, pack_v = (_pack([d[n] for n in small_names]) for d in (weights, grads, mom1, mom2))
    d, nm, nv = _adamw("adamw_small", pack_w, pack_g, pack_m, pack_v)
    for dst, buf in ((delta, d), (new_m, nm), (new_v, nv)):
        for n, a in zip(small_names, _unpack(buf, small_shapes)):
            dst[n] = a

    return (loss, grad_x, *[grads[n] for n in order], *[delta[n] for n in order],
            *[new_m[n] for n in order], *[new_v[n] for n in order])
```

```python
import functools
import math

import numpy as np
import jax
import jax.numpy as jnp
from jax import lax
from jax.experimental import pallas as pl
from jax.experimental.pallas import tpu as pltpu

F32 = jnp.float32
BF16 = jnp.bfloat16
MESH = pl.DeviceIdType.MESH

EPS = 1e-6
CHUNK = 128
HEAD = 128
LANES = 128
N_CHIPS = 4
VMEM_LIMIT = 56 * 1024 * 1024

ADAM_LR = 0.001
ADAM_B1 = 0.9
ADAM_B2 = 0.999
ADAM_EPS = 1e-08
ADAM_WD = 0.01
ADAM_STEP = 10

_NT = (((1,), (1,)), ((), ()))
_TN = (((0,), (0,)), ((), ()))
_GELU_C = math.sqrt(2.0 / math.pi)

HBM_SPEC = pl.BlockSpec(memory_space=pltpu.HBM)


def _params(*sem):
    return pltpu.CompilerParams(dimension_semantics=sem, vmem_limit_bytes=VMEM_LIMIT)


def _tile(dim, pref, unit=LANES):
    t = (min(pref, dim) // unit) * unit
    while t >= unit:
        if dim % t == 0:
            return t
        t -= unit
    return dim


def _gelu(x):
    return 0.5 * x * (1.0 + jnp.tanh(_GELU_C * (x + 0.044715 * (x * x * x))))


def _gelu_and_grad(x):
    x2 = x * x
    t = jnp.tanh(_GELU_C * (x + 0.044715 * (x2 * x)))
    val = 0.5 * x * (1.0 + t)
    grad = 0.5 * (1.0 + t) + 0.5 * x * (1.0 - t * t) * (_GELU_C * (1.0 + 3.0 * 0.044715 * x2))
    return val, grad


def _sigmoid(x):
    return 1.0 / (1.0 + jnp.exp(-x))


class _View:
    def __init__(self, arr, kind="2d", lead=()):
        self.arr, self.kind, self.lead = arr, kind, tuple(lead)
        shp = arr.shape[len(self.lead):]
        if kind == "2d":
            self.R, self.C = shp
        elif kind == "col":
            self.nb, self.R, self.cb = shp
            self.C = self.nb * self.cb
        else:
            self.nb, self.rb, self.C = shp
            self.R = self.nb * self.rb

    def fit(self, tr, tc):
        if self.kind == "col":
            tc = _tile(self.cb, tc)
        elif self.kind == "row":
            tr = _tile(self.rb, tr, unit=8)
        return tr, tc

    def spec(self, tr, tc, rc_of_grid):
        lead = self.lead
        sq = (None,) * len(lead)
        if self.kind == "2d":
            return pl.BlockSpec(sq + (tr, tc), lambda *g: lead + tuple(rc_of_grid(*g)))
        if self.kind == "col":
            q = self.cb // tc

            def im(*g):
                r, c = rc_of_grid(*g)
                return lead + (c // q, r, c % q)

            return pl.BlockSpec(sq + (None, tr, tc), im)
        q = self.rb // tr

        def im(*g):
            r, c = rc_of_grid(*g)
            return lead + (r // q, r % q, c)

        return pl.BlockSpec(sq + (None, tr, tc), im)


def _matmul(name, a, b, *, ta=False, tb=False, out_dtype=F32, tm=1024, tn=1024, tk=1024,
            out_colblocks=None, residual=None):
    M, K = (a.C, a.R) if ta else (a.R, a.C)
    N, K2 = (b.R, b.C) if tb else (b.C, b.R)
    assert K == K2, (name, K, K2)
    tm, tn, tk = _tile(M, tm), _tile(N, tn), _tile(K, tk)
    if ta:
        tk, tm = a.fit(tk, tm)
    else:
        tm, tk = a.fit(tm, tk)
    if tb:
        tn, tk2 = b.fit(tn, tk)
    else:
        tk2, tn = b.fit(tk, tn)
    if tk2 != tk:
        tk = min(tk, tk2)
        if ta:
            tk, tm = a.fit(tk, tm)
        else:
            tm, tk = a.fit(tm, tk)
    if out_colblocks:
        tn = _tile(N // out_colblocks, tn)
    assert M % tm == 0 and N % tn == 0 and K % tk == 0, (name, M, N, K, tm, tn, tk)
    nk = K // tk
    dims = (((0 if ta else 1,), (1 if tb else 0,)), ((), ()))

    def body(*refs):
        if residual is not None:
            a_ref, b_ref, r_ref, o_ref, acc_ref = refs
        else:
            a_ref, b_ref, o_ref, acc_ref = refs
            r_ref = None
        k = pl.program_id(2)
        part = lax.dot_general(a_ref[...], b_ref[...], dims, preferred_element_type=F32)

        def finish(total):
            if r_ref is not None:
                total = total + r_ref[...]
            o_ref[...] = total.astype(out_dtype)

        if nk == 1:
            finish(part)
        else:
            @pl.when(k == 0)
            def _():
                acc_ref[...] = part

            @pl.when(jnp.logical_and(k > 0, k < nk - 1))
            def _():
                acc_ref[...] += part

            @pl.when(k == nk - 1)
            def _():
                finish(acc_ref[...] + part)

    a_spec = a.spec(tk, tm, lambda i, j, k: (k, i)) if ta else a.spec(tm, tk, lambda i, j, k: (i, k))
    b_spec = b.spec(tn, tk, lambda i, j, k: (j, k)) if tb else b.spec(tk, tn, lambda i, j, k: (k, j))
    in_specs, args = [a_spec, b_spec], [a.arr, b.arr]
    if residual is not None:
        in_specs.append(pl.BlockSpec((tm, tn), lambda i, j, k: (i, j)))
        args.append(residual)
    if out_colblocks:
        q = (N // out_colblocks) // tn
        out_shape = jax.ShapeDtypeStruct((out_colblocks, M, N // out_colblocks), out_dtype)
        out_spec = pl.BlockSpec((None, tm, tn), lambda i, j, k: (j // q, i, j % q))
    else:
        out_shape = jax.ShapeDtypeStruct((M, N), out_dtype)
        out_spec = pl.BlockSpec((tm, tn), lambda i, j, k: (i, j))
    acc_shape = (tm, tn) if nk > 1 else (8, LANES)
    return pl.pallas_call(
        body, name=name, grid=(M // tm, N // tn, nk), in_specs=in_specs, out_specs=out_spec,
        out_shape=out_shape, scratch_shapes=[pltpu.VMEM(acc_shape, F32)],
        compiler_params=_params("arbitrary", "arbitrary", "arbitrary"),
    )(*args)


def _rmsnorm_fwd(name, x, gain):
    S, D = x.shape
    tr = _tile(S, 512, unit=8)

    def body(x_ref, g_ref, h_ref):
        xv = x_ref[...]
        r = lax.rsqrt(jnp.mean(xv * xv, axis=-1, keepdims=True) + EPS)
        h_ref[...] = (xv * r * g_ref[...]).astype(BF16)

    return pl.pallas_call(
        body, name=name, grid=(S // tr,),
        in_specs=[pl.BlockSpec((tr, D), lambda i: (i, 0)), pl.BlockSpec((1, D), lambda i: (0, 0))],
        out_specs=pl.BlockSpec((tr, D), lambda i: (i, 0)),
        out_shape=jax.ShapeDtypeStruct((S, D), BF16),
        compiler_params=_params("arbitrary"),
    )(x, gain.reshape(1, D))


def _rmsnorm_bwd(name, x, gain, dh, g_res):
    S, D = x.shape
    tr = _tile(S, 256, unit=8)

    def body(x_ref, g_ref, dh_ref, res_ref, dx_ref, dxb_ref, dg_ref):
        i = pl.program_id(0)
        xv = x_ref[...]
        r = lax.rsqrt(jnp.mean(xv * xv, axis=-1, keepdims=True) + EPS)
        xhat = xv * r
        dhv = dh_ref[...]
        part = jnp.sum(dhv * xhat, axis=0, keepdims=True)

        @pl.when(i == 0)
        def _():
            dg_ref[...] = part

        @pl.when(i > 0)
        def _():
            dg_ref[...] += part

        dxhat = dhv * g_ref[...]
        dx = res_ref[...] + r * (dxhat - xhat * jnp.mean(dxhat * xhat, axis=-1, keepdims=True))
        dx_ref[...] = dx
        dxb_ref[...] = dx.astype(BF16)

    row = pl.BlockSpec((tr, D), lambda i: (i, 0))
    vec = pl.BlockSpec((1, D), lambda i: (0, 0))
    return pl.pallas_call(
        body, name=name, grid=(S // tr,), in_specs=[row, vec, row, row], out_specs=[row, row, vec],
        out_shape=[jax.ShapeDtypeStruct((S, D), F32), jax.ShapeDtypeStruct((S, D), BF16),
                   jax.ShapeDtypeStruct((1, D), F32)],
        compiler_params=_params("arbitrary"),
    )(x, gain.reshape(1, D), dh, g_res)


def _loss_grad(x, target):
    S, D = x.shape
    tr = _tile(S, 512, unit=8)

    def body(x_ref, t_ref, g_ref, gb_ref, l_ref):
        i = pl.program_id(0)
        e = x_ref[...] - t_ref[...]
        g = e * (1.0 / D)
        g_ref[...] = g
        gb_ref[...] = g.astype(BF16)
        part = jnp.sum(e * e, axis=0, keepdims=True)

        @pl.when(i == 0)
        def _():
            l_ref[...] = part

        @pl.when(i > 0)
        def _():
            l_ref[...] += part

    row = pl.BlockSpec((tr, D), lambda i: (i, 0))
    vec = pl.BlockSpec((1, D), lambda i: (0, 0))
    return pl.pallas_call(
        body, name="loss_grad", grid=(S // tr,), in_specs=[row, row], out_specs=[row, row, vec],
        out_shape=[jax.ShapeDtypeStruct((S, D), F32), jax.ShapeDtypeStruct((S, D), BF16),
                   jax.ShapeDtypeStruct((1, D), F32)],
        compiler_params=_params("arbitrary"),
    )(x, target)


def _gate_fwd(name, p, v_gain, wc, bs_t):
    S, W3 = p.shape
    W = W3 // 3
    G = wc.shape[0]
    gd = W // G

    def body(p_ref, gv_ref, wc_ref, bs_ref, y_ref):
        vg = _gelu(p_ref[:, W:2 * W])
        r = lax.rsqrt(jnp.mean(vg * vg, axis=-1, keepdims=True) + EPS)
        vb = (vg * r * gv_ref[...]).astype(BF16)
        zp = p_ref[:, 2 * W:]
        gate = _gelu(p_ref[:, :W]) * (zp * _sigmoid(zp))
        for g in range(G):
            sl = slice(g * gd, (g + 1) * gd)
            mixed = jnp.dot(wc_ref[g], vb[:, sl], preferred_element_type=F32) + bs_ref[:, g:g + 1]
            y_ref[:, sl] = (gate[:, sl] * mixed).astype(BF16)

    return pl.pallas_call(
        body, name=name, grid=(S // CHUNK,),
        in_specs=[pl.BlockSpec((CHUNK, W3), lambda i: (i, 0)), pl.BlockSpec((1, W), lambda i: (0, 0)),
                  pl.BlockSpec((G, CHUNK, CHUNK), lambda i: (0, 0, 0)), pl.BlockSpec((CHUNK, G), lambda i: (0, 0))],
        out_specs=pl.BlockSpec((CHUNK, W), lambda i: (i, 0)),
        out_shape=jax.ShapeDtypeStruct((S, W), BF16),
        compiler_params=_params("arbitrary"),
    )(p, v_gain.reshape(1, W), wc, bs_t)


def _gate_bwd(name, p, dy, v_gain, wc, wc_t, bs_t):
    S, W3 = p.shape
    W = W3 // 3
    G = wc.shape[0]
    gd = W // G

    def body(p_ref, dy_ref, gv_ref, wc_ref, wct_ref, bs_ref, dp_ref, dws_ref, dbs_ref, dgv_ref, dv_scr):
        i = pl.program_id(0)

        @pl.when(i == 0)
        def _():
            dws_ref[...] = jnp.zeros_like(dws_ref)
            dbs_ref[...] = jnp.zeros_like(dbs_ref)
            dgv_ref[...] = jnp.zeros_like(dgv_ref)

        gu, dgu = _gelu_and_grad(p_ref[:, :W])
        vg, dvg_dv = _gelu_and_grad(p_ref[:, W:2 * W])
        zp = p_ref[:, 2 * W:]
        sig = _sigmoid(zp)
        sz = zp * sig
        dsz = sig * (1.0 + zp * (1.0 - sig))
        r = lax.rsqrt(jnp.mean(vg * vg, axis=-1, keepdims=True) + EPS)
        vhat = vg * r
        gv = gv_ref[...]
        vb = (vhat * gv).astype(BF16)
        dy = dy_ref[...].astype(F32)
        lane = lax.broadcasted_iota(jnp.int32, (CHUNK, LANES), 1)
        dbs = jnp.zeros((CHUNK, LANES), F32)
        for g in range(G):
            sl = slice(g * gd, (g + 1) * gd)
            vsl = vb[:, sl]
            mixed = jnp.dot(wc_ref[g], vsl, preferred_element_type=F32) + bs_ref[:, g:g + 1]
            dyg, gug, szg = dy[:, sl], gu[:, sl], sz[:, sl]
            dp_ref[:, sl] = (dyg * mixed * szg * dgu[:, sl]).astype(BF16)
            dp_ref[:, 2 * W + g * gd:2 * W + (g + 1) * gd] = (dyg * gug * mixed * dsz[:, sl]).astype(BF16)
            dm = dyg * gug * szg
            dmb = dm.astype(BF16)
            dws_ref[g] += lax.dot_general(dmb, vsl, _NT, preferred_element_type=F32)
            dbs = dbs + jnp.where(lane == g, jnp.sum(dm, axis=1, keepdims=True), 0.0)
            dv_scr[:, sl] = jnp.dot(wct_ref[g], dmb, preferred_element_type=F32)
        dbs_ref[...] += dbs
        dv = dv_scr[...]
        dgv_ref[...] += jnp.sum(dv * vhat, axis=0, keepdims=True)
        dvhat = dv * gv
        dvg = r * (dvhat - vhat * jnp.mean(dvhat * vhat, axis=-1, keepdims=True))
        dp_ref[:, W:2 * W] = (dvg * dvg_dv).astype(BF16)

    return pl.pallas_call(
        body, name=name, grid=(S // CHUNK,),
        in_specs=[pl.BlockSpec((CHUNK, W3), lambda i: (i, 0)), pl.BlockSpec((CHUNK, W), lambda i: (i, 0)),
                  pl.BlockSpec((1, W), lambda i: (0, 0)),
                  pl.BlockSpec((G, CHUNK, CHUNK), lambda i: (0, 0, 0)),
                  pl.BlockSpec((G, CHUNK, CHUNK), lambda i: (0, 0, 0)),
                  pl.BlockSpec((CHUNK, G), lambda i: (0, 0))],
        out_specs=[pl.BlockSpec((CHUNK, W3), lambda i: (i, 0)),
                   pl.BlockSpec((G, CHUNK, CHUNK), lambda i: (0, 0, 0)),
                   pl.BlockSpec((CHUNK, LANES), lambda i: (0, 0)),
                   pl.BlockSpec((1, W), lambda i: (0, 0))],
        out_shape=[jax.ShapeDtypeStruct((S, W3), BF16), jax.ShapeDtypeStruct((G, CHUNK, CHUNK), F32),
                   jax.ShapeDtypeStruct((CHUNK, LANES), F32), jax.ShapeDtypeStruct((1, W), F32)],
        scratch_shapes=[pltpu.VMEM((CHUNK, W), F32)],
        compiler_params=_params("arbitrary"),
    )(p, dy, v_gain.reshape(1, W), wc, wc_t, bs_t)


AUG = 2 * HEAD
LOG2E = 1.0 / math.log(2.0)
Q_SUM_LANE = HEAD + 3
K_SUM_LANE = HEAD


def _pieces(x, sign=1.0):
    hi, mid, lo = _split3(sign * x)
    return hi.astype(F32), mid.astype(F32), lo.astype(F32)


def _lanes(lane, start, vals, rest):
    out = rest
    for n, v in enumerate(vals):
        out = jnp.where(lane == start + n, v, out)
    return out


def _qkv_prep(name, proj, cum, q_gain, k_gain, H):
    S = proj.shape[0]
    HW = H * HEAD
    tr = _tile(S, 256, unit=8)
    sigma = (HEAD ** -0.5) * LOG2E

    def body(q_ref, k_ref, v_ref, c_ref, gq_ref, gk_ref, qa_ref, ka_ref, va_ref):
        lane = lax.broadcasted_iota(jnp.int32, (tr, HEAD), 1)
        zero = jnp.zeros((tr, HEAD), F32)
        v_aug = jnp.where(lane < 3, 1.0, zero).astype(BF16)
        for h in range(H):
            sl = slice(h * HEAD, (h + 1) * HEAD)
            a0 = h * AUG
            t = q_ref[:, sl]
            r = lax.rsqrt(jnp.mean(t * t, axis=-1, keepdims=True) + EPS)
            qa_ref[:, a0:a0 + HEAD] = (t * r * gq_ref[...] * sigma).astype(BF16)
            t = k_ref[:, sl]
            r = lax.rsqrt(jnp.mean(t * t, axis=-1, keepdims=True) + EPS)
            ka_ref[:, a0:a0 + HEAD] = (t * r * gk_ref[...]).astype(BF16)
            va_ref[:, a0:a0 + HEAD] = v_ref[:, sl].astype(BF16)
            va_ref[:, a0 + HEAD:a0 + AUG] = v_aug
            c2 = c_ref[:, h:h + 1] * LOG2E
            qa_ref[:, a0 + HEAD:a0 + AUG] = _lanes(lane, 0, _pieces(c2) + (1.0, 1.0, 1.0), zero).astype(BF16)
            ka_ref[:, a0 + HEAD:a0 + AUG] = _lanes(
                lane, 0, (1.0, 1.0, 1.0) + _pieces(c2, -1.0) + (1.0, 1.0, 1.0), zero).astype(BF16)

    col = lambda c: pl.BlockSpec((tr, HW), lambda i: (i, c))
    vec = pl.BlockSpec((1, HEAD), lambda i: (0, 0))
    aug = pl.BlockSpec((tr, H * AUG), lambda i: (i, 0))
    return pl.pallas_call(
        body, name=name, grid=(S // tr,),
        in_specs=[col(0), col(1), col(2), pl.BlockSpec((tr, LANES), lambda i: (i, 0)), vec, vec],
        out_specs=[aug] * 3, out_shape=[jax.ShapeDtypeStruct((S, H * AUG), BF16)] * 3,
        compiler_params=_params("arbitrary"),
    )(proj, proj, proj, cum, q_gain.reshape(1, HEAD), k_gain.reshape(1, HEAD))


def _split3(x):
    hi = x.astype(BF16)
    r1 = x - hi.astype(F32)
    mid = r1.astype(BF16)
    lo = (r1 - mid.astype(F32)).astype(BF16)
    return hi, mid, lo


def _tri_sum(tri, x):
    hi, mid, lo = _split3(x)
    d = lambda t: jnp.dot(tri, t, preferred_element_type=F32)
    return d(hi) + (d(mid) + d(lo))


def _log_sigmoid(x):
    return jnp.minimum(x, 0.0) - jnp.log(1.0 + jnp.exp(-jnp.abs(x)))


def _fox_cum(name, f, bias):
    S = f.shape[0]
    tb = _tile(S, 256, unit=8)

    def body(f_ref, b_ref, c_ref):
        rr = lax.broadcasted_iota(jnp.int32, (tb, tb), 0)
        cc = lax.broadcasted_iota(jnp.int32, (tb, tb), 1)
        tri = (rr >= cc).astype(BF16)

        def step(t, carry):
            off = pl.multiple_of(t * tb, tb)
            lf = _log_sigmoid(f_ref[pl.ds(off, tb), :] + b_ref[...])
            c = _tri_sum(tri, lf) + carry
            c_ref[pl.ds(off, tb), :] = c
            return c[tb - 1:tb, :]

        lax.fori_loop(0, S // tb, step, jnp.zeros((1, LANES), F32))

    return pl.pallas_call(
        body, name=name, out_shape=jax.ShapeDtypeStruct((S, LANES), F32),
        in_specs=[pl.BlockSpec(memory_space=pltpu.VMEM)] * 2, out_specs=pl.BlockSpec(memory_space=pltpu.VMEM),
        compiler_params=pltpu.CompilerParams(vmem_limit_bytes=VMEM_LIMIT),
    )(f, bias)


def _fox_cum_bwd(name, dcum, f, bias):
    S = f.shape[0]
    tb = _tile(S, 256, unit=8)
    nb = S // tb

    def body(dc_ref, f_ref, b_ref, df_ref, db_ref):
        rr = lax.broadcasted_iota(jnp.int32, (tb, tb), 0)
        cc = lax.broadcasted_iota(jnp.int32, (tb, tb), 1)
        tri = (rr <= cc).astype(BF16)

        def step(t, carry):
            tail, dbias = carry
            off = pl.multiple_of((nb - 1 - t) * tb, tb)
            dlf = _tri_sum(tri, dc_ref[pl.ds(off, tb), :]) + tail
            d = dlf * _sigmoid(-(f_ref[pl.ds(off, tb), :] + b_ref[...]))
            df_ref[pl.ds(off, tb), :] = d.astype(BF16)
            return dlf[0:1, :], dbias + jnp.sum(d, axis=0, keepdims=True)

        z = jnp.zeros((1, LANES), F32)
        _, dbias = lax.fori_loop(0, nb, step, (z, z))
        db_ref[...] = dbias

    vm = pl.BlockSpec(memory_space=pltpu.VMEM)
    return pl.pallas_call(
        body, name=name, out_shape=[jax.ShapeDtypeStruct((S, LANES), BF16), jax.ShapeDtypeStruct((1, LANES), F32)],
        in_specs=[vm] * 3, out_specs=[vm] * 2,
        compiler_params=pltpu.CompilerParams(vmem_limit_bytes=VMEM_LIMIT),
    )(dcum, f, bias)


def _attn_fwd(name, qa, ka, va, proj, H, tq):
    S = qa.shape[0]
    HW = H * HEAD
    nq = S // tq

    def body(q_ref, k_ref, v_ref, z_ref, o_ref, y_ref, lse_ref):
        i = pl.program_id(1)
        q = q_ref[...]

        def step(j, carry, masked):
            m, acc = carry
            off = pl.multiple_of(j * tq, tq)
            s = lax.dot_general(q, k_ref[pl.ds(off, tq), :], _NT, preferred_element_type=F32)
            if masked:
                rr = lax.broadcasted_iota(jnp.int32, (tq, tq), 0)
                cc = lax.broadcasted_iota(jnp.int32, (tq, tq), 1)
                s = jnp.where(rr >= cc, s, -jnp.inf)
            m_new = jnp.maximum(m, jnp.max(s, axis=1, keepdims=True))
            pr = jnp.exp2(s - m_new).astype(BF16)
            acc = jnp.exp2(m - m_new) * acc + jnp.dot(pr, v_ref[pl.ds(off, tq), :], preferred_element_type=F32)
            return m_new, acc

        init = (jnp.full((tq, 1), -jnp.inf, F32), jnp.zeros((tq, AUG), F32))
        carry = lax.fori_loop(0, i, lambda j, c: step(j, c, False), init)
        m, acc = step(i, carry, True)
        l = acc[:, HEAD:HEAD + 1]
        o = acc[:, :HEAD] / l
        z = z_ref[...]
        o_ref[...] = o
        y_ref[...] = (o * (z * _sigmoid(z))).astype(BF16)
        lse_ref[...] = m + jnp.log(l) * LOG2E

    qspec = pl.BlockSpec((tq, AUG), lambda h, i: (i, h))
    kvspec = pl.BlockSpec((S, AUG), lambda h, i: (0, h))
    ospec = pl.BlockSpec((tq, HEAD), lambda h, i: (i, h))
    return pl.pallas_call(
        body, name=name, grid=(H, nq),
        in_specs=[qspec, kvspec, kvspec, pl.BlockSpec((tq, HEAD), lambda h, i: (i, 3 * H + h))],
        out_specs=[ospec, ospec, pl.BlockSpec((None, tq, 1), lambda h, i: (h, i, 0))],
        out_shape=[jax.ShapeDtypeStruct((S, HW), F32), jax.ShapeDtypeStruct((S, HW), BF16),
                   jax.ShapeDtypeStruct((H, S, 1), F32)],
        compiler_params=_params("arbitrary", "arbitrary"),
    )(qa, ka, va, proj)


def _attn_bwd_prep(name, dy, o, proj, qa, lse, H):
    S, HW = o.shape
    tr = _tile(S, 256, unit=8)

    def body(dy_ref, o_ref, z_ref, qa_ref, lse_ref, doa_ref, dz_ref, qab_ref):
        lane = lax.broadcasted_iota(jnp.int32, (tr, HEAD), 1)
        zero = jnp.zeros((tr, HEAD), F32)
        for h in range(H):
            sl = slice(h * HEAD, (h + 1) * HEAD)
            a0 = h * AUG
            dy = dy_ref[:, sl].astype(F32)
            z = z_ref[:, sl]
            o = o_ref[:, sl]
            sig = _sigmoid(z)
            dob = (dy * (z * sig)).astype(BF16)
            dz_ref[:, sl] = (dy * o * (sig * (1.0 + z * (1.0 - sig)))).astype(BF16)
            delta = jnp.sum(dob.astype(F32) * o, axis=1, keepdims=True)
            doa_ref[:, a0:a0 + HEAD] = dob
            doa_ref[:, a0 + HEAD:a0 + AUG] = _lanes(lane, 0, _pieces(delta, -1.0), zero).astype(BF16)
            qab_ref[:, a0:a0 + HEAD] = qa_ref[:, a0:a0 + HEAD]
            qab_ref[:, a0 + HEAD:a0 + AUG] = _lanes(
                lane, 6, _pieces(lse_ref[:, h:h + 1], -1.0), qa_ref[:, a0 + HEAD:a0 + AUG].astype(F32)).astype(BF16)

    row = pl.BlockSpec((tr, HW), lambda i: (i, 0))
    aug = pl.BlockSpec((tr, H * AUG), lambda i: (i, 0))
    return pl.pallas_call(
        body, name=name, grid=(S // tr,),
        in_specs=[row, row, pl.BlockSpec((tr, HW), lambda i: (i, 3)), aug, pl.BlockSpec((tr, LANES), lambda i: (i, 0))],
        out_specs=[aug, row, aug],
        out_shape=[jax.ShapeDtypeStruct((S, H * AUG), BF16), jax.ShapeDtypeStruct((S, HW), BF16),
                   jax.ShapeDtypeStruct((S, H * AUG), BF16)],
        compiler_params=_params("arbitrary"),
    )(dy, o, proj, qa, lse)


def _attn_bwd(name, qab, doa, ka, va, H, tq):
    S = qab.shape[0]
    nq = S // tq

    def body(q_ref, do_ref, k_ref, v_ref, dq_ref, dk_ref, dv_ref):
        j = pl.program_id(1)

        @pl.when(j == 0)
        def _():
            dq_ref[...] = jnp.zeros_like(dq_ref)

        k = k_ref[...]
        v = v_ref[...]

        def step(i, carry, masked):
            dk_acc, dv_acc = carry
            off = pl.multiple_of(i * tq, tq)
            q = q_ref[pl.ds(off, tq), :]
            do = do_ref[pl.ds(off, tq), :]
            st = lax.dot_general(k, q, _NT, preferred_element_type=F32)
            if masked:
                rr = lax.broadcasted_iota(jnp.int32, (tq, tq), 0)
                cc = lax.broadcasted_iota(jnp.int32, (tq, tq), 1)
                st = jnp.where(cc >= rr, st, -jnp.inf)
            pt = jnp.exp2(st)
            dst = pt * lax.dot_general(v, do, _NT, preferred_element_type=F32)
            dsb = dst.astype(BF16)
            dv_acc = dv_acc + jnp.dot(pt.astype(BF16), do[:, :HEAD], preferred_element_type=F32)
            dk_acc = dk_acc + jnp.dot(dsb, q, preferred_element_type=F32)
            dq_ref[pl.ds(off, tq), :] += lax.dot_general(dsb, k, _TN, preferred_element_type=F32)
            return dk_acc, dv_acc

        carry = step(j, (jnp.zeros((tq, AUG), F32), jnp.zeros((tq, HEAD), F32)), True)
        dk_acc, dv_acc = lax.fori_loop(j + 1, nq, lambda i, c: step(i, c, False), carry)
        dk_ref[...] = dk_acc
        dv_ref[...] = dv_acc

    full = pl.BlockSpec((S, AUG), lambda h, j: (0, h))
    blk = pl.BlockSpec((tq, AUG), lambda h, j: (j, h))
    return pl.pallas_call(
        body, name=name, grid=(H, nq),
        in_specs=[full, full, blk, blk],
        out_specs=[full, blk, pl.BlockSpec((tq, HEAD), lambda h, j: (j, h))],
        out_shape=[jax.ShapeDtypeStruct((S, H * AUG), F32), jax.ShapeDtypeStruct((S, H * AUG), F32),
                   jax.ShapeDtypeStruct((S, H * HEAD), F32)],
        compiler_params=_params("arbitrary", "arbitrary"),
    )(qab, doa, ka, va)


def _qk_bwd(name, proj, dqa, dka, dv, dz, q_gain, k_gain, H):
    S = proj.shape[0]
    HW = H * HEAD
    tr = _tile(S, 256, unit=8)
    scale = HEAD ** -0.5
    factors = (scale, 1.0 / LOG2E)

    def body(q_ref, k_ref, dq_ref, dk_ref, dv_ref, dz_ref, gq_ref, gk_ref, dp_ref, dgq_ref, dgk_ref, dc_ref):
        i = pl.program_id(0)

        @pl.when(i == 0)
        def _():
            dgq_ref[...] = jnp.zeros_like(dgq_ref)
            dgk_ref[...] = jnp.zeros_like(dgk_ref)

        for n, (src, dsrc, gain, dgain) in enumerate(((q_ref, dq_ref, gq_ref, dgq_ref), (k_ref, dk_ref, gk_ref, dgk_ref))):
            acc = jnp.zeros((1, HEAD), F32)
            for h in range(H):
                sl = slice(h * HEAD, (h + 1) * HEAD)
                t = src[:, sl]
                r = lax.rsqrt(jnp.mean(t * t, axis=-1, keepdims=True) + EPS)
                that = t * r
                dn = dsrc[:, h * AUG:h * AUG + HEAD] * factors[n]
                acc = acc + jnp.sum(dn * that, axis=0, keepdims=True)
                dhat = dn * gain[...]
                dt = r * (dhat - that * jnp.mean(dhat * that, axis=-1, keepdims=True))
                dp_ref[:, n * HW + h * HEAD:n * HW + (h + 1) * HEAD] = dt.astype(BF16)
            dgain[...] += acc
        dp_ref[:, 2 * HW:3 * HW] = dv_ref[...].astype(BF16)
        dp_ref[:, 3 * HW:] = dz_ref[...]
        lane = lax.broadcasted_iota(jnp.int32, (tr, LANES), 1)
        dc = jnp.zeros((tr, LANES), F32)
        for h in range(H):
            qs = dq_ref[:, h * AUG + K_SUM_LANE:h * AUG + K_SUM_LANE + 1]
            ks = dk_ref[:, h * AUG + Q_SUM_LANE:h * AUG + Q_SUM_LANE + 1]
            dc = jnp.where(lane == h, qs - ks, dc)
        dc_ref[...] = dc

    col = lambda c: pl.BlockSpec((tr, HW), lambda i: (i, c))
    row = col(0)
    aug = pl.BlockSpec((tr, H * AUG), lambda i: (i, 0))
    vec = pl.BlockSpec((1, HEAD), lambda i: (0, 0))
    return pl.pallas_call(
        body, name=name, grid=(S // tr,),
        in_specs=[col(0), col(1), aug, aug, row, row, vec, vec],
        out_specs=[pl.BlockSpec((tr, 4 * HW), lambda i: (i, 0)), vec, vec, pl.BlockSpec((tr, LANES), lambda i: (i, 0))],
        out_shape=[jax.ShapeDtypeStruct((S, 4 * HW), BF16), jax.ShapeDtypeStruct((1, HEAD), F32),
                   jax.ShapeDtypeStruct((1, HEAD), F32), jax.ShapeDtypeStruct((S, LANES), F32)],
        compiler_params=_params("arbitrary"),
    )(proj, proj, dqa, dka, dv, dz, q_gain.reshape(1, HEAD), k_gain.reshape(1, HEAD))


def _row_tile(R, C, budget_bytes=1 << 20):
    cap = max(8, budget_bytes // (4 * C))
    t = (min(cap, R) // 8) * 8
    while t >= 8:
        if R % t == 0:
            return t
        t -= 8
    return R


def _add_selected(name, g0, g1, other, sel):
    R, C = other.shape
    tr = _row_tile(R, C)

    def body(sel_ref, a0_ref, a1_ref, b_ref, o_ref):
        @pl.when(sel_ref[0] == 0)
        def _():
            o_ref[...] = a0_ref[...] + b_ref[...]

        @pl.when(sel_ref[0] != 0)
        def _():
            o_ref[...] = a1_ref[...] + b_ref[...]

    grid_spec = pltpu.PrefetchScalarGridSpec(
        num_scalar_prefetch=1, grid=(R // tr,),
        in_specs=[pl.BlockSpec((tr, C), lambda i, s: (i * (1 - s[0]), 0)),
                  pl.BlockSpec((tr, C), lambda i, s: (i * s[0], 0)),
                  pl.BlockSpec((tr, C), lambda i, s: (i, 0))],
        out_specs=pl.BlockSpec((tr, C), lambda i, s: (i, 0)))
    return pl.pallas_call(
        body, name=name, grid_spec=grid_spec, out_shape=jax.ShapeDtypeStruct((R, C), F32),
        compiler_params=_params("arbitrary"),
    )(sel, g0, g1, other)


def _sum_slots(name, slots):
    n, R, C = slots.shape
    tr = _row_tile(R, C, budget_bytes=(1 << 20) // 2)

    def body(s_ref, o_ref):
        acc = s_ref[0]
        for k in range(1, n):
            acc = acc + s_ref[k]
        o_ref[...] = acc

    return pl.pallas_call(
        body, name=name, grid=(R // tr,),
        in_specs=[pl.BlockSpec((n, tr, C), lambda i: (0, i, 0))],
        out_specs=pl.BlockSpec((tr, C), lambda i: (i, 0)),
        out_shape=jax.ShapeDtypeStruct((R, C), F32),
        compiler_params=_params("arbitrary"),
    )(slots)


def _adamw(name, w, g, m, v):
    R, C = w.shape
    tr = _row_tile(R, C, budget_bytes=(1 << 20) // 2)
    c1 = 1.0 - ADAM_B1 ** ADAM_STEP
    c2 = 1.0 - ADAM_B2 ** ADAM_STEP

    def body(w_ref, g_ref, m_ref, v_ref, d_ref, nm_ref, nv_ref):
        gv = g_ref[...]
        nm = ADAM_B1 * m_ref[...] + (1.0 - ADAM_B1) * gv
        nv = ADAM_B2 * v_ref[...] + (1.0 - ADAM_B2) * (gv * gv)
        m_hat = nm / c1
        v_hat = nv / c2
        d_ref[...] = -ADAM_LR * (m_hat / (jnp.sqrt(v_hat) + ADAM_EPS) + ADAM_WD * w_ref[...])
        nm_ref[...] = nm
        nv_ref[...] = nv

    blk = pl.BlockSpec((tr, C), lambda i: (i, 0))
    return pl.pallas_call(
        body, name=name, grid=(R // tr,), in_specs=[blk] * 4, out_specs=[blk] * 3,
        out_shape=[jax.ShapeDtypeStruct((R, C), F32)] * 3,
        compiler_params=_params("arbitrary"),
    )(w, g, m, v)


def _place():
    x, y, c = lax.axis_index("x"), lax.axis_index("y"), lax.axis_index("c")
    chips = [(1 - x, y), (x, 1 - y), (1 - x, 1 - y)]
    return x, y, c, chips


def _gather_weights(shards):
    nt = len(shards)

    def body(*refs):
        ins, outs = refs[:nt], refs[nt:2 * nt]
        s_send, s_recv, f_send, f_recv = refs[2 * nt:]
        x, y, c, chips = _place()
        me = 2 * x + y
        sibling = (x, y, 1 - c)
        ids = [2 * cx + cy for cx, cy in chips]

        def over_ici(t, k, block):
            return pltpu.make_async_remote_copy(
                src_ref=ins[t].at[c], dst_ref=outs[t].at[c, block], send_sem=s_send.at[3 * t + k],
                recv_sem=s_recv.at[3 * t + k], device_id=(*chips[k], c), device_id_type=MESH)

        def over_d2d(t, k, layer):
            blk = outs[t].at[layer, ids[k]]
            return pltpu.make_async_remote_copy(
                src_ref=blk, dst_ref=blk, send_sem=f_send.at[3 * t + k], recv_sem=f_recv.at[3 * t + k],
                device_id=sibling, device_id_type=MESH)

        pairs = [(t, k) for t in range(nt) for k in range(3)]
        for t, k in pairs:
            over_ici(t, k, me).start()
        for t, k in pairs:
            over_ici(t, k, ids[k]).wait_recv()
            over_d2d(t, k, c).start()
        for t, k in pairs:
            over_d2d(t, k, 1 - c).wait_recv()
        for t, k in pairs:
            over_ici(t, k, me).wait_send()
            over_d2d(t, k, c).wait_send()

    return pl.pallas_call(
        body, name="gather_weights",
        out_shape=[jax.ShapeDtypeStruct((2, N_CHIPS) + s.shape[1:], s.dtype) for s in shards],
        in_specs=[HBM_SPEC] * nt, out_specs=[HBM_SPEC] * nt,
        scratch_shapes=[pltpu.SemaphoreType.DMA((3 * nt,))] * 4,
    )(*shards)


def _swap_layers(grads):
    nt = len(grads)

    def body(*refs):
        ins, outs = refs[:2 * nt], refs[2 * nt:3 * nt]
        s_send, s_recv = refs[3 * nt:]
        x, y, c, _ = _place()

        def copy(t, layer):
            return pltpu.make_async_remote_copy(
                src_ref=ins[2 * t + layer], dst_ref=outs[t], send_sem=s_send.at[t], recv_sem=s_recv.at[t],
                device_id=(x, y, 1 - c), device_id_type=MESH)

        for layer in range(2):
            @pl.when(c == 1 - layer)
            def _():
                for t in range(nt):
                    copy(t, layer).start()
        for t in range(nt):
            copy(t, 0).wait()

    flat = [g for pair in grads for g in pair]
    return pl.pallas_call(
        body, name="swap_layers",
        out_shape=[jax.ShapeDtypeStruct(pair[0].shape, F32) for pair in grads],
        in_specs=[HBM_SPEC] * (2 * nt), out_specs=[HBM_SPEC] * nt,
        scratch_shapes=[pltpu.SemaphoreType.DMA((nt,))] * 2,
    )(*flat)


def _scatter_chip_sums(sums):
    nt = len(sums)

    def body(*refs):
        ins, outs = refs[:nt], refs[nt:2 * nt]
        s_send, s_recv = refs[2 * nt:]
        x, y, c, chips = _place()
        me = 2 * x + y
        ids = [2 * cx + cy for cx, cy in chips]

        def copy(t, k, slot):
            return pltpu.make_async_remote_copy(
                src_ref=ins[t].at[ids[k]], dst_ref=outs[t].at[slot], send_sem=s_send.at[3 * t + k],
                recv_sem=s_recv.at[3 * t + k], device_id=(*chips[k], c), device_id_type=MESH)

        pairs = [(t, k) for t in range(nt) for k in range(3)]
        for t, k in pairs:
            copy(t, k, me).start()
        for t, k in pairs:
            copy(t, k, ids[k]).wait()

    return pl.pallas_call(
        body, name="scatter_chip_sums",
        out_shape=[jax.ShapeDtypeStruct(s.shape, F32) for s in sums],
        in_specs=[HBM_SPEC] * nt, out_specs=[HBM_SPEC] * nt,
        scratch_shapes=[pltpu.SemaphoreType.DMA((3 * nt,))] * 2,
    )(*sums)


def _swap_with_sibling(reduced):
    nt = len(reduced)

    def body(*refs):
        ins, outs = refs[:nt], refs[nt:2 * nt]
        s_send, s_recv = refs[2 * nt:]
        x, y, c, _ = _place()

        def copy(t):
            return pltpu.make_async_remote_copy(
                src_ref=ins[t], dst_ref=outs[t], send_sem=s_send.at[t], recv_sem=s_recv.at[t],
                device_id=(x, y, 1 - c), device_id_type=MESH)

        for t in range(nt):
            copy(t).start()
        for t in range(nt):
            copy(t).wait()

    return pl.pallas_call(
        body, name="swap_with_sibling",
        out_shape=[jax.ShapeDtypeStruct(r.shape, F32) for r in reduced],
        in_specs=[HBM_SPEC] * nt, out_specs=[HBM_SPEC] * nt,
        scratch_shapes=[pltpu.SemaphoreType.DMA((nt,))] * 2,
    )(*reduced)


def _gather_small(buf):
    def body(in_ref, out_ref, s_send, s_recv, l_sem):
        x, y, c, _ = _place()
        flips = [(fx, fy, fc) for fx in (0, 1) for fy in (0, 1) for fc in (0, 1)][1:]

        def peer(f):
            return tuple(1 - a if flip else a for a, flip in zip((x, y, c), f))

        def slot(p):
            return 4 * p[0] + 2 * p[1] + p[2]

        local = pltpu.make_async_copy(in_ref, out_ref.at[slot((x, y, c))], l_sem)
        local.start()

        def copy(k, owner):
            return pltpu.make_async_remote_copy(
                src_ref=in_ref, dst_ref=out_ref.at[slot(owner)], send_sem=s_send.at[k], recv_sem=s_recv.at[k],
                device_id=peer(flips[k]), device_id_type=MESH)

        for k in range(7):
            copy(k, (x, y, c)).start()
        for k in range(7):
            copy(k, peer(flips[k])).wait()
        local.wait()

    return pl.pallas_call(
        body, name="gather_small", out_shape=jax.ShapeDtypeStruct((8,) + buf.shape, F32),
        in_specs=[HBM_SPEC], out_specs=HBM_SPEC,
        scratch_shapes=[pltpu.SemaphoreType.DMA((7,))] * 2 + [pltpu.SemaphoreType.DMA],
    )(buf)


def _rows128(a):
    flat = a.reshape(-1)
    rows = -(-flat.shape[0] // LANES)
    rows8 = -(-rows // 8) * 8
    flat = jnp.pad(flat, (0, rows8 * LANES - flat.shape[0]))
    return flat.reshape(rows8, LANES)


def _pack(parts):
    return jnp.concatenate([_rows128(p) for p in parts], axis=0)


def _unpack(buf, shapes):
    out, r = [], 0
    for shp in shapes:
        n = int(np.prod(shp))
        rows8 = -(-(-(-n // LANES)) // 8) * 8
        out.append(buf[r:r + rows8].reshape(-1)[:n].reshape(shp))
        r += rows8
    return out


def kernel(x, a_norm_g, a_w_in, a_v_norm_g, a_w_s, a_b_s, a_w_out, b_norm_g, b_w_in, b_f_bias, b_q_norm_g, b_k_norm_g, b_w_out, loss_target, m_a_norm_g, m_a_w_in, m_a_v_norm_g, m_a_w_s, m_a_b_s, m_a_w_out, m_b_norm_g, m_b_w_in, m_b_f_bias, m_b_q_norm_g, m_b_k_norm_g, m_b_w_out, v_a_norm_g, v_a_w_in, v_a_v_norm_g, v_a_w_s, v_a_b_s, v_a_w_out, v_b_norm_g, v_b_w_in, v_b_f_bias, v_b_q_norm_g, v_b_k_norm_g, v_b_w_out):
    xs = x[0]
    target = loss_target[0]
    S, D = xs.shape
    n_layers = a_w_in.shape[0]
    assert n_layers == 2
    W = a_v_norm_g.shape[1]
    G = a_w_s.shape[1]
    H = b_f_bias.shape[1]
    HW = H * HEAD
    tq_fwd = _tile(S, 512)
    tq_bwd = _tile(S, 512)
    core = lax.axis_index("c")
    chip = 2 * lax.axis_index("x") + lax.axis_index("y")
    sel = core.astype(jnp.int32).reshape(1)

    shards = [a_w_in.astype(BF16), a_w_out.astype(BF16), b_w_in.astype(BF16), b_w_out.astype(BF16),
              b_norm_g.reshape(n_layers, 1, -1)]
    w_ain, w_aout, w_bin, w_bout, bn_all = [
        lax.dynamic_update_slice(full, own[:, None], (0, chip, 0, 0))
        for full, own in zip(_gather_weights(shards), shards)]
    b_norm_full = bn_all.reshape(n_layers, D)
    cb = b_w_in.shape[2]
    w_bin_full = jnp.transpose(w_bin, (0, 2, 1, 3)).reshape(n_layers, D, N_CHIPS * cb)
    w_bmain = w_bin_full[:, :, :4 * HW]
    w_bf = jnp.pad(w_bin_full[:, :, 4 * HW:], ((0, 0), (0, 0), (0, LANES - H)))
    causal = jnp.tril(jnp.ones((CHUNK, CHUNK), dtype=bool))
    wc = jnp.where(causal[None, None], a_w_s, 0).astype(BF16)
    wc_t = jnp.swapaxes(wc, 2, 3)
    bs_t = jnp.swapaxes(a_b_s, 1, 2)
    f_bias = jnp.pad(b_f_bias, ((0, 0), (0, LANES - H))).reshape(n_layers, 1, LANES)

    def view_ain(l):
        return _View(w_ain, "col", (l,))

    def view_aout(l):
        return _View(w_aout, "row", (l,))

    def view_bout(l):
        return _View(w_bout, "row", (l,))

    saved = []
    cur = xs
    for i in range(2 * n_layers):
        l = i // 2
        if i % 2 == 0:
            h = _rmsnorm_fwd(f"a{l}_norm", cur, a_norm_g[l])
            p = _matmul(f"a{l}_in", _View(h), view_ain(l), tm=1024, tn=1024, tk=2048)
            y = _gate_fwd(f"a{l}_gate", p, a_v_norm_g[l], wc[l], bs_t[l])
            nxt = _matmul(f"a{l}_out", _View(y), view_aout(l), tm=1024, tn=1024, tk=1024, residual=cur)
            saved.append((cur, h, p, y))
        else:
            h = _rmsnorm_fwd(f"b{l}_norm", cur, b_norm_full[l])
            proj = _matmul(f"b{l}_in", _View(h), _View(w_bmain[l]), tm=1024, tn=1024, tk=2048)
            f = _matmul(f"b{l}_inf", _View(h), _View(w_bf[l]), tm=1024, tn=LANES, tk=2048)
            cum = _fox_cum(f"b{l}_cum", f, f_bias[l])
            qa, ka, va = _qkv_prep(f"b{l}_qkv", proj, cum, b_q_norm_g[l], b_k_norm_g[l], H)
            o, y, lse = _attn_fwd(f"b{l}_attn", qa, ka, va, proj, H, tq_fwd)
            nxt = _matmul(f"b{l}_out", _View(y), view_bout(l), tm=1024, tn=1024, tk=1024, residual=cur)
            saved.append((cur, h, proj, f, qa, ka, va, o, y, lse))
        cur = nxt

    g, gb, lcols = _loss_grad(cur, target)
    loss = lax.psum(0.5 * jnp.sum(lcols) / D, ("x", "y", "c"))

    big = {"a_w_in": [None] * n_layers, "a_w_out": [None] * n_layers,
           "b_w_in": [None] * n_layers, "b_w_out": [None] * n_layers}
    small = {k: [None] * n_layers for k in
             ("a_norm_g", "a_v_norm_g", "a_w_s", "a_b_s", "b_norm_g", "b_f_bias", "b_q_norm_g", "b_k_norm_g")}
    for i in reversed(range(2 * n_layers)):
        l = i // 2
        if i % 2 == 0:
            x_in, h, p, y = saved[i]
            dy = _matmul(f"a{l}_dy", _View(gb), view_aout(l), tb=True, out_dtype=BF16, tm=1024, tn=1024, tk=2048)
            d_wout = _matmul(f"a{l}_dwout", _View(y), _View(gb), ta=True, tm=2048, tn=1024, tk=512)
            dp, d_ws, d_bs, d_gv = _gate_bwd(f"a{l}_dgate", p, dy, a_v_norm_g[l], wc[l], wc_t[l], bs_t[l])
            dh = _matmul(f"a{l}_dh", _View(dp), view_ain(l), tb=True, tm=1024, tn=1024, tk=1024)
            d_win = _matmul(f"a{l}_dwin", _View(h), _View(dp), ta=True, tm=2048, tn=1024, tk=512,
                            out_colblocks=N_CHIPS)
            g, gb, d_gn = _rmsnorm_bwd(f"a{l}_dnorm", x_in, a_norm_g[l], dh, g)
            big["a_w_in"][l] = d_win
            big["a_w_out"][l] = d_wout.reshape(N_CHIPS, W // N_CHIPS, D)
            small["a_norm_g"][l] = d_gn.reshape(D)
            small["a_v_norm_g"][l] = d_gv.reshape(W)
            small["a_w_s"][l] = jnp.where(causal[None], d_ws, 0.0)
            small["a_b_s"][l] = d_bs[:, :G].T
        else:
            x_in, h, proj, f, qa, ka, va, o, y, lse = saved[i]
            dy = _matmul(f"b{l}_dy", _View(gb), view_bout(l), tb=True, out_dtype=BF16, tm=1024, tn=1024, tk=2048)
            d_wout = _matmul(f"b{l}_dwout", _View(y), _View(gb), ta=True, tm=2048, tn=1024, tk=512)
            lse_lanes = jnp.pad(lse.reshape(H, S).T, ((0, 0), (0, LANES - H)))
            doa, dz, qab = _attn_bwd_prep(f"b{l}_dprep", dy, o, proj, qa, lse_lanes, H)
            dqa, dka, dv = _attn_bwd(f"b{l}_dattn", qab, doa, ka, va, H, tq_bwd)
            dproj, d_gq, d_gk, dcum = _qk_bwd(f"b{l}_dqk", proj, dqa, dka, dv, dz, b_q_norm_g[l], b_k_norm_g[l], H)
            df, d_fb = _fox_cum_bwd(f"b{l}_dcum", dcum, f, f_bias[l])
            dh_f = _matmul(f"b{l}_dhf", _View(df), _View(w_bf[l]), tb=True, tm=1024, tn=1024, tk=LANES)
            dh = _matmul(f"b{l}_dh", _View(dproj), _View(w_bmain[l]), tb=True, tm=1024, tn=1024, tk=1024,
                         residual=dh_f)
            d_wmain = _matmul(f"b{l}_dwin", _View(h), _View(dproj), ta=True, tm=2048, tn=1024, tk=512)
            d_wf = _matmul(f"b{l}_dwinf", _View(h), _View(df), ta=True, tm=2048, tn=LANES, tk=512)
            d_win = jnp.concatenate([d_wmain, d_wf[:, :H]], axis=1)
            g, gb, d_gn = _rmsnorm_bwd(f"b{l}_dnorm", x_in, b_norm_full[l], dh, g)
            big["b_w_in"][l] = jnp.transpose(d_win.reshape(D, N_CHIPS, cb), (1, 0, 2))
            big["b_w_out"][l] = d_wout.reshape(N_CHIPS, HW // N_CHIPS, D)
            small["b_norm_g"][l] = d_gn.reshape(D)
            small["b_f_bias"][l] = d_fb[0, :H]
            small["b_q_norm_g"][l] = d_gq.reshape(HEAD)
            small["b_k_norm_g"][l] = d_gk.reshape(HEAD)
    grad_x = g[None]

    names = ["a_w_in", "a_w_out", "b_w_in", "b_w_out"]
    pairs = [tuple(big[n]) for n in names]
    from_sibling = _swap_layers(pairs)
    chip_sums = []
    for n, pair, other in zip(names, pairs, from_sibling):
        shp = other.shape
        flat = lambda a: a.reshape(shp[0] * shp[1], shp[2])
        chip_sums.append(_add_selected(f"chipsum_{n}", flat(pair[0]), flat(pair[1]), flat(other), sel).reshape(shp))
    slots = [lax.dynamic_update_slice(got, lax.dynamic_index_in_dim(mine, chip, keepdims=True), (chip, 0, 0))
             for got, mine in zip(_scatter_chip_sums(chip_sums), chip_sums)]
    reduced = [_sum_slots(f"reduce_{n}", s) for n, s in zip(names, slots)]
    others = _swap_with_sibling(reduced)
    grads = {n: jnp.where(core == 0, jnp.stack([mine, other]), jnp.stack([other, mine]))
             for n, mine, other in zip(names, reduced, others)}

    small_names = ["a_norm_g", "a_v_norm_g", "a_w_s", "a_b_s", "b_norm_g", "b_f_bias", "b_q_norm_g", "b_k_norm_g"]
    small_parts = [jnp.stack(small[n]) for n in small_names]
    small_sum = _sum_slots("reduce_small", _gather_small(_pack(small_parts)))
    for n, a in zip(small_names, _unpack(small_sum, [p.shape for p in small_parts])):
        grads[n] = a
    nb = b_norm_g.shape[1]
    grads["b_norm_g"] = lax.dynamic_slice_in_dim(grads["b_norm_g"], chip * nb, nb, axis=1)

    weights = dict(a_norm_g=a_norm_g, a_w_in=a_w_in, a_v_norm_g=a_v_norm_g, a_w_s=a_w_s, a_b_s=a_b_s,
                   a_w_out=a_w_out, b_norm_g=b_norm_g, b_w_in=b_w_in, b_f_bias=b_f_bias,
                   b_q_norm_g=b_q_norm_g, b_k_norm_g=b_k_norm_g, b_w_out=b_w_out)
    mom1 = dict(a_norm_g=m_a_norm_g, a_w_in=m_a_w_in, a_v_norm_g=m_a_v_norm_g, a_w_s=m_a_w_s, a_b_s=m_a_b_s,
                a_w_out=m_a_w_out, b_norm_g=m_b_norm_g, b_w_in=m_b_w_in, b_f_bias=m_b_f_bias,
                b_q_norm_g=m_b_q_norm_g, b_k_norm_g=m_b_k_norm_g, b_w_out=m_b_w_out)
    mom2 = dict(a_norm_g=v_a_norm_g, a_w_in=v_a_w_in, a_v_norm_g=v_a_v_norm_g, a_w_s=v_a_w_s, a_b_s=v_a_b_s,
                a_w_out=v_a_w_out, b_norm_g=v_b_norm_g, b_w_in=v_b_w_in, b_f_bias=v_b_f_bias,
                b_q_norm_g=v_b_q_norm_g, b_k_norm_g=v_b_k_norm_g, b_w_out=v_b_w_out)
    order = ["a_norm_g", "a_w_in", "a_v_norm_g", "a_w_s", "a_b_s", "a_w_out", "b_norm_g", "b_w_in", "b_f_bias",
             "b_q_norm_g", "b_k_norm_g", "b_w_out"]
    delta, new_m, new_v = {}, {}, {}
    for n in names:
        shp = weights[n].shape
        flat = lambda a: a.reshape(shp[0] * shp[1], shp[2])
        d, nm, nv = _adamw(f"adamw_{n}", flat(weights[n]), flat(grads[n]), flat(mom1[n]), flat(mom2[n]))
        delta[n], new_m[n], new_v[n] = d.reshape(shp), nm.reshape(shp), nv.reshape(shp)
    small_shapes = [weights[n].shape for n in small_names]
    pack_w, pack_g, pack_m, pack_v = (_pack([d[n] for n in small_names]) for d in (weights, grads, mom1, mom2))
    d, nm, nv = _adamw("adamw_small", pack_w, pack_g, pack_m, pack_v)
    for dst, buf in ((delta, d), (new_m, nm), (new_v, nv)):
        for n, a in zip(small_names, _unpack(buf, small_shapes)):
            dst[n] = a

    return (loss, grad_x, *[grads[n] for n in order], *[delta[n] for n in order],
            *[new_m[n] for n in order], *[new_v[n] for n in order])
```

```python
import functools
import math

import numpy as np
import jax
import jax.numpy as jnp
from jax import lax
from jax.experimental import pallas as pl
from jax.experimental.pallas import tpu as pltpu

F32 = jnp.float32
BF16 = jnp.bfloat16
MESH = pl.DeviceIdType.MESH

EPS = 1e-6
CHUNK = 128
HEAD = 128
LANES = 128
N_CHIPS = 4
VMEM_LIMIT = 56 * 1024 * 1024

ADAM_LR = 0.001
ADAM_B1 = 0.9
ADAM_B2 = 0.999
ADAM_EPS = 1e-08
ADAM_WD = 0.01
ADAM_STEP = 10

_NT = (((1,), (1,)), ((), ()))
_TN = (((0,), (0,)), ((), ()))
_GELU_C = math.sqrt(2.0 / math.pi)

HBM_SPEC = pl.BlockSpec(memory_space=pltpu.HBM)


def _params(*sem):
    return pltpu.CompilerParams(dimension_semantics=sem, vmem_limit_bytes=VMEM_LIMIT)


def _tile(dim, pref, unit=LANES):
    t = (min(pref, dim) // unit) * unit
    while t >= unit:
        if dim % t == 0:
            return t
        t -= unit
    return dim


def _gelu(x):
    return 0.5 * x * (1.0 + jnp.tanh(_GELU_C * (x + 0.044715 * (x * x * x))))


def _gelu_and_grad(x):
    x2 = x * x
    t = jnp.tanh(_GELU_C * (x + 0.044715 * (x2 * x)))
    val = 0.5 * x * (1.0 + t)
    grad = 0.5 * (1.0 + t) + 0.5 * x * (1.0 - t * t) * (_GELU_C * (1.0 + 3.0 * 0.044715 * x2))
    return val, grad


def _sigmoid(x):
    return 1.0 / (1.0 + jnp.exp(-x))


class _View:
    def __init__(self, arr, kind="2d", lead=()):
        self.arr, self.kind, self.lead = arr, kind, tuple(lead)
        shp = arr.shape[len(self.lead):]
        if kind == "2d":
            self.R, self.C = shp
        elif kind == "col":
            self.nb, self.R, self.cb = shp
            self.C = self.nb * self.cb
        else:
            self.nb, self.rb, self.C = shp
            self.R = self.nb * self.rb

    def fit(self, tr, tc):
        if self.kind == "col":
            tc = _tile(self.cb, tc)
        elif self.kind == "row":
            tr = _tile(self.rb, tr, unit=8)
        return tr, tc

    def spec(self, tr, tc, rc_of_grid):
        lead = self.lead
        sq = (None,) * len(lead)
        if self.kind == "2d":
            return pl.BlockSpec(sq + (tr, tc), lambda *g: lead + tuple(rc_of_grid(*g)))
        if self.kind == "col":
            q = self.cb // tc

            def im(*g):
                r, c = rc_of_grid(*g)
                return lead + (c // q, r, c % q)

            return pl.BlockSpec(sq + (None, tr, tc), im)
        q = self.rb // tr

        def im(*g):
            r, c = rc_of_grid(*g)
            return lead + (r // q, r % q, c)

        return pl.BlockSpec(sq + (None, tr, tc), im)


def _matmul(name, a, b, *, ta=False, tb=False, out_dtype=F32, tm=1024, tn=1024, tk=1024,
            out_colblocks=None, residual=None):
    M, K = (a.C, a.R) if ta else (a.R, a.C)
    N, K2 = (b.R, b.C) if tb else (b.C, b.R)
    assert K == K2, (name, K, K2)
    tm, tn, tk = _tile(M, tm), _tile(N, tn), _tile(K, tk)
    if ta:
        tk, tm = a.fit(tk, tm)
    else:
        tm, tk = a.fit(tm, tk)
    if tb:
        tn, tk2 = b.fit(tn, tk)
    else:
        tk2, tn = b.fit(tk, tn)
    if tk2 != tk:
        tk = min(tk, tk2)
        if ta:
            tk, tm = a.fit(tk, tm)
        else:
            tm, tk = a.fit(tm, tk)
    if out_colblocks:
        tn = _tile(N // out_colblocks, tn)
    assert M % tm == 0 and N % tn == 0 and K % tk == 0, (name, M, N, K, tm, tn, tk)
    nk = K // tk
    dims = (((0 if ta else 1,), (1 if tb else 0,)), ((), ()))

    def body(*refs):
        if residual is not None:
            a_ref, b_ref, r_ref, o_ref, acc_ref = refs
        else:
            a_ref, b_ref, o_ref, acc_ref = refs
            r_ref = None
        k = pl.program_id(2)
        part = lax.dot_general(a_ref[...], b_ref[...], dims, preferred_element_type=F32)

        def finish(total):
            if r_ref is not None:
                total = total + r_ref[...]
            o_ref[...] = total.astype(out_dtype)

        if nk == 1:
            finish(part)
        else:
            @pl.when(k == 0)
            def _():
                acc_ref[...] = part

            @pl.when(jnp.logical_and(k > 0, k < nk - 1))
            def _():
                acc_ref[...] += part

            @pl.when(k == nk - 1)
            def _():
                finish(acc_ref[...] + part)

    a_spec = a.spec(tk, tm, lambda i, j, k: (k, i)) if ta else a.spec(tm, tk, lambda i, j, k: (i, k))
    b_spec = b.spec(tn, tk, lambda i, j, k: (j, k)) if tb else b.spec(tk, tn, lambda i, j, k: (k, j))
    in_specs, args = [a_spec, b_spec], [a.arr, b.arr]
    if residual is not None:
        in_specs.append(pl.BlockSpec((tm, tn), lambda i, j, k: (i, j)))
        args.append(residual)
    if out_colblocks:
        q = (N // out_colblocks) // tn
        out_shape = jax.ShapeDtypeStruct((out_colblocks, M, N // out_colblocks), out_dtype)
        out_spec = pl.BlockSpec((None, tm, tn), lambda i, j, k: (j // q, i, j % q))
    else:
        out_shape = jax.ShapeDtypeStruct((M, N), out_dtype)
        out_spec = pl.BlockSpec((tm, tn), lambda i, j, k: (i, j))
    acc_shape = (tm, tn) if nk > 1 else (8, LANES)
    return pl.pallas_call(
        body, name=name, grid=(M // tm, N // tn, nk), in_specs=in_specs, out_specs=out_spec,
        out_shape=out_shape, scratch_shapes=[pltpu.VMEM(acc_shape, F32)],
        compiler_params=_params("arbitrary", "arbitrary", "arbitrary"),
    )(*args)


def _rmsnorm_fwd(name, x, gain):
    S, D = x.shape
    tr = _tile(S, 512)

    def body(x_ref, g_ref, h_ref, ht_ref):
        xv = x_ref[...]
        r = lax.rsqrt(jnp.mean(xv * xv, axis=-1, keepdims=True) + EPS)
        h = xv * r * g_ref[...]
        h_ref[...] = h.astype(BF16)
        ht_ref[...] = h.T.astype(BF16)

    return pl.pallas_call(
        body, name=name, grid=(S // tr,),
        in_specs=[pl.BlockSpec((tr, D), lambda i: (i, 0)), pl.BlockSpec((1, D), lambda i: (0, 0))],
        out_specs=[pl.BlockSpec((tr, D), lambda i: (i, 0)), pl.BlockSpec((D, tr), lambda i: (0, i))],
        out_shape=[jax.ShapeDtypeStruct((S, D), BF16), jax.ShapeDtypeStruct((D, S), BF16)],
        compiler_params=_params("arbitrary"),
    )(x, gain.reshape(1, D))


def _rmsnorm_bwd(name, x, gain, dh, g_res):
    S, D = x.shape
    tr = _tile(S, 256, unit=8)

    def body(x_ref, g_ref, dh_ref, res_ref, dx_ref, dxb_ref, dg_ref):
        i = pl.program_id(0)
        xv = x_ref[...]
        r = lax.rsqrt(jnp.mean(xv * xv, axis=-1, keepdims=True) + EPS)
        xhat = xv * r
        dhv = dh_ref[...]
        part = jnp.sum(dhv * xhat, axis=0, keepdims=True)

        @pl.when(i == 0)
        def _():
            dg_ref[...] = part

        @pl.when(i > 0)
        def _():
            dg_ref[...] += part

        dxhat = dhv * g_ref[...]
        dx = res_ref[...] + r * (dxhat - xhat * jnp.mean(dxhat * xhat, axis=-1, keepdims=True))
        dx_ref[...] = dx
        dxb_ref[...] = dx.astype(BF16)

    row = pl.BlockSpec((tr, D), lambda i: (i, 0))
    vec = pl.BlockSpec((1, D), lambda i: (0, 0))
    return pl.pallas_call(
        body, name=name, grid=(S // tr,), in_specs=[row, vec, row, row], out_specs=[row, row, vec],
        out_shape=[jax.ShapeDtypeStruct((S, D), F32), jax.ShapeDtypeStruct((S, D), BF16),
                   jax.ShapeDtypeStruct((1, D), F32)],
        compiler_params=_params("arbitrary"),
    )(x, gain.reshape(1, D), dh, g_res)


def _loss_grad(x, target):
    S, D = x.shape
    tr = _tile(S, 512, unit=8)

    def body(x_ref, t_ref, g_ref, gb_ref, l_ref):
        i = pl.program_id(0)
        e = x_ref[...] - t_ref[...]
        g = e * (1.0 / D)
        g_ref[...] = g
        gb_ref[...] = g.astype(BF16)
        part = jnp.sum(e * e, axis=0, keepdims=True)

        @pl.when(i == 0)
        def _():
            l_ref[...] = part

        @pl.when(i > 0)
        def _():
            l_ref[...] += part

    row = pl.BlockSpec((tr, D), lambda i: (i, 0))
    vec = pl.BlockSpec((1, D), lambda i: (0, 0))
    return pl.pallas_call(
        body, name="loss_grad", grid=(S // tr,), in_specs=[row, row], out_specs=[row, row, vec],
        out_shape=[jax.ShapeDtypeStruct((S, D), F32), jax.ShapeDtypeStruct((S, D), BF16),
                   jax.ShapeDtypeStruct((1, D), F32)],
        compiler_params=_params("arbitrary"),
    )(x, target)


def _gate_fwd(name, p, v_gain, wc, bs_t):
    S, W3 = p.shape
    W = W3 // 3
    G = wc.shape[0]
    gd = W // G

    def body(p_ref, gv_ref, wc_ref, bs_ref, y_ref):
        vg = _gelu(p_ref[:, W:2 * W])
        r = lax.rsqrt(jnp.mean(vg * vg, axis=-1, keepdims=True) + EPS)
        vb = (vg * r * gv_ref[...]).astype(BF16)
        zp = p_ref[:, 2 * W:]
        gate = _gelu(p_ref[:, :W]) * (zp * _sigmoid(zp))
        for g in range(G):
            sl = slice(g * gd, (g + 1) * gd)
            mixed = jnp.dot(wc_ref[g], vb[:, sl], preferred_element_type=F32) + bs_ref[:, g:g + 1]
            y_ref[:, sl] = (gate[:, sl] * mixed).astype(BF16)

    return pl.pallas_call(
        body, name=name, grid=(S // CHUNK,),
        in_specs=[pl.BlockSpec((CHUNK, W3), lambda i: (i, 0)), pl.BlockSpec((1, W), lambda i: (0, 0)),
                  pl.BlockSpec((G, CHUNK, CHUNK), lambda i: (0, 0, 0)), pl.BlockSpec((CHUNK, G), lambda i: (0, 0))],
        out_specs=pl.BlockSpec((CHUNK, W), lambda i: (i, 0)),
        out_shape=jax.ShapeDtypeStruct((S, W), BF16),
        compiler_params=_params("arbitrary"),
    )(p, v_gain.reshape(1, W), wc, bs_t)


def _gate_bwd(name, p, dy, v_gain, wc, wc_t, bs_t):
    S, W3 = p.shape
    W = W3 // 3
    G = wc.shape[0]
    gd = W // G

    def body(p_ref, dy_ref, gv_ref, wc_ref, wct_ref, bs_ref, dp_ref, dws_ref, dbs_ref, dgv_ref, dv_scr):
        i = pl.program_id(0)

        @pl.when(i == 0)
        def _():
            dws_ref[...] = jnp.zeros_like(dws_ref)
            dbs_ref[...] = jnp.zeros_like(dbs_ref)
            dgv_ref[...] = jnp.zeros_like(dgv_ref)

        gu, dgu = _gelu_and_grad(p_ref[:, :W])
        vg, dvg_dv = _gelu_and_grad(p_ref[:, W:2 * W])
        zp = p_ref[:, 2 * W:]
        sig = _sigmoid(zp)
        sz = zp * sig
        dsz = sig * (1.0 + zp * (1.0 - sig))
        r = lax.rsqrt(jnp.mean(vg * vg, axis=-1, keepdims=True) + EPS)
        vhat = vg * r
        gv = gv_ref[...]
        vb = (vhat * gv).astype(BF16)
        dy = dy_ref[...].astype(F32)
        lane = lax.broadcasted_iota(jnp.int32, (CHUNK, LANES), 1)
        dbs = jnp.zeros((CHUNK, LANES), F32)
        for g in range(G):
            sl = slice(g * gd, (g + 1) * gd)
            vsl = vb[:, sl]
            mixed = jnp.dot(wc_ref[g], vsl, preferred_element_type=F32) + bs_ref[:, g:g + 1]
            dyg, gug, szg = dy[:, sl], gu[:, sl], sz[:, sl]
            dp_ref[:, sl] = (dyg * mixed * szg * dgu[:, sl]).astype(BF16)
            dp_ref[:, 2 * W + g * gd:2 * W + (g + 1) * gd] = (dyg * gug * mixed * dsz[:, sl]).astype(BF16)
            dm = dyg * gug * szg
            dmb = dm.astype(BF16)
            dws_ref[g] += lax.dot_general(dmb, vsl, _NT, preferred_element_type=F32)
            dbs = dbs + jnp.where(lane == g, jnp.sum(dm, axis=1, keepdims=True), 0.0)
            dv_scr[:, sl] = jnp.dot(wct_ref[g], dmb, preferred_element_type=F32)
        dbs_ref[...] += dbs
        dv = dv_scr[...]
        dgv_ref[...] += jnp.sum(dv * vhat, axis=0, keepdims=True)
        dvhat = dv * gv
        dvg = r * (dvhat - vhat * jnp.mean(dvhat * vhat, axis=-1, keepdims=True))
        dp_ref[:, W:2 * W] = (dvg * dvg_dv).astype(BF16)

    return pl.pallas_call(
        body, name=name, grid=(S // CHUNK,),
        in_specs=[pl.BlockSpec((CHUNK, W3), lambda i: (i, 0)), pl.BlockSpec((CHUNK, W), lambda i: (i, 0)),
                  pl.BlockSpec((1, W), lambda i: (0, 0)),
                  pl.BlockSpec((G, CHUNK, CHUNK), lambda i: (0, 0, 0)),
                  pl.BlockSpec((G, CHUNK, CHUNK), lambda i: (0, 0, 0)),
                  pl.BlockSpec((CHUNK, G), lambda i: (0, 0))],
        out_specs=[pl.BlockSpec((CHUNK, W3), lambda i: (i, 0)),
                   pl.BlockSpec((G, CHUNK, CHUNK), lambda i: (0, 0, 0)),
                   pl.BlockSpec((CHUNK, LANES), lambda i: (0, 0)),
                   pl.BlockSpec((1, W), lambda i: (0, 0))],
        out_shape=[jax.ShapeDtypeStruct((S, W3), BF16), jax.ShapeDtypeStruct((G, CHUNK, CHUNK), F32),
                   jax.ShapeDtypeStruct((CHUNK, LANES), F32), jax.ShapeDtypeStruct((1, W), F32)],
        scratch_shapes=[pltpu.VMEM((CHUNK, W), F32)],
        compiler_params=_params("arbitrary"),
    )(p, dy, v_gain.reshape(1, W), wc, wc_t, bs_t)


AUG = 2 * HEAD
LOG2E = 1.0 / math.log(2.0)
Q_SUM_LANE = HEAD + 3
K_SUM_LANE = HEAD


def _pieces(x, sign=1.0):
    hi, mid, lo = _split3(sign * x)
    return hi.astype(F32), mid.astype(F32), lo.astype(F32)


def _lanes(lane, start, vals, rest):
    out = rest
    for n, v in enumerate(vals):
        out = jnp.where(lane == start + n, v, out)
    return out


def _qkv_prep(name, proj, cum, q_gain, k_gain, H):
    S = proj.shape[0]
    HW = H * HEAD
    tr = _tile(S, 256, unit=8)
    sigma = (HEAD ** -0.5) * LOG2E

    def body(q_ref, k_ref, v_ref, c_ref, gq_ref, gk_ref, qa_ref, ka_ref, va_ref):
        lane = lax.broadcasted_iota(jnp.int32, (tr, HEAD), 1)
        zero = jnp.zeros((tr, HEAD), F32)
        v_aug = jnp.where(lane < 3, 1.0, zero).astype(BF16)
        for h in range(H):
            sl = slice(h * HEAD, (h + 1) * HEAD)
            a0 = h * AUG
            t = q_ref[:, sl]
            r = lax.rsqrt(jnp.mean(t * t, axis=-1, keepdims=True) + EPS)
            qa_ref[:, a0:a0 + HEAD] = (t * r * gq_ref[...] * sigma).astype(BF16)
            t = k_ref[:, sl]
            r = lax.rsqrt(jnp.mean(t * t, axis=-1, keepdims=True) + EPS)
            ka_ref[:, a0:a0 + HEAD] = (t * r * gk_ref[...]).astype(BF16)
            va_ref[:, a0:a0 + HEAD] = v_ref[:, sl].astype(BF16)
            va_ref[:, a0 + HEAD:a0 + AUG] = v_aug
            c2 = c_ref[:, h:h + 1] * LOG2E
            qa_ref[:, a0 + HEAD:a0 + AUG] = _lanes(lane, 0, _pieces(c2) + (1.0, 1.0, 1.0), zero).astype(BF16)
            ka_ref[:, a0 + HEAD:a0 + AUG] = _lanes(
                lane, 0, (1.0, 1.0, 1.0) + _pieces(c2, -1.0) + (1.0, 1.0, 1.0), zero).astype(BF16)

    col = lambda c: pl.BlockSpec((tr, HW), lambda i: (i, c))
    vec = pl.BlockSpec((1, HEAD), lambda i: (0, 0))
    aug = pl.BlockSpec((tr, H * AUG), lambda i: (i, 0))
    return pl.pallas_call(
        body, name=name, grid=(S // tr,),
        in_specs=[col(0), col(1), col(2), pl.BlockSpec((tr, LANES), lambda i: (i, 0)), vec, vec],
        out_specs=[aug] * 3, out_shape=[jax.ShapeDtypeStruct((S, H * AUG), BF16)] * 3,
        compiler_params=_params("arbitrary"),
    )(proj, proj, proj, cum, q_gain.reshape(1, HEAD), k_gain.reshape(1, HEAD))


def _split3(x):
    hi = x.astype(BF16)
    r1 = x - hi.astype(F32)
    mid = r1.astype(BF16)
    lo = (r1 - mid.astype(F32)).astype(BF16)
    return hi, mid, lo


def _tri_sum(tri, x):
    hi, mid, lo = _split3(x)
    d = lambda t: jnp.dot(tri, t, preferred_element_type=F32)
    return d(hi) + (d(mid) + d(lo))


def _log_sigmoid(x):
    return jnp.minimum(x, 0.0) - jnp.log(1.0 + jnp.exp(-jnp.abs(x)))


def _fox_cum(name, f, bias):
    S = f.shape[0]
    tb = _tile(S, 256, unit=8)

    def body(f_ref, b_ref, c_ref):
        rr = lax.broadcasted_iota(jnp.int32, (tb, tb), 0)
        cc = lax.broadcasted_iota(jnp.int32, (tb, tb), 1)
        tri = (rr >= cc).astype(BF16)

        def step(t, carry):
            off = pl.multiple_of(t * tb, tb)
            lf = _log_sigmoid(f_ref[pl.ds(off, tb), :] + b_ref[...])
            c = _tri_sum(tri, lf) + carry
            c_ref[pl.ds(off, tb), :] = c
            return c[tb - 1:tb, :]

        lax.fori_loop(0, S // tb, step, jnp.zeros((1, LANES), F32))

    return pl.pallas_call(
        body, name=name, out_shape=jax.ShapeDtypeStruct((S, LANES), F32),
        in_specs=[pl.BlockSpec(memory_space=pltpu.VMEM)] * 2, out_specs=pl.BlockSpec(memory_space=pltpu.VMEM),
        compiler_params=pltpu.CompilerParams(vmem_limit_bytes=VMEM_LIMIT),
    )(f, bias)


def _fox_cum_bwd(name, dcum, f, bias):
    S = f.shape[0]
    tb = _tile(S, 256, unit=8)
    nb = S // tb

    def body(dc_ref, f_ref, b_ref, df_ref, db_ref):
        rr = lax.broadcasted_iota(jnp.int32, (tb, tb), 0)
        cc = lax.broadcasted_iota(jnp.int32, (tb, tb), 1)
        tri = (rr <= cc).astype(BF16)

        def step(t, carry):
            tail, dbias = carry
            off = pl.multiple_of((nb - 1 - t) * tb, tb)
            dlf = _tri_sum(tri, dc_ref[pl.ds(off, tb), :]) + tail
            d = dlf * _sigmoid(-(f_ref[pl.ds(off, tb), :] + b_ref[...]))
            df_ref[pl.ds(off, tb), :] = d.astype(BF16)
            return dlf[0:1, :], dbias + jnp.sum(d, axis=0, keepdims=True)

        z = jnp.zeros((1, LANES), F32)
        _, dbias = lax.fori_loop(0, nb, step, (z, z))
        db_ref[...] = dbias

    vm = pl.BlockSpec(memory_space=pltpu.VMEM)
    return pl.pallas_call(
        body, name=name, out_shape=[jax.ShapeDtypeStruct((S, LANES), BF16), jax.ShapeDtypeStruct((1, LANES), F32)],
        in_specs=[vm] * 3, out_specs=[vm] * 2,
        compiler_params=pltpu.CompilerParams(vmem_limit_bytes=VMEM_LIMIT),
    )(dcum, f, bias)


def _attn_fwd(name, qa, ka, va, proj, H, tq):
    S = qa.shape[0]
    HW = H * HEAD
    nq = S // tq
    hp = 2 if H % 2 == 0 else 1

    def body(q_ref, k_ref, v_ref, z_ref, o_ref, y_ref, lse_ref):
        i = pl.program_id(1)

        def step(j, carry, masked):
            off = pl.multiple_of(j * tq, tq)
            out = []
            for n in range(hp):
                m, acc = carry[n]
                a = slice(n * AUG, (n + 1) * AUG)
                s = lax.dot_general(q_ref[:, a], k_ref[pl.ds(off, tq), a], _NT, preferred_element_type=F32)
                if masked:
                    rr = lax.broadcasted_iota(jnp.int32, (tq, tq), 0)
                    cc = lax.broadcasted_iota(jnp.int32, (tq, tq), 1)
                    s = jnp.where(rr >= cc, s, -jnp.inf)
                m_new = jnp.maximum(m, jnp.max(s, axis=1, keepdims=True))
                pr = jnp.exp2(s - m_new).astype(BF16)
                acc = jnp.exp2(m - m_new) * acc + jnp.dot(pr, v_ref[pl.ds(off, tq), a], preferred_element_type=F32)
                out.append((m_new, acc))
            return tuple(out)

        init = ((jnp.full((tq, 1), -jnp.inf, F32), jnp.zeros((tq, AUG), F32)),) * hp
        carry = lax.fori_loop(0, i, lambda j, c: step(j, c, False), init)
        carry = step(i, carry, True)
        for n in range(hp):
            m, acc = carry[n]
            sl = slice(n * HEAD, (n + 1) * HEAD)
            l = acc[:, HEAD:HEAD + 1]
            o = acc[:, :HEAD] / l
            z = z_ref[:, sl]
            o_ref[:, sl] = o
            y_ref[:, sl] = (o * (z * _sigmoid(z))).astype(BF16)
            lse_ref[n] = m + jnp.log(l) * LOG2E

    qspec = pl.BlockSpec((tq, hp * AUG), lambda h, i: (i, h))
    kvspec = pl.BlockSpec((S, hp * AUG), lambda h, i: (0, h))
    ospec = pl.BlockSpec((tq, hp * HEAD), lambda h, i: (i, h))
    return pl.pallas_call(
        body, name=name, grid=(H // hp, nq),
        in_specs=[qspec, kvspec, kvspec, pl.BlockSpec((tq, hp * HEAD), lambda h, i: (i, 3 * H // hp + h))],
        out_specs=[ospec, ospec, pl.BlockSpec((hp, tq, 1), lambda h, i: (h, i, 0))],
        out_shape=[jax.ShapeDtypeStruct((S, HW), F32), jax.ShapeDtypeStruct((S, HW), BF16),
                   jax.ShapeDtypeStruct((H, S, 1), F32)],
        compiler_params=_params("arbitrary", "arbitrary"),
    )(qa, ka, va, proj)


def _attn_bwd_prep(name, dy, o, proj, qa, lse, H):
    S, HW = o.shape
    tr = _tile(S, 256, unit=8)

    def body(dy_ref, o_ref, z_ref, qa_ref, lse_ref, doa_ref, dz_ref, qab_ref):
        lane = lax.broadcasted_iota(jnp.int32, (tr, HEAD), 1)
        zero = jnp.zeros((tr, HEAD), F32)
        for h in range(H):
            sl = slice(h * HEAD, (h + 1) * HEAD)
            a0 = h * AUG
            dy = dy_ref[:, sl].astype(F32)
            z = z_ref[:, sl]
            o = o_ref[:, sl]
            sig = _sigmoid(z)
            dob = (dy * (z * sig)).astype(BF16)
            dz_ref[:, sl] = (dy * o * (sig * (1.0 + z * (1.0 - sig)))).astype(BF16)
            delta = jnp.sum(dob.astype(F32) * o, axis=1, keepdims=True)
            doa_ref[:, a0:a0 + HEAD] = dob
            doa_ref[:, a0 + HEAD:a0 + AUG] = _lanes(lane, 0, _pieces(delta, -1.0), zero).astype(BF16)
            qab_ref[:, a0:a0 + HEAD] = qa_ref[:, a0:a0 + HEAD]
            qab_ref[:, a0 + HEAD:a0 + AUG] = _lanes(
                lane, 6, _pieces(lse_ref[:, h:h + 1], -1.0), qa_ref[:, a0 + HEAD:a0 + AUG].astype(F32)).astype(BF16)

    row = pl.BlockSpec((tr, HW), lambda i: (i, 0))
    aug = pl.BlockSpec((tr, H * AUG), lambda i: (i, 0))
    return pl.pallas_call(
        body, name=name, grid=(S // tr,),
        in_specs=[row, row, pl.BlockSpec((tr, HW), lambda i: (i, 3)), aug, pl.BlockSpec((tr, LANES), lambda i: (i, 0))],
        out_specs=[aug, row, aug],
        out_shape=[jax.ShapeDtypeStruct((S, H * AUG), BF16), jax.ShapeDtypeStruct((S, HW), BF16),
                   jax.ShapeDtypeStruct((S, H * AUG), BF16)],
        compiler_params=_params("arbitrary"),
    )(dy, o, proj, qa, lse)


def _attn_bwd(name, qab, doa, ka, va, H, tq, carried=None):
    S = qab.shape[0]
    nq = S // tq
    sums, owner = carried if carried else ([], None)
    nt = len(sums)

    def body(*refs):
        q_ref, do_ref, k_ref, v_ref = refs[:4]
        dq_ref, dk_ref, dv_ref = refs[4 + nt:7 + nt]
        j = pl.program_id(1)
        if carried:
            mine, start, wait = _scatter_copies(refs[4:4 + nt], refs[7 + nt:7 + 2 * nt], *refs[7 + 2 * nt:], owner)
            hd = pl.program_id(0)

            @pl.when(jnp.logical_and(mine, jnp.logical_and(hd == 0, j == 0)))
            def _():
                start()

        @pl.when(j == 0)
        def _():
            dq_ref[...] = jnp.zeros_like(dq_ref)

        k = k_ref[...]
        v = v_ref[...]

        def step(i, carry, masked):
            dk_acc, dv_acc = carry
            off = pl.multiple_of(i * tq, tq)
            q = q_ref[pl.ds(off, tq), :]
            do = do_ref[pl.ds(off, tq), :]
            st = lax.dot_general(k, q, _NT, preferred_element_type=F32)
            if masked:
                rr = lax.broadcasted_iota(jnp.int32, (tq, tq), 0)
                cc = lax.broadcasted_iota(jnp.int32, (tq, tq), 1)
                st = jnp.where(cc >= rr, st, -jnp.inf)
            pt = jnp.exp2(st)
            dst = pt * lax.dot_general(v, do, _NT, preferred_element_type=F32)
            dsb = dst.astype(BF16)
            dv_acc = dv_acc + jnp.dot(pt.astype(BF16), do[:, :HEAD], preferred_element_type=F32)
            dk_acc = dk_acc + jnp.dot(dsb, q, preferred_element_type=F32)
            dq_ref[pl.ds(off, tq), :] += lax.dot_general(dsb, k, _TN, preferred_element_type=F32)
            return dk_acc, dv_acc

        carry = step(j, (jnp.zeros((tq, AUG), F32), jnp.zeros((tq, HEAD), F32)), True)
        dk_acc, dv_acc = lax.fori_loop(j + 1, nq, lambda i, c: step(i, c, False), carry)
        dk_ref[...] = dk_acc
        dv_ref[...] = dv_acc
        if carried:
            @pl.when(jnp.logical_and(mine, jnp.logical_and(hd == H - 1, j == nq - 1)))
            def _():
                wait()

    full = pl.BlockSpec((S, AUG), lambda h, j: (0, h))
    blk = pl.BlockSpec((tq, AUG), lambda h, j: (j, h))
    out = pl.pallas_call(
        body, name=name, grid=(H, nq),
        in_specs=[full, full, blk, blk] + [HBM_SPEC] * nt,
        out_specs=[full, blk, pl.BlockSpec((tq, HEAD), lambda h, j: (j, h))] + [HBM_SPEC] * nt,
        out_shape=[jax.ShapeDtypeStruct((S, H * AUG), F32), jax.ShapeDtypeStruct((S, H * AUG), F32),
                   jax.ShapeDtypeStruct((S, H * HEAD), F32)] + [jax.ShapeDtypeStruct(s.shape, F32) for s in sums],
        scratch_shapes=[pltpu.SemaphoreType.DMA((3 * nt,))] * 2 if carried else [],
        compiler_params=_params("arbitrary", "arbitrary"),
    )(qab, doa, ka, va, *sums)
    return out[0], out[1], out[2], list(out[3:])


def _qk_bwd(name, proj, dqa, dka, dv, dz, q_gain, k_gain, H):
    S = proj.shape[0]
    HW = H * HEAD
    tr = _tile(S, 256, unit=8)
    scale = HEAD ** -0.5
    factors = (scale, 1.0 / LOG2E)

    def body(q_ref, k_ref, dq_ref, dk_ref, dv_ref, dz_ref, gq_ref, gk_ref, dp_ref, dgq_ref, dgk_ref, dc_ref):
        i = pl.program_id(0)

        @pl.when(i == 0)
        def _():
            dgq_ref[...] = jnp.zeros_like(dgq_ref)
            dgk_ref[...] = jnp.zeros_like(dgk_ref)

        for n, (src, dsrc, gain, dgain) in enumerate(((q_ref, dq_ref, gq_ref, dgq_ref), (k_ref, dk_ref, gk_ref, dgk_ref))):
            acc = jnp.zeros((1, HEAD), F32)
            for h in range(H):
                sl = slice(h * HEAD, (h + 1) * HEAD)
                t = src[:, sl]
                r = lax.rsqrt(jnp.mean(t * t, axis=-1, keepdims=True) + EPS)
                that = t * r
                dn = dsrc[:, h * AUG:h * AUG + HEAD] * factors[n]
                acc = acc + jnp.sum(dn * that, axis=0, keepdims=True)
                dhat = dn * gain[...]
                dt = r * (dhat - that * jnp.mean(dhat * that, axis=-1, keepdims=True))
                dp_ref[:, n * HW + h * HEAD:n * HW + (h + 1) * HEAD] = dt.astype(BF16)
            dgain[...] += acc
        dp_ref[:, 2 * HW:3 * HW] = dv_ref[...].astype(BF16)
        dp_ref[:, 3 * HW:] = dz_ref[...]
        lane = lax.broadcasted_iota(jnp.int32, (tr, LANES), 1)
        dc = jnp.zeros((tr, LANES), F32)
        for h in range(H):
            qs = dq_ref[:, h * AUG + K_SUM_LANE:h * AUG + K_SUM_LANE + 1]
            ks = dk_ref[:, h * AUG + Q_SUM_LANE:h * AUG + Q_SUM_LANE + 1]
            dc = jnp.where(lane == h, qs - ks, dc)
        dc_ref[...] = dc

    col = lambda c: pl.BlockSpec((tr, HW), lambda i: (i, c))
    row = col(0)
    aug = pl.BlockSpec((tr, H * AUG), lambda i: (i, 0))
    vec = pl.BlockSpec((1, HEAD), lambda i: (0, 0))
    return pl.pallas_call(
        body, name=name, grid=(S // tr,),
        in_specs=[col(0), col(1), aug, aug, row, row, vec, vec],
        out_specs=[pl.BlockSpec((tr, 4 * HW), lambda i: (i, 0)), vec, vec, pl.BlockSpec((tr, LANES), lambda i: (i, 0))],
        out_shape=[jax.ShapeDtypeStruct((S, 4 * HW), BF16), jax.ShapeDtypeStruct((1, HEAD), F32),
                   jax.ShapeDtypeStruct((1, HEAD), F32), jax.ShapeDtypeStruct((S, LANES), F32)],
        compiler_params=_params("arbitrary"),
    )(proj, proj, dqa, dka, dv, dz, q_gain.reshape(1, HEAD), k_gain.reshape(1, HEAD))


def _row_tile(R, C, budget_bytes=1 << 20):
    cap = max(8, budget_bytes // (4 * C))
    t = (min(cap, R) // 8) * 8
    while t >= 8:
        if R % t == 0:
            return t
        t -= 8
    return R


def _add_if(name, a, b, active):
    R, C = a.shape
    tr = _row_tile(R, C)

    def body(act_ref, a_ref, b_ref, o_ref):
        @pl.when(act_ref[0] != 0)
        def _():
            o_ref[...] = a_ref[...] + b_ref[...]

    blk = pl.BlockSpec((tr, C), lambda i, s: (i * s[0], 0))
    grid_spec = pltpu.PrefetchScalarGridSpec(
        num_scalar_prefetch=1, grid=(R // tr,), in_specs=[blk, blk], out_specs=blk)
    return pl.pallas_call(
        body, name=name, grid_spec=grid_spec, out_shape=jax.ShapeDtypeStruct((R, C), F32),
        compiler_params=_params("arbitrary"),
    )(active, a, b)


def _sum_slots(name, slots, active):
    n, R, C = slots.shape
    tr = _row_tile(R, C, budget_bytes=(1 << 20) // 2)

    def body(act_ref, s_ref, o_ref):
        @pl.when(act_ref[0] != 0)
        def _():
            acc = s_ref[0]
            for k in range(1, n):
                acc = acc + s_ref[k]
            o_ref[...] = acc

    grid_spec = pltpu.PrefetchScalarGridSpec(
        num_scalar_prefetch=1, grid=(R // tr,),
        in_specs=[pl.BlockSpec((n, tr, C), lambda i, s: (0, i * s[0], 0))],
        out_specs=pl.BlockSpec((tr, C), lambda i, s: (i * s[0], 0)))
    return pl.pallas_call(
        body, name=name, grid_spec=grid_spec, out_shape=jax.ShapeDtypeStruct((R, C), F32),
        compiler_params=_params("arbitrary"),
    )(active, slots)


def _adamw(name, w, g, m, v):
    R, C = w.shape
    tr = _row_tile(R, C, budget_bytes=(1 << 20) // 2)
    c1 = 1.0 - ADAM_B1 ** ADAM_STEP
    c2 = 1.0 - ADAM_B2 ** ADAM_STEP

    def body(w_ref, g_ref, m_ref, v_ref, d_ref, nm_ref, nv_ref):
        gv = g_ref[...]
        nm = ADAM_B1 * m_ref[...] + (1.0 - ADAM_B1) * gv
        nv = ADAM_B2 * v_ref[...] + (1.0 - ADAM_B2) * (gv * gv)
        m_hat = nm / c1
        v_hat = nv / c2
        d_ref[...] = -ADAM_LR * (m_hat / (jnp.sqrt(v_hat) + ADAM_EPS) + ADAM_WD * w_ref[...])
        nm_ref[...] = nm
        nv_ref[...] = nv

    blk = pl.BlockSpec((tr, C), lambda i: (i, 0))
    return pl.pallas_call(
        body, name=name, grid=(R // tr,), in_specs=[blk] * 4, out_specs=[blk] * 3,
        out_shape=[jax.ShapeDtypeStruct((R, C), F32)] * 3,
        compiler_params=_params("arbitrary"),
    )(w, g, m, v)


def _place():
    x, y, c = lax.axis_index("x"), lax.axis_index("y"), lax.axis_index("c")
    chips = [(1 - x, y), (x, 1 - y), (1 - x, 1 - y)]
    return x, y, c, chips


def _gather_weights(shards):
    nt = len(shards)

    def body(*refs):
        ins, outs = refs[:nt], refs[nt:2 * nt]
        s_send, s_recv, f_send, f_recv = refs[2 * nt:]
        x, y, c, chips = _place()
        me = 2 * x + y
        sibling = (x, y, 1 - c)
        ids = [2 * cx + cy for cx, cy in chips]

        def over_ici(t, k, block):
            return pltpu.make_async_remote_copy(
                src_ref=ins[t].at[c], dst_ref=outs[t].at[c, block], send_sem=s_send.at[3 * t + k],
                recv_sem=s_recv.at[3 * t + k], device_id=(*chips[k], c), device_id_type=MESH)

        def over_d2d(t, k, layer):
            blk = outs[t].at[layer, ids[k]]
            return pltpu.make_async_remote_copy(
                src_ref=blk, dst_ref=blk, send_sem=f_send.at[3 * t + k], recv_sem=f_recv.at[3 * t + k],
                device_id=sibling, device_id_type=MESH)

        pairs = [(t, k) for t in range(nt) for k in range(3)]
        for t, k in pairs:
            over_ici(t, k, me).start()
        for t, k in pairs:
            over_ici(t, k, ids[k]).wait_recv()
            over_d2d(t, k, c).start()
        for t, k in pairs:
            over_d2d(t, k, 1 - c).wait_recv()
        for t, k in pairs:
            over_ici(t, k, me).wait_send()
            over_d2d(t, k, c).wait_send()

    return pl.pallas_call(
        body, name="gather_weights",
        out_shape=[jax.ShapeDtypeStruct((2, N_CHIPS) + s.shape[1:], s.dtype) for s in shards],
        in_specs=[HBM_SPEC] * nt, out_specs=[HBM_SPEC] * nt,
        scratch_shapes=[pltpu.SemaphoreType.DMA((3 * nt,))] * 4,
    )(*shards)


def _send_to_owner(name, grads, owner):
    nt = len(grads)

    def body(*refs):
        ins, outs = refs[:nt], refs[nt:2 * nt]
        s_send, s_recv = refs[2 * nt:]
        x, y, c, _ = _place()

        def copy(t):
            return pltpu.make_async_remote_copy(
                src_ref=ins[t], dst_ref=outs[t], send_sem=s_send.at[t], recv_sem=s_recv.at[t],
                device_id=(x, y, owner), device_id_type=MESH)

        @pl.when(c != owner)
        def _():
            for t in range(nt):
                copy(t).start()
            for t in range(nt):
                copy(t).wait_send()

        @pl.when(c == owner)
        def _():
            for t in range(nt):
                copy(t).wait_recv()

    return pl.pallas_call(
        body, name=name,
        out_shape=[jax.ShapeDtypeStruct(g.shape, F32) for g in grads],
        in_specs=[HBM_SPEC] * nt, out_specs=[HBM_SPEC] * nt,
        scratch_shapes=[pltpu.SemaphoreType.DMA((nt,))] * 2,
    )(*grads)


def _scatter_copies(ins, outs, s_send, s_recv, owner):
    nt = len(ins)
    x, y, c, chips = _place()
    me = 2 * x + y
    ids = [2 * cx + cy for cx, cy in chips]
    pairs = [(t, k) for t in range(nt) for k in range(3)]

    def copy(t, k, slot):
        return pltpu.make_async_remote_copy(
            src_ref=ins[t].at[ids[k]], dst_ref=outs[t].at[slot], send_sem=s_send.at[3 * t + k],
            recv_sem=s_recv.at[3 * t + k], device_id=(*chips[k], owner), device_id_type=MESH)

    def start():
        for t, k in pairs:
            copy(t, k, me).start()

    def wait():
        for t, k in pairs:
            copy(t, k, ids[k]).wait()

    return c == owner, start, wait


def _scatter_chip_sums(name, sums, owner):
    nt = len(sums)

    def body(*refs):
        mine, start, wait = _scatter_copies(refs[:nt], refs[nt:2 * nt], *refs[2 * nt:], owner)

        @pl.when(mine)
        def _():
            start()
            wait()

    return pl.pallas_call(
        body, name=name,
        out_shape=[jax.ShapeDtypeStruct(s.shape, F32) for s in sums],
        in_specs=[HBM_SPEC] * nt, out_specs=[HBM_SPEC] * nt,
        scratch_shapes=[pltpu.SemaphoreType.DMA((3 * nt,))] * 2,
    )(*sums)


def _swap_with_sibling(reduced):
    nt = len(reduced)

    def body(*refs):
        ins, outs = refs[:nt], refs[nt:2 * nt]
        s_send, s_recv = refs[2 * nt:]
        x, y, c, _ = _place()

        def copy(t):
            return pltpu.make_async_remote_copy(
                src_ref=ins[t], dst_ref=outs[t], send_sem=s_send.at[t], recv_sem=s_recv.at[t],
                device_id=(x, y, 1 - c), device_id_type=MESH)

        for t in range(nt):
            copy(t).start()
        for t in range(nt):
            copy(t).wait()

    return pl.pallas_call(
        body, name="swap_with_sibling",
        out_shape=[jax.ShapeDtypeStruct(r.shape, F32) for r in reduced],
        in_specs=[HBM_SPEC] * nt, out_specs=[HBM_SPEC] * nt,
        scratch_shapes=[pltpu.SemaphoreType.DMA((nt,))] * 2,
    )(*reduced)


def _gather_small(buf):
    def body(in_ref, out_ref, s_send, s_recv, l_sem):
        x, y, c, _ = _place()
        flips = [(fx, fy, fc) for fx in (0, 1) for fy in (0, 1) for fc in (0, 1)][1:]

        def peer(f):
            return tuple(1 - a if flip else a for a, flip in zip((x, y, c), f))

        def slot(p):
            return 4 * p[0] + 2 * p[1] + p[2]

        local = pltpu.make_async_copy(in_ref, out_ref.at[slot((x, y, c))], l_sem)
        local.start()

        def copy(k, owner):
            return pltpu.make_async_remote_copy(
                src_ref=in_ref, dst_ref=out_ref.at[slot(owner)], send_sem=s_send.at[k], recv_sem=s_recv.at[k],
                device_id=peer(flips[k]), device_id_type=MESH)

        for k in range(7):
            copy(k, (x, y, c)).start()
        for k in range(7):
            copy(k, peer(flips[k])).wait()
        local.wait()

    return pl.pallas_call(
        body, name="gather_small", out_shape=jax.ShapeDtypeStruct((8,) + buf.shape, F32),
        in_specs=[HBM_SPEC], out_specs=HBM_SPEC,
        scratch_shapes=[pltpu.SemaphoreType.DMA((7,))] * 2 + [pltpu.SemaphoreType.DMA],
    )(buf)


def _rows128(a):
    flat = a.reshape(-1)
    rows = -(-flat.shape[0] // LANES)
    rows8 = -(-rows // 8) * 8
    flat = jnp.pad(flat, (0, rows8 * LANES - flat.shape[0]))
    return flat.reshape(rows8, LANES)


def _pack(parts):
    return jnp.concatenate([_rows128(p) for p in parts], axis=0)


def _unpack(buf, shapes):
    out, r = [], 0
    for shp in shapes:
        n = int(np.prod(shp))
        rows8 = -(-(-(-n // LANES)) // 8) * 8
        out.append(buf[r:r + rows8].reshape(-1)[:n].reshape(shp))
        r += rows8
    return out


def kernel(x, a_norm_g, a_w_in, a_v_norm_g, a_w_s, a_b_s, a_w_out, b_norm_g, b_w_in, b_f_bias, b_q_norm_g, b_k_norm_g, b_w_out, loss_target, m_a_norm_g, m_a_w_in, m_a_v_norm_g, m_a_w_s, m_a_b_s, m_a_w_out, m_b_norm_g, m_b_w_in, m_b_f_bias, m_b_q_norm_g, m_b_k_norm_g, m_b_w_out, v_a_norm_g, v_a_w_in, v_a_v_norm_g, v_a_w_s, v_a_b_s, v_a_w_out, v_b_norm_g, v_b_w_in, v_b_f_bias, v_b_q_norm_g, v_b_k_norm_g, v_b_w_out):
    xs = x[0]
    target = loss_target[0]
    S, D = xs.shape
    n_layers = a_w_in.shape[0]
    assert n_layers == 2
    W = a_v_norm_g.shape[1]
    G = a_w_s.shape[1]
    H = b_f_bias.shape[1]
    HW = H * HEAD
    tq_fwd = _tile(S, 512)
    tq_bwd = _tile(S, 512)
    core = lax.axis_index("c")
    chip = 2 * lax.axis_index("x") + lax.axis_index("y")

    shards = [a_w_in.astype(BF16), a_w_out.astype(BF16), b_w_in.astype(BF16), b_w_out.astype(BF16),
              b_norm_g.reshape(n_layers, 1, -1)]
    w_ain, w_aout, w_bin, w_bout, bn_all = [
        lax.dynamic_update_slice(full, own[:, None], (0, chip, 0, 0))
        for full, own in zip(_gather_weights(shards), shards)]
    b_norm_full = bn_all.reshape(n_layers, D)
    cb = b_w_in.shape[2]
    w_bin_full = jnp.transpose(w_bin, (0, 2, 1, 3)).reshape(n_layers, D, N_CHIPS * cb)
    w_bmain = w_bin_full[:, :, :4 * HW]
    w_bf = jnp.pad(w_bin_full[:, :, 4 * HW:], ((0, 0), (0, 0), (0, LANES - H)))
    causal = jnp.tril(jnp.ones((CHUNK, CHUNK), dtype=bool))
    wc = jnp.where(causal[None, None], a_w_s, 0).astype(BF16)
    wc_t = jnp.swapaxes(wc, 2, 3)
    bs_t = jnp.swapaxes(a_b_s, 1, 2)
    f_bias = jnp.pad(b_f_bias, ((0, 0), (0, LANES - H))).reshape(n_layers, 1, LANES)

    def view_ain(l):
        return _View(w_ain, "col", (l,))

    def view_aout(l):
        return _View(w_aout, "row", (l,))

    def view_bout(l):
        return _View(w_bout, "row", (l,))

    saved = []
    cur = xs
    for i in range(2 * n_layers):
        l = i // 2
        if i % 2 == 0:
            h, h_t = _rmsnorm_fwd(f"a{l}_norm", cur, a_norm_g[l])
            p = _matmul(f"a{l}_in", _View(h), view_ain(l), tm=1024, tn=1024, tk=2048)
            y = _gate_fwd(f"a{l}_gate", p, a_v_norm_g[l], wc[l], bs_t[l])
            nxt = _matmul(f"a{l}_out", _View(y), view_aout(l), tm=1024, tn=1024, tk=1024, residual=cur)
            saved.append((cur, h_t, p, y))
        else:
            h, h_t = _rmsnorm_fwd(f"b{l}_norm", cur, b_norm_full[l])
            proj = _matmul(f"b{l}_in", _View(h), _View(w_bmain[l]), tm=1024, tn=1024, tk=2048)
            f = _matmul(f"b{l}_inf", _View(h), _View(w_bf[l]), tm=1024, tn=LANES, tk=2048)
            cum = _fox_cum(f"b{l}_cum", f, f_bias[l])
            qa, ka, va = _qkv_prep(f"b{l}_qkv", proj, cum, b_q_norm_g[l], b_k_norm_g[l], H)
            o, y, lse = _attn_fwd(f"b{l}_attn", qa, ka, va, proj, H, tq_fwd)
            nxt = _matmul(f"b{l}_out", _View(y), view_bout(l), tm=1024, tn=1024, tk=1024, residual=cur)
            saved.append((cur, h_t, proj, f, qa, ka, va, o, y, lse))
        cur = nxt

    g, gb, lcols = _loss_grad(cur, target)
    loss = lax.psum(0.5 * jnp.sum(lcols) / D, ("x", "y", "c"))

    big = {"a_w_in": [None] * n_layers, "a_w_out": [None] * n_layers,
           "b_w_in": [None] * n_layers, "b_w_out": [None] * n_layers}
    small = {k: [None] * n_layers for k in
             ("a_norm_g", "a_v_norm_g", "a_w_s", "a_b_s", "b_norm_g", "b_f_bias", "b_q_norm_g", "b_k_norm_g")}
    names = ["a_w_in", "a_w_out", "b_w_in", "b_w_out"]
    reduced = [None] * n_layers

    def chip_sums_of(layer):
        mine = [big[n][layer] for n in names]
        got = _send_to_owner(f"to_owner{layer}", mine, layer)
        active = (core == layer).astype(jnp.int32).reshape(1)
        sums = []
        for n, a, b in zip(names, mine, got):
            shp = a.shape
            flat = lambda t: t.reshape(shp[0] * shp[1], shp[2])
            sums.append(_add_if(f"chipsum{layer}_{n}", flat(a), flat(b), active).reshape(shp))
        return sums, active

    def reduce_slots(layer, sums, got, active):
        slots = [lax.dynamic_update_slice(g_, lax.dynamic_index_in_dim(s_, chip, keepdims=True), (chip, 0, 0))
                 for g_, s_ in zip(got, sums)]
        return [_sum_slots(f"reduce{layer}_{n}", s_, active) for n, s_ in zip(names, slots)]

    for i in reversed(range(2 * n_layers)):
        l = i // 2
        if i % 2 == 0:
            x_in, h_t, p, y = saved[i]
            dy = _matmul(f"a{l}_dy", _View(gb), view_aout(l), tb=True, out_dtype=BF16, tm=1024, tn=1024, tk=2048)
            d_wout = _matmul(f"a{l}_dwout", _View(y), _View(gb), ta=True, tm=2048, tn=1024, tk=512)
            dp, d_ws, d_bs, d_gv = _gate_bwd(f"a{l}_dgate", p, dy, a_v_norm_g[l], wc[l], wc_t[l], bs_t[l])
            dh = _matmul(f"a{l}_dh", _View(dp), view_ain(l), tb=True, tm=1024, tn=1024, tk=1024)
            d_win = _matmul(f"a{l}_dwin", _View(h_t), _View(dp), tm=2048, tn=1024, tk=512, out_colblocks=N_CHIPS)
            g, gb, d_gn = _rmsnorm_bwd(f"a{l}_dnorm", x_in, a_norm_g[l], dh, g)
            big["a_w_in"][l] = d_win
            big["a_w_out"][l] = d_wout.reshape(N_CHIPS, W // N_CHIPS, D)
            small["a_norm_g"][l] = d_gn.reshape(D)
            small["a_v_norm_g"][l] = d_gv.reshape(W)
            small["a_w_s"][l] = jnp.where(causal[None], d_ws, 0.0)
            small["a_b_s"][l] = d_bs[:, :G].T
        else:
            x_in, h_t, proj, f, qa, ka, va, o, y, lse = saved[i]
            dy = _matmul(f"b{l}_dy", _View(gb), view_bout(l), tb=True, out_dtype=BF16, tm=1024, tn=1024, tk=2048)
            d_wout = _matmul(f"b{l}_dwout", _View(y), _View(gb), ta=True, tm=2048, tn=1024, tk=512)
            lse_lanes = jnp.pad(lse.reshape(H, S).T, ((0, 0), (0, LANES - H)))
            doa, dz, qab = _attn_bwd_prep(f"b{l}_dprep", dy, o, proj, qa, lse_lanes, H)
            early = (i == 1)
            if early:
                sums1, active1 = chip_sums_of(1)
            dqa, dka, dv, got1 = _attn_bwd(f"b{l}_dattn", qab, doa, ka, va, H, tq_bwd,
                                           carried=(sums1, 1) if early else None)
            if early:
                reduced[1] = reduce_slots(1, sums1, got1, active1)
            dproj, d_gq, d_gk, dcum = _qk_bwd(f"b{l}_dqk", proj, dqa, dka, dv, dz, b_q_norm_g[l], b_k_norm_g[l], H)
            df, d_fb = _fox_cum_bwd(f"b{l}_dcum", dcum, f, f_bias[l])
            dh_f = _matmul(f"b{l}_dhf", _View(df), _View(w_bf[l]), tb=True, tm=1024, tn=1024, tk=LANES)
            dh = _matmul(f"b{l}_dh", _View(dproj), _View(w_bmain[l]), tb=True, tm=1024, tn=1024, tk=1024,
                         residual=dh_f)
            d_wmain = _matmul(f"b{l}_dwin", _View(h_t), _View(dproj), tm=2048, tn=1024, tk=512)
            d_wf = _matmul(f"b{l}_dwinf", _View(h_t), _View(df), tm=2048, tn=LANES, tk=512)
            d_win = jnp.concatenate([d_wmain, d_wf[:, :H]], axis=1)
            g, gb, d_gn = _rmsnorm_bwd(f"b{l}_dnorm", x_in, b_norm_full[l], dh, g)
            big["b_w_in"][l] = jnp.transpose(d_win.reshape(D, N_CHIPS, cb), (1, 0, 2))
            big["b_w_out"][l] = d_wout.reshape(N_CHIPS, HW // N_CHIPS, D)
            small["b_norm_g"][l] = d_gn.reshape(D)
            small["b_f_bias"][l] = d_fb[0, :H]
            small["b_q_norm_g"][l] = d_gq.reshape(HEAD)
            small["b_k_norm_g"][l] = d_gk.reshape(HEAD)
    grad_x = g[None]

    sums0, active0 = chip_sums_of(0)
    reduced[0] = reduce_slots(0, sums0, _scatter_chip_sums("scatter0", sums0, 0), active0)
    mine = [jnp.where(core == 0, r0, r1) for r0, r1 in zip(*reduced)]
    others = _swap_with_sibling(mine)
    grads = {n: jnp.where(core == 0, jnp.stack([m_, o_]), jnp.stack([o_, m_]))
             for n, m_, o_ in zip(names, mine, others)}

    small_names = ["a_norm_g", "a_v_norm_g", "a_w_s", "a_b_s", "b_norm_g", "b_f_bias", "b_q_norm_g", "b_k_norm_g"]
    small_parts = [jnp.stack(small[n]) for n in small_names]
    small_sum = _sum_slots("reduce_small", _gather_small(_pack(small_parts)), jnp.ones((1,), jnp.int32))
    for n, a in zip(small_names, _unpack(small_sum, [p.shape for p in small_parts])):
        grads[n] = a
    nb = b_norm_g.shape[1]
    grads["b_norm_g"] = lax.dynamic_slice_in_dim(grads["b_norm_g"], chip * nb, nb, axis=1)

    weights = dict(a_norm_g=a_norm_g, a_w_in=a_w_in, a_v_norm_g=a_v_norm_g, a_w_s=a_w_s, a_b_s=a_b_s,
                   a_w_out=a_w_out, b_norm_g=b_norm_g, b_w_in=b_w_in, b_f_bias=b_f_bias,
                   b_q_norm_g=b_q_norm_g, b_k_norm_g=b_k_norm_g, b_w_out=b_w_out)
    mom1 = dict(a_norm_g=m_a_norm_g, a_w_in=m_a_w_in, a_v_norm_g=m_a_v_norm_g, a_w_s=m_a_w_s, a_b_s=m_a_b_s,
                a_w_out=m_a_w_out, b_norm_g=m_b_norm_g, b_w_in=m_b_w_in, b_f_bias=m_b_f_bias,
                b_q_norm_g=m_b_q_norm_g, b_k_norm_g=m_b_k_norm_g, b_w_out=m_b_w_out)
    mom2 = dict(a_norm_g=v_a_norm_g, a_w_in=v_a_w_in, a_v_norm_g=v_a_v_norm_g, a_w_s=v_a_w_s, a_b_s=v_a_b_s,
                a_w_out=v_a_w_out, b_norm_g=v_b_norm_g, b_w_in=v_b_w_in, b_f_bias=v_b_f_bias,
                b_q_norm_g=v_b_q_norm_g, b_k_norm_g=v_b_k_norm_g, b_w_out=v_b_w_out)
    order = ["a_norm_g", "a_w_in", "a_v_norm_g", "a_w_s", "a_b_s", "a_w_out", "b_norm_g", "b_w_in", "b_f_bias",
             "b_q_norm_g", "b_k_norm_g", "b_w_out"]
    delta, new_m, new_v = {}, {}, {}
    for n in names:
        shp = weights[n].shape
        flat = lambda a: a.reshape(shp[0] * shp[1], shp[2])
        d, nm, nv = _adamw(f"adamw_{n}", flat(weights[n]), flat(grads[n]), flat(mom1[n]), flat(mom2[n]))
        delta[n], new_m[n], new_v[n] = d.reshape(shp), nm.reshape(shp), nv.reshape(shp)
    small_shapes = [weights[n].shape for n in small_names]
    pack_w, pack_g, pack_m, pack_v = (_pack([d[n] for n in small_names]) for d in (weights, grads, mom1, mom2))
    d, nm, nv = _adamw("adamw_small", pack_w, pack_g, pack_m, pack_v)
    for dst, buf in ((delta, d), (new_m, nm), (new_v, nv)):
        for n, a in zip(small_names, _unpack(buf, small_shapes)):
            dst[n] = a

    return (loss, grad_x, *[grads[n] for n in order], *[delta[n] for n in order],
            *[new_m[n] for n in order], *[new_v[n] for n in order])
```

```python
import functools
import math

import numpy as np
import jax
import jax.numpy as jnp
from jax import lax
from jax.experimental import pallas as pl
from jax.experimental.pallas import tpu as pltpu

F32 = jnp.float32
BF16 = jnp.bfloat16
MESH = pl.DeviceIdType.MESH

EPS = 1e-6
CHUNK = 128
HEAD = 128
LANES = 128
N_CHIPS = 4
VMEM_LIMIT = 56 * 1024 * 1024

ADAM_LR = 0.001
ADAM_B1 = 0.9
ADAM_B2 = 0.999
ADAM_EPS = 1e-08
ADAM_WD = 0.01
ADAM_STEP = 10

_NT = (((1,), (1,)), ((), ()))
_TN = (((0,), (0,)), ((), ()))
_GELU_C = math.sqrt(2.0 / math.pi)

HBM_SPEC = pl.BlockSpec(memory_space=pltpu.HBM)


def _params(*sem):
    return pltpu.CompilerParams(dimension_semantics=sem, vmem_limit_bytes=VMEM_LIMIT)


def _tile(dim, pref, unit=LANES):
    t = (min(pref, dim) // unit) * unit
    while t >= unit:
        if dim % t == 0:
            return t
        t -= unit
    return dim


def _gelu(x):
    return 0.5 * x * (1.0 + jnp.tanh(_GELU_C * (x + 0.044715 * (x * x * x))))


def _gelu_and_grad(x):
    x2 = x * x
    t = jnp.tanh(_GELU_C * (x + 0.044715 * (x2 * x)))
    val = 0.5 * x * (1.0 + t)
    grad = 0.5 * (1.0 + t) + 0.5 * x * (1.0 - t * t) * (_GELU_C * (1.0 + 3.0 * 0.044715 * x2))
    return val, grad


def _sigmoid(x):
    return 1.0 / (1.0 + jnp.exp(-x))


class _View:
    def __init__(self, arr, kind="2d", lead=()):
        self.arr, self.kind, self.lead = arr, kind, tuple(lead)
        shp = arr.shape[len(self.lead):]
        if kind == "2d":
            self.R, self.C = shp
        elif kind == "col":
            self.nb, self.R, self.cb = shp
            self.C = self.nb * self.cb
        else:
            self.nb, self.rb, self.C = shp
            self.R = self.nb * self.rb

    def fit(self, tr, tc):
        if self.kind == "col":
            tc = _tile(self.cb, tc)
        elif self.kind == "row":
            tr = _tile(self.rb, tr, unit=8)
        return tr, tc

    def spec(self, tr, tc, rc_of_grid):
        lead = self.lead
        sq = (None,) * len(lead)
        if self.kind == "2d":
            return pl.BlockSpec(sq + (tr, tc), lambda *g: lead + tuple(rc_of_grid(*g)))
        if self.kind == "col":
            q = self.cb // tc

            def im(*g):
                r, c = rc_of_grid(*g)
                return lead + (c // q, r, c % q)

            return pl.BlockSpec(sq + (None, tr, tc), im)
        q = self.rb // tr

        def im(*g):
            r, c = rc_of_grid(*g)
            return lead + (r // q, r % q, c)

        return pl.BlockSpec(sq + (None, tr, tc), im)


def _matmul(name, a, b, *, ta=False, tb=False, out_dtype=F32, tm=1024, tn=1024, tk=1024,
            out_colblocks=None, residual=None, gathered=None):
    M, K = (a.C, a.R) if ta else (a.R, a.C)
    N, K2 = (b.R, b.C) if tb else (b.C, b.R)
    assert K == K2, (name, K, K2)
    tm, tn, tk = _tile(M, tm), _tile(N, tn), _tile(K, tk)
    if ta:
        tk, tm = a.fit(tk, tm)
    else:
        tm, tk = a.fit(tm, tk)
    if tb:
        tn, tk2 = b.fit(tn, tk)
    else:
        tk2, tn = b.fit(tk, tn)
    if tk2 != tk:
        tk = min(tk, tk2)
        if ta:
            tk, tm = a.fit(tk, tm)
        else:
            tm, tk = a.fit(tm, tk)
    if out_colblocks:
        tn = _tile(N // out_colblocks, tn)
    assert M % tm == 0 and N % tn == 0 and K % tk == 0, (name, M, N, K, tm, tn, tk)
    nk = K // tk
    assert nk == 1 or out_dtype == F32, name
    dims = (((0 if ta else 1,), (1 if tb else 0,)), ((), ()))
    grid = (M // tm, N // tn, nk)
    g_blocks, g_layer = gathered if gathered else ([], None)
    ng = len(g_blocks)
    n_in = 2 + (residual is not None)

    def body(*refs):
        a_ref, b_ref = refs[:2]
        r_ref = refs[2] if residual is not None else None
        o_ref = refs[n_in + ng]
        k = pl.program_id(2)
        if gathered:
            mine, start, finish = _gather_copies(refs[n_in:n_in + ng], refs[n_in + ng + 1:n_in + 2 * ng + 1],
                                                 refs[n_in + 2 * ng + 1:], g_layer)
            at = lambda step: functools.reduce(
                jnp.logical_and, [pl.program_id(d) == (0 if step == "first" else grid[d] - 1) for d in range(3)])

            @pl.when(jnp.logical_and(mine, at("first")))
            def _():
                start()

        def product():
            return lax.dot_general(a_ref[...], b_ref[...], dims, preferred_element_type=F32)

        if nk == 1:
            total = product()
            if r_ref is not None:
                total = total + r_ref[...]
            o_ref[...] = total.astype(out_dtype)
        else:
            @pl.when(k == 0)
            def _():
                o_ref[...] = product() + r_ref[...] if r_ref is not None else product()

            @pl.when(k > 0)
            def _():
                o_ref[...] += product()

        if gathered:
            @pl.when(at("last"))
            def _():
                finish()

    a_spec = a.spec(tk, tm, lambda i, j, k: (k, i)) if ta else a.spec(tm, tk, lambda i, j, k: (i, k))
    b_spec = b.spec(tn, tk, lambda i, j, k: (j, k)) if tb else b.spec(tk, tn, lambda i, j, k: (k, j))
    in_specs, args = [a_spec, b_spec], [a.arr, b.arr]
    if residual is not None:
        in_specs.append(pl.BlockSpec((tm, tn), lambda i, j, k: (i, j)))
        args.append(residual)
    in_specs += [HBM_SPEC] * ng
    args += list(g_blocks)
    if out_colblocks:
        q = (N // out_colblocks) // tn
        out_shape = jax.ShapeDtypeStruct((out_colblocks, M, N // out_colblocks), out_dtype)
        out_spec = pl.BlockSpec((None, tm, tn), lambda i, j, k: (j // q, i, j % q))
    else:
        out_shape = jax.ShapeDtypeStruct((M, N), out_dtype)
        out_spec = pl.BlockSpec((tm, tn), lambda i, j, k: (i, j))
    out = pl.pallas_call(
        body, name=name, grid=grid, in_specs=in_specs, out_specs=[out_spec] + [HBM_SPEC] * ng,
        out_shape=[out_shape] + _gather_out_shapes(g_blocks),
        scratch_shapes=_gather_sems(ng) if gathered else [],
        compiler_params=_params("arbitrary", "arbitrary", "arbitrary"),
    )(*args)
    return (out[0], list(out[1:])) if gathered else out[0]


def _rmsnorm_fwd(name, x, gain):
    S, D = x.shape
    tr = _tile(S, 512)

    def body(x_ref, g_ref, h_ref, ht_ref):
        xv = x_ref[...]
        r = lax.rsqrt(jnp.mean(xv * xv, axis=-1, keepdims=True) + EPS)
        h = xv * r * g_ref[...]
        h_ref[...] = h.astype(BF16)
        ht_ref[...] = h.T.astype(BF16)

    return pl.pallas_call(
        body, name=name, grid=(S // tr,),
        in_specs=[pl.BlockSpec((tr, D), lambda i: (i, 0)), pl.BlockSpec((1, D), lambda i: (0, 0))],
        out_specs=[pl.BlockSpec((tr, D), lambda i: (i, 0)), pl.BlockSpec((D, tr), lambda i: (0, i))],
        out_shape=[jax.ShapeDtypeStruct((S, D), BF16), jax.ShapeDtypeStruct((D, S), BF16)],
        compiler_params=_params("arbitrary"),
    )(x, gain.reshape(1, D))


def _rmsnorm_bwd(name, x, gain, dh, g_res):
    S, D = x.shape
    tr = _tile(S, 256, unit=8)

    def body(x_ref, g_ref, dh_ref, res_ref, dx_ref, dxb_ref, dg_ref):
        i = pl.program_id(0)
        xv = x_ref[...]
        r = lax.rsqrt(jnp.mean(xv * xv, axis=-1, keepdims=True) + EPS)
        xhat = xv * r
        dhv = dh_ref[...]
        part = jnp.sum(dhv * xhat, axis=0, keepdims=True)

        @pl.when(i == 0)
        def _():
            dg_ref[...] = part

        @pl.when(i > 0)
        def _():
            dg_ref[...] += part

        dxhat = dhv * g_ref[...]
        dx = res_ref[...] + r * (dxhat - xhat * jnp.mean(dxhat * xhat, axis=-1, keepdims=True))
        dx_ref[...] = dx
        dxb_ref[...] = dx.astype(BF16)

    row = pl.BlockSpec((tr, D), lambda i: (i, 0))
    vec = pl.BlockSpec((1, D), lambda i: (0, 0))
    return pl.pallas_call(
        body, name=name, grid=(S // tr,), in_specs=[row, vec, row, row], out_specs=[row, row, vec],
        out_shape=[jax.ShapeDtypeStruct((S, D), F32), jax.ShapeDtypeStruct((S, D), BF16),
                   jax.ShapeDtypeStruct((1, D), F32)],
        compiler_params=_params("arbitrary"),
    )(x, gain.reshape(1, D), dh, g_res)


def _loss_grad(x, target):
    S, D = x.shape
    tr = _tile(S, 512, unit=8)

    def body(x_ref, t_ref, g_ref, gb_ref, l_ref):
        i = pl.program_id(0)
        e = x_ref[...] - t_ref[...]
        g = e * (1.0 / D)
        g_ref[...] = g
        gb_ref[...] = g.astype(BF16)
        part = jnp.sum(e * e, axis=0, keepdims=True)

        @pl.when(i == 0)
        def _():
            l_ref[...] = part

        @pl.when(i > 0)
        def _():
            l_ref[...] += part

    row = pl.BlockSpec((tr, D), lambda i: (i, 0))
    vec = pl.BlockSpec((1, D), lambda i: (0, 0))
    return pl.pallas_call(
        body, name="loss_grad", grid=(S // tr,), in_specs=[row, row], out_specs=[row, row, vec],
        out_shape=[jax.ShapeDtypeStruct((S, D), F32), jax.ShapeDtypeStruct((S, D), BF16),
                   jax.ShapeDtypeStruct((1, D), F32)],
        compiler_params=_params("arbitrary"),
    )(x, target)


def _gate_fwd(name, p, v_gain, wc, bs_t):
    S, W3 = p.shape
    W = W3 // 3
    G = wc.shape[0]
    gd = W // G

    def body(p_ref, gv_ref, wc_ref, bs_ref, y_ref):
        vg = _gelu(p_ref[:, W:2 * W])
        r = lax.rsqrt(jnp.mean(vg * vg, axis=-1, keepdims=True) + EPS)
        vb = (vg * r * gv_ref[...]).astype(BF16)
        zp = p_ref[:, 2 * W:]
        gate = _gelu(p_ref[:, :W]) * (zp * _sigmoid(zp))
        for g in range(G):
            sl = slice(g * gd, (g + 1) * gd)
            mixed = jnp.dot(wc_ref[g], vb[:, sl], preferred_element_type=F32) + bs_ref[:, g:g + 1]
            y_ref[:, sl] = (gate[:, sl] * mixed).astype(BF16)

    return pl.pallas_call(
        body, name=name, grid=(S // CHUNK,),
        in_specs=[pl.BlockSpec((CHUNK, W3), lambda i: (i, 0)), pl.BlockSpec((1, W), lambda i: (0, 0)),
                  pl.BlockSpec((G, CHUNK, CHUNK), lambda i: (0, 0, 0)), pl.BlockSpec((CHUNK, G), lambda i: (0, 0))],
        out_specs=pl.BlockSpec((CHUNK, W), lambda i: (i, 0)),
        out_shape=jax.ShapeDtypeStruct((S, W), BF16),
        compiler_params=_params("arbitrary"),
    )(p, v_gain.reshape(1, W), wc, bs_t)


def _gate_bwd(name, p, dy, v_gain, wc, wc_t, bs_t):
    S, W3 = p.shape
    W = W3 // 3
    G = wc.shape[0]
    gd = W // G

    def body(p_ref, dy_ref, gv_ref, wc_ref, wct_ref, bs_ref, dp_ref, dws_ref, dbs_ref, dgv_ref, dv_scr):
        i = pl.program_id(0)

        @pl.when(i == 0)
        def _():
            dws_ref[...] = jnp.zeros_like(dws_ref)
            dbs_ref[...] = jnp.zeros_like(dbs_ref)
            dgv_ref[...] = jnp.zeros_like(dgv_ref)

        gu, dgu = _gelu_and_grad(p_ref[:, :W])
        vg, dvg_dv = _gelu_and_grad(p_ref[:, W:2 * W])
        zp = p_ref[:, 2 * W:]
        sig = _sigmoid(zp)
        sz = zp * sig
        dsz = sig * (1.0 + zp * (1.0 - sig))
        r = lax.rsqrt(jnp.mean(vg * vg, axis=-1, keepdims=True) + EPS)
        vhat = vg * r
        gv = gv_ref[...]
        vb = (vhat * gv).astype(BF16)
        dy = dy_ref[...].astype(F32)
        lane = lax.broadcasted_iota(jnp.int32, (CHUNK, LANES), 1)
        dbs = jnp.zeros((CHUNK, LANES), F32)
        for g in range(G):
            sl = slice(g * gd, (g + 1) * gd)
            vsl = vb[:, sl]
            mixed = jnp.dot(wc_ref[g], vsl, preferred_element_type=F32) + bs_ref[:, g:g + 1]
            dyg, gug, szg = dy[:, sl], gu[:, sl], sz[:, sl]
            dp_ref[:, sl] = (dyg * mixed * szg * dgu[:, sl]).astype(BF16)
            dp_ref[:, 2 * W + g * gd:2 * W + (g + 1) * gd] = (dyg * gug * mixed * dsz[:, sl]).astype(BF16)
            dm = dyg * gug * szg
            dmb = dm.astype(BF16)
            dws_ref[g] += lax.dot_general(dmb, vsl, _NT, preferred_element_type=F32)
            dbs = dbs + jnp.where(lane == g, jnp.sum(dm, axis=1, keepdims=True), 0.0)
            dv_scr[:, sl] = jnp.dot(wct_ref[g], dmb, preferred_element_type=F32)
        dbs_ref[...] += dbs
        dv = dv_scr[...]
        dgv_ref[...] += jnp.sum(dv * vhat, axis=0, keepdims=True)
        dvhat = dv * gv
        dvg = r * (dvhat - vhat * jnp.mean(dvhat * vhat, axis=-1, keepdims=True))
        dp_ref[:, W:2 * W] = (dvg * dvg_dv).astype(BF16)

    return pl.pallas_call(
        body, name=name, grid=(S // CHUNK,),
        in_specs=[pl.BlockSpec((CHUNK, W3), lambda i: (i, 0)), pl.BlockSpec((CHUNK, W), lambda i: (i, 0)),
                  pl.BlockSpec((1, W), lambda i: (0, 0)),
                  pl.BlockSpec((G, CHUNK, CHUNK), lambda i: (0, 0, 0)),
                  pl.BlockSpec((G, CHUNK, CHUNK), lambda i: (0, 0, 0)),
                  pl.BlockSpec((CHUNK, G), lambda i: (0, 0))],
        out_specs=[pl.BlockSpec((CHUNK, W3), lambda i: (i, 0)),
                   pl.BlockSpec((G, CHUNK, CHUNK), lambda i: (0, 0, 0)),
                   pl.BlockSpec((CHUNK, LANES), lambda i: (0, 0)),
                   pl.BlockSpec((1, W), lambda i: (0, 0))],
        out_shape=[jax.ShapeDtypeStruct((S, W3), BF16), jax.ShapeDtypeStruct((G, CHUNK, CHUNK), F32),
                   jax.ShapeDtypeStruct((CHUNK, LANES), F32), jax.ShapeDtypeStruct((1, W), F32)],
        scratch_shapes=[pltpu.VMEM((CHUNK, W), F32)],
        compiler_params=_params("arbitrary"),
    )(p, dy, v_gain.reshape(1, W), wc, wc_t, bs_t)


AUG = 2 * HEAD
LOG2E = 1.0 / math.log(2.0)
Q_SUM_LANE = HEAD + 3
K_SUM_LANE = HEAD


def _pieces(x, sign=1.0):
    hi, mid, lo = _split3(sign * x)
    return hi.astype(F32), mid.astype(F32), lo.astype(F32)


def _lanes(lane, start, vals, rest):
    out = rest
    for n, v in enumerate(vals):
        out = jnp.where(lane == start + n, v, out)
    return out


def _qkv_prep(name, proj, cum, q_gain, k_gain, H):
    S = proj.shape[0]
    HW = H * HEAD
    tr = _tile(S, 256, unit=8)
    sigma = (HEAD ** -0.5) * LOG2E

    def body(q_ref, k_ref, v_ref, c_ref, gq_ref, gk_ref, qa_ref, ka_ref, va_ref):
        lane = lax.broadcasted_iota(jnp.int32, (tr, HEAD), 1)
        zero = jnp.zeros((tr, HEAD), F32)
        v_aug = jnp.where(lane < 3, 1.0, zero).astype(BF16)
        for h in range(H):
            sl = slice(h * HEAD, (h + 1) * HEAD)
            a0 = h * AUG
            t = q_ref[:, sl]
            r = lax.rsqrt(jnp.mean(t * t, axis=-1, keepdims=True) + EPS)
            qa_ref[:, a0:a0 + HEAD] = (t * r * gq_ref[...] * sigma).astype(BF16)
            t = k_ref[:, sl]
            r = lax.rsqrt(jnp.mean(t * t, axis=-1, keepdims=True) + EPS)
            ka_ref[:, a0:a0 + HEAD] = (t * r * gk_ref[...]).astype(BF16)
            va_ref[:, a0:a0 + HEAD] = v_ref[:, sl].astype(BF16)
            va_ref[:, a0 + HEAD:a0 + AUG] = v_aug
            c2 = c_ref[:, h:h + 1] * LOG2E
            qa_ref[:, a0 + HEAD:a0 + AUG] = _lanes(lane, 0, _pieces(c2) + (1.0, 1.0, 1.0), zero).astype(BF16)
            ka_ref[:, a0 + HEAD:a0 + AUG] = _lanes(
                lane, 0, (1.0, 1.0, 1.0) + _pieces(c2, -1.0) + (1.0, 1.0, 1.0), zero).astype(BF16)

    col = lambda c: pl.BlockSpec((tr, HW), lambda i: (i, c))
    vec = pl.BlockSpec((1, HEAD), lambda i: (0, 0))
    aug = pl.BlockSpec((tr, H * AUG), lambda i: (i, 0))
    return pl.pallas_call(
        body, name=name, grid=(S // tr,),
        in_specs=[col(0), col(1), col(2), pl.BlockSpec((tr, LANES), lambda i: (i, 0)), vec, vec],
        out_specs=[aug] * 3, out_shape=[jax.ShapeDtypeStruct((S, H * AUG), BF16)] * 3,
        compiler_params=_params("arbitrary"),
    )(proj, proj, proj, cum, q_gain.reshape(1, HEAD), k_gain.reshape(1, HEAD))


def _split3(x):
    hi = x.astype(BF16)
    r1 = x - hi.astype(F32)
    mid = r1.astype(BF16)
    lo = (r1 - mid.astype(F32)).astype(BF16)
    return hi, mid, lo


def _tri_sum(tri, x):
    hi, mid, lo = _split3(x)
    d = lambda t: jnp.dot(tri, t, preferred_element_type=F32)
    return d(hi) + (d(mid) + d(lo))


def _log_sigmoid(x):
    return jnp.minimum(x, 0.0) - jnp.log(1.0 + jnp.exp(-jnp.abs(x)))


def _fox_cum(name, f, bias):
    S = f.shape[0]
    tb = _tile(S, 256, unit=8)

    def body(f_ref, b_ref, c_ref):
        rr = lax.broadcasted_iota(jnp.int32, (tb, tb), 0)
        cc = lax.broadcasted_iota(jnp.int32, (tb, tb), 1)
        tri = (rr >= cc).astype(BF16)

        def step(t, carry):
            off = pl.multiple_of(t * tb, tb)
            lf = _log_sigmoid(f_ref[pl.ds(off, tb), :] + b_ref[...])
            c = _tri_sum(tri, lf) + carry
            c_ref[pl.ds(off, tb), :] = c
            return c[tb - 1:tb, :]

        lax.fori_loop(0, S // tb, step, jnp.zeros((1, LANES), F32))

    return pl.pallas_call(
        body, name=name, out_shape=jax.ShapeDtypeStruct((S, LANES), F32),
        in_specs=[pl.BlockSpec(memory_space=pltpu.VMEM)] * 2, out_specs=pl.BlockSpec(memory_space=pltpu.VMEM),
        compiler_params=pltpu.CompilerParams(vmem_limit_bytes=VMEM_LIMIT),
    )(f, bias)


def _fox_cum_bwd(name, dcum, f, bias):
    S = f.shape[0]
    tb = _tile(S, 256, unit=8)
    nb = S // tb

    def body(dc_ref, f_ref, b_ref, df_ref, db_ref):
        rr = lax.broadcasted_iota(jnp.int32, (tb, tb), 0)
        cc = lax.broadcasted_iota(jnp.int32, (tb, tb), 1)
        tri = (rr <= cc).astype(BF16)

        def step(t, carry):
            tail, dbias = carry
            off = pl.multiple_of((nb - 1 - t) * tb, tb)
            dlf = _tri_sum(tri, dc_ref[pl.ds(off, tb), :]) + tail
            d = dlf * _sigmoid(-(f_ref[pl.ds(off, tb), :] + b_ref[...]))
            df_ref[pl.ds(off, tb), :] = d.astype(BF16)
            return dlf[0:1, :], dbias + jnp.sum(d, axis=0, keepdims=True)

        z = jnp.zeros((1, LANES), F32)
        _, dbias = lax.fori_loop(0, nb, step, (z, z))
        db_ref[...] = dbias

    vm = pl.BlockSpec(memory_space=pltpu.VMEM)
    return pl.pallas_call(
        body, name=name, out_shape=[jax.ShapeDtypeStruct((S, LANES), BF16), jax.ShapeDtypeStruct((1, LANES), F32)],
        in_specs=[vm] * 3, out_specs=[vm] * 2,
        compiler_params=pltpu.CompilerParams(vmem_limit_bytes=VMEM_LIMIT),
    )(dcum, f, bias)


def _attn_fwd(name, qa, ka, va, proj, H, tq, gathered=None):
    S = qa.shape[0]
    HW = H * HEAD
    nq = S // tq
    hp = 2 if H % 2 == 0 else 1
    g_blocks, g_layer = gathered if gathered else ([], None)
    ng = len(g_blocks)

    def body(*refs):
        q_ref, k_ref, v_ref, z_ref = refs[:4]
        o_ref, y_ref, lse_ref = refs[4 + ng:7 + ng]
        i = pl.program_id(1)
        if gathered:
            mine, start, finish = _gather_copies(refs[4:4 + ng], refs[7 + ng:7 + 2 * ng], refs[7 + 2 * ng:], g_layer)
            hd = pl.program_id(0)

            @pl.when(jnp.logical_and(mine, jnp.logical_and(hd == 0, i == 0)))
            def _():
                start()

        def step(j, carry, masked):
            off = pl.multiple_of(j * tq, tq)
            out = []
            for n in range(hp):
                m, acc = carry[n]
                a = slice(n * AUG, (n + 1) * AUG)
                s = lax.dot_general(q_ref[:, a], k_ref[pl.ds(off, tq), a], _NT, preferred_element_type=F32)
                if masked:
                    rr = lax.broadcasted_iota(jnp.int32, (tq, tq), 0)
                    cc = lax.broadcasted_iota(jnp.int32, (tq, tq), 1)
                    s = jnp.where(rr >= cc, s, -jnp.inf)
                m_new = jnp.maximum(m, jnp.max(s, axis=1, keepdims=True))
                pr = jnp.exp2(s - m_new).astype(BF16)
                acc = jnp.exp2(m - m_new) * acc + jnp.dot(pr, v_ref[pl.ds(off, tq), a], preferred_element_type=F32)
                out.append((m_new, acc))
            return tuple(out)

        init = ((jnp.full((tq, 1), -jnp.inf, F32), jnp.zeros((tq, AUG), F32)),) * hp
        carry = lax.fori_loop(0, i, lambda j, c: step(j, c, False), init)
        carry = step(i, carry, True)
        for n in range(hp):
            m, acc = carry[n]
            sl = slice(n * HEAD, (n + 1) * HEAD)
            l = acc[:, HEAD:HEAD + 1]
            o = acc[:, :HEAD] / l
            z = z_ref[:, sl]
            o_ref[:, sl] = o
            y_ref[:, sl] = (o * (z * _sigmoid(z))).astype(BF16)
            lse_ref[n] = m + jnp.log(l) * LOG2E

        if gathered:
            @pl.when(jnp.logical_and(hd == H // hp - 1, i == nq - 1))
            def _():
                finish()

    qspec = pl.BlockSpec((tq, hp * AUG), lambda h, i: (i, h))
    kvspec = pl.BlockSpec((S, hp * AUG), lambda h, i: (0, h))
    ospec = pl.BlockSpec((tq, hp * HEAD), lambda h, i: (i, h))
    out = pl.pallas_call(
        body, name=name, grid=(H // hp, nq),
        in_specs=[qspec, kvspec, kvspec, pl.BlockSpec((tq, hp * HEAD), lambda h, i: (i, 3 * H // hp + h))]
        + [HBM_SPEC] * ng,
        out_specs=[ospec, ospec, pl.BlockSpec((hp, tq, 1), lambda h, i: (h, i, 0))] + [HBM_SPEC] * ng,
        out_shape=[jax.ShapeDtypeStruct((S, HW), F32), jax.ShapeDtypeStruct((S, HW), BF16),
                   jax.ShapeDtypeStruct((H, S, 1), F32)] + _gather_out_shapes(g_blocks),
        scratch_shapes=_gather_sems(ng) if gathered else [],
        compiler_params=_params("arbitrary", "arbitrary"),
    )(qa, ka, va, proj, *g_blocks)
    return out[0], out[1], out[2], list(out[3:])


def _attn_bwd_prep(name, dy, o, proj, qa, lse, H):
    S, HW = o.shape
    tr = _tile(S, 256, unit=8)

    def body(dy_ref, o_ref, z_ref, qa_ref, lse_ref, doa_ref, dz_ref, qab_ref):
        lane = lax.broadcasted_iota(jnp.int32, (tr, HEAD), 1)
        zero = jnp.zeros((tr, HEAD), F32)
        for h in range(H):
            sl = slice(h * HEAD, (h + 1) * HEAD)
            a0 = h * AUG
            dy = dy_ref[:, sl].astype(F32)
            z = z_ref[:, sl]
            o = o_ref[:, sl]
            sig = _sigmoid(z)
            dob = (dy * (z * sig)).astype(BF16)
            dz_ref[:, sl] = (dy * o * (sig * (1.0 + z * (1.0 - sig)))).astype(BF16)
            delta = jnp.sum(dob.astype(F32) * o, axis=1, keepdims=True)
            doa_ref[:, a0:a0 + HEAD] = dob
            doa_ref[:, a0 + HEAD:a0 + AUG] = _lanes(lane, 0, _pieces(delta, -1.0), zero).astype(BF16)
            qab_ref[:, a0:a0 + HEAD] = qa_ref[:, a0:a0 + HEAD]
            qab_ref[:, a0 + HEAD:a0 + AUG] = _lanes(
                lane, 6, _pieces(lse_ref[:, h:h + 1], -1.0), qa_ref[:, a0 + HEAD:a0 + AUG].astype(F32)).astype(BF16)

    row = pl.BlockSpec((tr, HW), lambda i: (i, 0))
    aug = pl.BlockSpec((tr, H * AUG), lambda i: (i, 0))
    return pl.pallas_call(
        body, name=name, grid=(S // tr,),
        in_specs=[row, row, pl.BlockSpec((tr, HW), lambda i: (i, 3)), aug, pl.BlockSpec((tr, LANES), lambda i: (i, 0))],
        out_specs=[aug, row, aug],
        out_shape=[jax.ShapeDtypeStruct((S, H * AUG), BF16), jax.ShapeDtypeStruct((S, HW), BF16),
                   jax.ShapeDtypeStruct((S, H * AUG), BF16)],
        compiler_params=_params("arbitrary"),
    )(dy, o, proj, qa, lse)


def _attn_bwd(name, qab, doa, ka, va, H, tq, carried=None):
    S = qab.shape[0]
    nq = S // tq
    sums, owner = carried if carried else ([], None)
    nt = len(sums)

    def body(*refs):
        q_ref, do_ref, k_ref, v_ref = refs[:4]
        dq_ref, dk_ref, dv_ref = refs[4 + nt:7 + nt]
        j = pl.program_id(1)
        if carried:
            mine, start, wait = _scatter_copies(refs[4:4 + nt], refs[7 + nt:7 + 2 * nt], *refs[7 + 2 * nt:], owner)
            hd = pl.program_id(0)

            @pl.when(jnp.logical_and(mine, jnp.logical_and(hd == 0, j == 0)))
            def _():
                start()

        @pl.when(j == 0)
        def _():
            dq_ref[...] = jnp.zeros_like(dq_ref)

        k = k_ref[...]
        v = v_ref[...]

        def step(i, carry, masked):
            dk_acc, dv_acc = carry
            off = pl.multiple_of(i * tq, tq)
            q = q_ref[pl.ds(off, tq), :]
            do = do_ref[pl.ds(off, tq), :]
            st = lax.dot_general(k, q, _NT, preferred_element_type=F32)
            if masked:
                rr = lax.broadcasted_iota(jnp.int32, (tq, tq), 0)
                cc = lax.broadcasted_iota(jnp.int32, (tq, tq), 1)
                st = jnp.where(cc >= rr, st, -jnp.inf)
            pt = jnp.exp2(st)
            dst = pt * lax.dot_general(v, do, _NT, preferred_element_type=F32)
            dsb = dst.astype(BF16)
            dv_acc = dv_acc + jnp.dot(pt.astype(BF16), do[:, :HEAD], preferred_element_type=F32)
            dk_acc = dk_acc + jnp.dot(dsb, q, preferred_element_type=F32)
            dq_ref[pl.ds(off, tq), :] += lax.dot_general(dsb, k, _TN, preferred_element_type=F32)
            return dk_acc, dv_acc

        carry = step(j, (jnp.zeros((tq, AUG), F32), jnp.zeros((tq, HEAD), F32)), True)
        dk_acc, dv_acc = lax.fori_loop(j + 1, nq, lambda i, c: step(i, c, False), carry)
        dk_ref[...] = dk_acc
        dv_ref[...] = dv_acc
        if carried:
            @pl.when(jnp.logical_and(mine, jnp.logical_and(hd == H - 1, j == nq - 1)))
            def _():
                wait()

    full = pl.BlockSpec((S, AUG), lambda h, j: (0, h))
    blk = pl.BlockSpec((tq, AUG), lambda h, j: (j, h))
    out = pl.pallas_call(
        body, name=name, grid=(H, nq),
        in_specs=[full, full, blk, blk] + [HBM_SPEC] * nt,
        out_specs=[full, blk, pl.BlockSpec((tq, HEAD), lambda h, j: (j, h))] + [HBM_SPEC] * nt,
        out_shape=[jax.ShapeDtypeStruct((S, H * AUG), F32), jax.ShapeDtypeStruct((S, H * AUG), F32),
                   jax.ShapeDtypeStruct((S, H * HEAD), F32)] + [jax.ShapeDtypeStruct(s.shape, F32) for s in sums],
        scratch_shapes=[pltpu.SemaphoreType.DMA((3 * nt,))] * 2 if carried else [],
        compiler_params=_params("arbitrary", "arbitrary"),
    )(qab, doa, ka, va, *sums)
    return out[0], out[1], out[2], list(out[3:])


def _qk_bwd(name, proj, dqa, dka, dv, dz, q_gain, k_gain, H):
    S = proj.shape[0]
    HW = H * HEAD
    tr = _tile(S, 256, unit=8)
    scale = HEAD ** -0.5
    factors = (scale, 1.0 / LOG2E)

    def body(q_ref, k_ref, dq_ref, dk_ref, dv_ref, dz_ref, gq_ref, gk_ref, dp_ref, dgq_ref, dgk_ref, dc_ref):
        i = pl.program_id(0)

        @pl.when(i == 0)
        def _():
            dgq_ref[...] = jnp.zeros_like(dgq_ref)
            dgk_ref[...] = jnp.zeros_like(dgk_ref)

        for n, (src, dsrc, gain, dgain) in enumerate(((q_ref, dq_ref, gq_ref, dgq_ref), (k_ref, dk_ref, gk_ref, dgk_ref))):
            acc = jnp.zeros((1, HEAD), F32)
            for h in range(H):
                sl = slice(h * HEAD, (h + 1) * HEAD)
                t = src[:, sl]
                r = lax.rsqrt(jnp.mean(t * t, axis=-1, keepdims=True) + EPS)
                that = t * r
                dn = dsrc[:, h * AUG:h * AUG + HEAD] * factors[n]
                acc = acc + jnp.sum(dn * that, axis=0, keepdims=True)
                dhat = dn * gain[...]
                dt = r * (dhat - that * jnp.mean(dhat * that, axis=-1, keepdims=True))
                dp_ref[:, n * HW + h * HEAD:n * HW + (h + 1) * HEAD] = dt.astype(BF16)
            dgain[...] += acc
        dp_ref[:, 2 * HW:3 * HW] = dv_ref[...].astype(BF16)
        dp_ref[:, 3 * HW:] = dz_ref[...]
        lane = lax.broadcasted_iota(jnp.int32, (tr, LANES), 1)
        dc = jnp.zeros((tr, LANES), F32)
        for h in range(H):
            qs = dq_ref[:, h * AUG + K_SUM_LANE:h * AUG + K_SUM_LANE + 1]
            ks = dk_ref[:, h * AUG + Q_SUM_LANE:h * AUG + Q_SUM_LANE + 1]
            dc = jnp.where(lane == h, qs - ks, dc)
        dc_ref[...] = dc

    col = lambda c: pl.BlockSpec((tr, HW), lambda i: (i, c))
    row = col(0)
    aug = pl.BlockSpec((tr, H * AUG), lambda i: (i, 0))
    vec = pl.BlockSpec((1, HEAD), lambda i: (0, 0))
    return pl.pallas_call(
        body, name=name, grid=(S // tr,),
        in_specs=[col(0), col(1), aug, aug, row, row, vec, vec],
        out_specs=[pl.BlockSpec((tr, 4 * HW), lambda i: (i, 0)), vec, vec, pl.BlockSpec((tr, LANES), lambda i: (i, 0))],
        out_shape=[jax.ShapeDtypeStruct((S, 4 * HW), BF16), jax.ShapeDtypeStruct((1, HEAD), F32),
                   jax.ShapeDtypeStruct((1, HEAD), F32), jax.ShapeDtypeStruct((S, LANES), F32)],
        compiler_params=_params("arbitrary"),
    )(proj, proj, dqa, dka, dv, dz, q_gain.reshape(1, HEAD), k_gain.reshape(1, HEAD))


def _row_tile(R, C, budget_bytes=1 << 20):
    cap = max(8, budget_bytes // (4 * C))
    t = (min(cap, R) // 8) * 8
    while t >= 8:
        if R % t == 0:
            return t
        t -= 8
    return R


def _add_if(name, a, b, active):
    R, C = a.shape
    tr = _row_tile(R, C)

    def body(act_ref, a_ref, b_ref, o_ref):
        @pl.when(act_ref[0] != 0)
        def _():
            o_ref[...] = a_ref[...] + b_ref[...]

    blk = pl.BlockSpec((tr, C), lambda i, s: (i * s[0], 0))
    grid_spec = pltpu.PrefetchScalarGridSpec(
        num_scalar_prefetch=1, grid=(R // tr,), in_specs=[blk, blk], out_specs=blk)
    return pl.pallas_call(
        body, name=name, grid_spec=grid_spec, out_shape=jax.ShapeDtypeStruct((R, C), F32),
        compiler_params=_params("arbitrary"),
    )(active, a, b)


def _sum_slots(name, slots, active):
    n, R, C = slots.shape
    tr = _row_tile(R, C, budget_bytes=(1 << 20) // 2)

    def body(act_ref, s_ref, o_ref):
        @pl.when(act_ref[0] != 0)
        def _():
            acc = s_ref[0]
            for k in range(1, n):
                acc = acc + s_ref[k]
            o_ref[...] = acc

    grid_spec = pltpu.PrefetchScalarGridSpec(
        num_scalar_prefetch=1, grid=(R // tr,),
        in_specs=[pl.BlockSpec((n, tr, C), lambda i, s: (0, i * s[0], 0))],
        out_specs=pl.BlockSpec((tr, C), lambda i, s: (i * s[0], 0)))
    return pl.pallas_call(
        body, name=name, grid_spec=grid_spec, out_shape=jax.ShapeDtypeStruct((R, C), F32),
        compiler_params=_params("arbitrary"),
    )(active, slots)


def _adamw(name, w, g, m, v):
    R, C = w.shape
    tr = _row_tile(R, C, budget_bytes=(1 << 20) // 2)
    c1 = 1.0 - ADAM_B1 ** ADAM_STEP
    c2 = 1.0 - ADAM_B2 ** ADAM_STEP

    def body(w_ref, g_ref, m_ref, v_ref, d_ref, nm_ref, nv_ref):
        gv = g_ref[...]
        nm = ADAM_B1 * m_ref[...] + (1.0 - ADAM_B1) * gv
        nv = ADAM_B2 * v_ref[...] + (1.0 - ADAM_B2) * (gv * gv)
        m_hat = nm / c1
        v_hat = nv / c2
        d_ref[...] = -ADAM_LR * (m_hat / (jnp.sqrt(v_hat) + ADAM_EPS) + ADAM_WD * w_ref[...])
        nm_ref[...] = nm
        nv_ref[...] = nv

    blk = pl.BlockSpec((tr, C), lambda i: (i, 0))
    return pl.pallas_call(
        body, name=name, grid=(R // tr,), in_specs=[blk] * 4, out_specs=[blk] * 3,
        out_shape=[jax.ShapeDtypeStruct((R, C), F32)] * 3,
        compiler_params=_params("arbitrary"),
    )(w, g, m, v)


def _place():
    x, y, c = lax.axis_index("x"), lax.axis_index("y"), lax.axis_index("c")
    chips = [(1 - x, y), (x, 1 - y), (1 - x, 1 - y)]
    return x, y, c, chips


def _gather_copies(ins, outs, sems, layer):
    s_send, s_recv, f_send, f_recv = sems
    nt = len(ins)
    x, y, c, chips = _place()
    me = 2 * x + y
    ids = [2 * cx + cy for cx, cy in chips]
    pairs = [(t, k) for t in range(nt) for k in range(3)]

    def over_ici(t, k, block):
        return pltpu.make_async_remote_copy(
            src_ref=ins[t], dst_ref=outs[t].at[block], send_sem=s_send.at[3 * t + k],
            recv_sem=s_recv.at[3 * t + k], device_id=(*chips[k], layer), device_id_type=MESH)

    def over_d2d(t, k):
        blk = outs[t].at[ids[k]]
        return pltpu.make_async_remote_copy(
            src_ref=blk, dst_ref=blk, send_sem=f_send.at[3 * t + k], recv_sem=f_recv.at[3 * t + k],
            device_id=(x, y, 1 - c), device_id_type=MESH)

    def start():
        for t, k in pairs:
            over_ici(t, k, me).start()

    def finish():
        @pl.when(c == layer)
        def _():
            for t, k in pairs:
                over_ici(t, k, ids[k]).wait_recv()
                over_d2d(t, k).start()
            for t, k in pairs:
                over_ici(t, k, me).wait_send()
                over_d2d(t, k).wait_send()

        @pl.when(c != layer)
        def _():
            for t, k in pairs:
                over_d2d(t, k).wait_recv()

    return c == layer, start, finish


def _gather_out_shapes(blocks):
    return [jax.ShapeDtypeStruct((N_CHIPS,) + b.shape, b.dtype) for b in blocks]


def _gather_sems(n):
    return [pltpu.SemaphoreType.DMA((3 * n,))] * 4


def _gather_weights(name, blocks, layer):
    nt = len(blocks)

    def body(*refs):
        mine, start, finish = _gather_copies(refs[:nt], refs[nt:2 * nt], refs[2 * nt:], layer)

        @pl.when(mine)
        def _():
            start()

        finish()

    return pl.pallas_call(
        body, name=name, out_shape=_gather_out_shapes(blocks),
        in_specs=[HBM_SPEC] * nt, out_specs=[HBM_SPEC] * nt, scratch_shapes=_gather_sems(nt),
    )(*blocks)


def _send_to_owner(name, grads, owner):
    nt = len(grads)

    def body(*refs):
        ins, outs = refs[:nt], refs[nt:2 * nt]
        s_send, s_recv = refs[2 * nt:]
        x, y, c, _ = _place()

        def copy(t):
            return pltpu.make_async_remote_copy(
                src_ref=ins[t], dst_ref=outs[t], send_sem=s_send.at[t], recv_sem=s_recv.at[t],
                device_id=(x, y, owner), device_id_type=MESH)

        @pl.when(c != owner)
        def _():
            for t in range(nt):
                copy(t).start()
            for t in range(nt):
                copy(t).wait_send()

        @pl.when(c == owner)
        def _():
            for t in range(nt):
                copy(t).wait_recv()

    return pl.pallas_call(
        body, name=name,
        out_shape=[jax.ShapeDtypeStruct(g.shape, F32) for g in grads],
        in_specs=[HBM_SPEC] * nt, out_specs=[HBM_SPEC] * nt,
        scratch_shapes=[pltpu.SemaphoreType.DMA((nt,))] * 2,
    )(*grads)


def _scatter_copies(ins, outs, s_send, s_recv, owner):
    nt = len(ins)
    x, y, c, chips = _place()
    me = 2 * x + y
    ids = [2 * cx + cy for cx, cy in chips]
    pairs = [(t, k) for t in range(nt) for k in range(3)]

    def copy(t, k, slot):
        return pltpu.make_async_remote_copy(
            src_ref=ins[t].at[ids[k]], dst_ref=outs[t].at[slot], send_sem=s_send.at[3 * t + k],
            recv_sem=s_recv.at[3 * t + k], device_id=(*chips[k], owner), device_id_type=MESH)

    def start():
        for t, k in pairs:
            copy(t, k, me).start()

    def wait():
        for t, k in pairs:
            copy(t, k, ids[k]).wait()

    return c == owner, start, wait


def _scatter_chip_sums(name, sums, owner):
    nt = len(sums)

    def body(*refs):
        mine, start, wait = _scatter_copies(refs[:nt], refs[nt:2 * nt], *refs[2 * nt:], owner)

        @pl.when(mine)
        def _():
            start()
            wait()

    return pl.pallas_call(
        body, name=name,
        out_shape=[jax.ShapeDtypeStruct(s.shape, F32) for s in sums],
        in_specs=[HBM_SPEC] * nt, out_specs=[HBM_SPEC] * nt,
        scratch_shapes=[pltpu.SemaphoreType.DMA((3 * nt,))] * 2,
    )(*sums)


def _swap_with_sibling(reduced):
    nt = len(reduced)

    def body(*refs):
        ins, outs = refs[:nt], refs[nt:2 * nt]
        s_send, s_recv = refs[2 * nt:]
        x, y, c, _ = _place()

        def copy(t):
            return pltpu.make_async_remote_copy(
                src_ref=ins[t], dst_ref=outs[t], send_sem=s_send.at[t], recv_sem=s_recv.at[t],
                device_id=(x, y, 1 - c), device_id_type=MESH)

        for t in range(nt):
            copy(t).start()
        for t in range(nt):
            copy(t).wait()

    return pl.pallas_call(
        body, name="swap_with_sibling",
        out_shape=[jax.ShapeDtypeStruct(r.shape, F32) for r in reduced],
        in_specs=[HBM_SPEC] * nt, out_specs=[HBM_SPEC] * nt,
        scratch_shapes=[pltpu.SemaphoreType.DMA((nt,))] * 2,
    )(*reduced)


def _gather_small(buf):
    def body(in_ref, out_ref, s_send, s_recv, l_sem):
        x, y, c, _ = _place()
        flips = [(fx, fy, fc) for fx in (0, 1) for fy in (0, 1) for fc in (0, 1)][1:]

        def peer(f):
            return tuple(1 - a if flip else a for a, flip in zip((x, y, c), f))

        def slot(p):
            return 4 * p[0] + 2 * p[1] + p[2]

        local = pltpu.make_async_copy(in_ref, out_ref.at[slot((x, y, c))], l_sem)
        local.start()

        def copy(k, owner):
            return pltpu.make_async_remote_copy(
                src_ref=in_ref, dst_ref=out_ref.at[slot(owner)], send_sem=s_send.at[k], recv_sem=s_recv.at[k],
                device_id=peer(flips[k]), device_id_type=MESH)

        for k in range(7):
            copy(k, (x, y, c)).start()
        for k in range(7):
            copy(k, peer(flips[k])).wait()
        local.wait()

    return pl.pallas_call(
        body, name="gather_small", out_shape=jax.ShapeDtypeStruct((8,) + buf.shape, F32),
        in_specs=[HBM_SPEC], out_specs=HBM_SPEC,
        scratch_shapes=[pltpu.SemaphoreType.DMA((7,))] * 2 + [pltpu.SemaphoreType.DMA],
    )(buf)


def _rows128(a):
    flat = a.reshape(-1)
    rows = -(-flat.shape[0] // LANES)
    rows8 = -(-rows // 8) * 8
    flat = jnp.pad(flat, (0, rows8 * LANES - flat.shape[0]))
    return flat.reshape(rows8, LANES)


def _pack(parts):
    return jnp.concatenate([_rows128(p) for p in parts], axis=0)


def _unpack(buf, shapes):
    out, r = [], 0
    for shp in shapes:
        n = int(np.prod(shp))
        rows8 = -(-(-(-n // LANES)) // 8) * 8
        out.append(buf[r:r + rows8].reshape(-1)[:n].reshape(shp))
        r += rows8
    return out


def kernel(x, a_norm_g, a_w_in, a_v_norm_g, a_w_s, a_b_s, a_w_out, b_norm_g, b_w_in, b_f_bias, b_q_norm_g, b_k_norm_g, b_w_out, loss_target, m_a_norm_g, m_a_w_in, m_a_v_norm_g, m_a_w_s, m_a_b_s, m_a_w_out, m_b_norm_g, m_b_w_in, m_b_f_bias, m_b_q_norm_g, m_b_k_norm_g, m_b_w_out, v_a_norm_g, v_a_w_in, v_a_v_norm_g, v_a_w_s, v_a_b_s, v_a_w_out, v_b_norm_g, v_b_w_in, v_b_f_bias, v_b_q_norm_g, v_b_k_norm_g, v_b_w_out):
    xs = x[0]
    target = loss_target[0]
    S, D = xs.shape
    n_layers = a_w_in.shape[0]
    assert n_layers == 2
    W = a_v_norm_g.shape[1]
    G = a_w_s.shape[1]
    H = b_f_bias.shape[1]
    HW = H * HEAD
    tq_fwd = _tile(S, 512)
    tq_bwd = _tile(S, 512)
    core = lax.axis_index("c")
    chip = 2 * lax.axis_index("x") + lax.axis_index("y")

    own = dict(a_w_in=a_w_in.astype(BF16), a_w_out=a_w_out.astype(BF16), b_w_in=b_w_in.astype(BF16),
               b_w_out=b_w_out.astype(BF16), b_norm_g=b_norm_g.reshape(n_layers, 1, -1))
    cb = b_w_in.shape[2]
    w_ain, w_aout, w_bmain, w_bf, w_bout, b_norm_full = ([None] * n_layers for _ in range(6))

    def blocks_of(tensors, layer):
        return [own[n][layer] for n in tensors]

    def take(tensors, layer, arrived):
        for n, got, mine in zip(tensors, arrived, blocks_of(tensors, layer)):
            full = lax.dynamic_update_slice(got, mine[None], (chip, 0, 0))
            if n == "a_w_in":
                w_ain[layer] = full
            elif n == "a_w_out":
                w_aout[layer] = full
            elif n == "b_w_out":
                w_bout[layer] = full
            elif n == "b_norm_g":
                b_norm_full[layer] = full.reshape(D)
            else:
                cols = jnp.transpose(full, (1, 0, 2)).reshape(D, N_CHIPS * cb)
                w_bmain[layer] = cols[:, :4 * HW]
                w_bf[layer] = jnp.pad(cols[:, 4 * HW:], ((0, 0), (0, LANES - H)))

    first_a = ("a_w_in", "a_w_out")
    first_b = ("b_w_in", "b_w_out", "b_norm_g")
    take(first_a, 0, _gather_weights("gather_a0", blocks_of(first_a, 0), 0))
    causal = jnp.tril(jnp.ones((CHUNK, CHUNK), dtype=bool))
    wc = jnp.where(causal[None, None], a_w_s, 0).astype(BF16)
    wc_t = jnp.swapaxes(wc, 2, 3)
    bs_t = jnp.swapaxes(a_b_s, 1, 2)
    f_bias = jnp.pad(b_f_bias, ((0, 0), (0, LANES - H))).reshape(n_layers, 1, LANES)

    def view_ain(l):
        return _View(w_ain[l], "col")

    def view_aout(l):
        return _View(w_aout[l], "row")

    def view_bout(l):
        return _View(w_bout[l], "row")

    saved = []
    cur = xs
    for i in range(2 * n_layers):
        l = i // 2
        if i % 2 == 0:
            h, h_t = _rmsnorm_fwd(f"a{l}_norm", cur, a_norm_g[l])
            if i == 0:
                p, arrived = _matmul(f"a{l}_in", _View(h), view_ain(l), tm=1024, tn=1024, tk=2048,
                                     gathered=(blocks_of(first_b, 0), 0))
                take(first_b, 0, arrived)
            else:
                p = _matmul(f"a{l}_in", _View(h), view_ain(l), tm=1024, tn=1024, tk=2048)
            y = _gate_fwd(f"a{l}_gate", p, a_v_norm_g[l], wc[l], bs_t[l])
            nxt = _matmul(f"a{l}_out", _View(y), view_aout(l), tm=1024, tn=1024, tk=1024, residual=cur)
            saved.append((cur, h_t, p, y))
        else:
            h, h_t = _rmsnorm_fwd(f"b{l}_norm", cur, b_norm_full[l])
            proj = _matmul(f"b{l}_in", _View(h), _View(w_bmain[l]), tm=1024, tn=1024, tk=2048)
            f = _matmul(f"b{l}_inf", _View(h), _View(w_bf[l]), tm=1024, tn=LANES, tk=2048)
            cum = _fox_cum(f"b{l}_cum", f, f_bias[l])
            qa, ka, va = _qkv_prep(f"b{l}_qkv", proj, cum, b_q_norm_g[l], b_k_norm_g[l], H)
            if i == 1:
                o, y, lse, arrived = _attn_fwd(f"b{l}_attn", qa, ka, va, proj, H, tq_fwd,
                                               gathered=(blocks_of(first_a + first_b, 1), 1))
                take(first_a + first_b, 1, arrived)
            else:
                o, y, lse, _ = _attn_fwd(f"b{l}_attn", qa, ka, va, proj, H, tq_fwd)
            nxt = _matmul(f"b{l}_out", _View(y), view_bout(l), tm=1024, tn=1024, tk=1024, residual=cur)
            saved.append((cur, h_t, proj, f, qa, ka, va, o, y, lse))
        cur = nxt

    g, gb, lcols = _loss_grad(cur, target)
    loss = lax.psum(0.5 * jnp.sum(lcols) / D, ("x", "y", "c"))

    big = {"a_w_in": [None] * n_layers, "a_w_out": [None] * n_layers,
           "b_w_in": [None] * n_layers, "b_w_out": [None] * n_layers}
    small = {k: [None] * n_layers for k in
             ("a_norm_g", "a_v_norm_g", "a_w_s", "a_b_s", "b_norm_g", "b_f_bias", "b_q_norm_g", "b_k_norm_g")}
    names = ["a_w_in", "a_w_out", "b_w_in", "b_w_out"]
    reduced = [None] * n_layers

    def chip_sums_of(layer):
        mine = [big[n][layer] for n in names]
        got = _send_to_owner(f"to_owner{layer}", mine, layer)
        active = (core == layer).astype(jnp.int32).reshape(1)
        sums = []
        for n, a, b in zip(names, mine, got):
            shp = a.shape
            flat = lambda t: t.reshape(shp[0] * shp[1], shp[2])
            sums.append(_add_if(f"chipsum{layer}_{n}", flat(a), flat(b), active).reshape(shp))
        return sums, active

    def reduce_slots(layer, sums, got, active):
        slots = [lax.dynamic_update_slice(g_, lax.dynamic_index_in_dim(s_, chip, keepdims=True), (chip, 0, 0))
                 for g_, s_ in zip(got, sums)]
        return [_sum_slots(f"reduce{layer}_{n}", s_, active) for n, s_ in zip(names, slots)]

    for i in reversed(range(2 * n_layers)):
        l = i // 2
        if i % 2 == 0:
            x_in, h_t, p, y = saved[i]
            dy = _matmul(f"a{l}_dy", _View(gb), view_aout(l), tb=True, out_dtype=BF16, tm=1024, tn=1024, tk=2048)
            d_wout = _matmul(f"a{l}_dwout", _View(y), _View(gb), ta=True, tm=2048, tn=1024, tk=1024)
            dp, d_ws, d_bs, d_gv = _gate_bwd(f"a{l}_dgate", p, dy, a_v_norm_g[l], wc[l], wc_t[l], bs_t[l])
            dh = _matmul(f"a{l}_dh", _View(dp), view_ain(l), tb=True, tm=1024, tn=1024, tk=3072)
            d_win = _matmul(f"a{l}_dwin", _View(h_t), _View(dp), tm=2048, tn=1024, tk=1024, out_colblocks=N_CHIPS)
            g, gb, d_gn = _rmsnorm_bwd(f"a{l}_dnorm", x_in, a_norm_g[l], dh, g)
            big["a_w_in"][l] = d_win
            big["a_w_out"][l] = d_wout.reshape(N_CHIPS, W // N_CHIPS, D)
            small["a_norm_g"][l] = d_gn.reshape(D)
            small["a_v_norm_g"][l] = d_gv.reshape(W)
            small["a_w_s"][l] = jnp.where(causal[None], d_ws, 0.0)
            small["a_b_s"][l] = d_bs[:, :G].T
        else:
            x_in, h_t, proj, f, qa, ka, va, o, y, lse = saved[i]
            dy = _matmul(f"b{l}_dy", _View(gb), view_bout(l), tb=True, out_dtype=BF16, tm=1024, tn=1024, tk=2048)
            d_wout = _matmul(f"b{l}_dwout", _View(y), _View(gb), ta=True, tm=2048, tn=1024, tk=1024)
            lse_lanes = jnp.pad(lse.reshape(H, S).T, ((0, 0), (0, LANES - H)))
            doa, dz, qab = _attn_bwd_prep(f"b{l}_dprep", dy, o, proj, qa, lse_lanes, H)
            early = (i == 1)
            if early:
                sums1, active1 = chip_sums_of(1)
            dqa, dka, dv, got1 = _attn_bwd(f"b{l}_dattn", qab, doa, ka, va, H, tq_bwd,
                                           carried=(sums1, 1) if early else None)
            if early:
                reduced[1] = reduce_slots(1, sums1, got1, active1)
            dproj, d_gq, d_gk, dcum = _qk_bwd(f"b{l}_dqk", proj, dqa, dka, dv, dz, b_q_norm_g[l], b_k_norm_g[l], H)
            df, d_fb = _fox_cum_bwd(f"b{l}_dcum", dcum, f, f_bias[l])
            dh_f = _matmul(f"b{l}_dhf", _View(df), _View(w_bf[l]), tb=True, tm=1024, tn=1024, tk=LANES)
            dh = _matmul(f"b{l}_dh", _View(dproj), _View(w_bmain[l]), tb=True, tm=1024, tn=1024, tk=2048,
                         residual=dh_f)
            d_wmain = _matmul(f"b{l}_dwin", _View(h_t), _View(dproj), tm=2048, tn=1024, tk=1024)
            d_wf = _matmul(f"b{l}_dwinf", _View(h_t), _View(df), tm=2048, tn=LANES, tk=1024)
            d_win = jnp.concatenate([d_wmain, d_wf[:, :H]], axis=1)
            g, gb, d_gn = _rmsnorm_bwd(f"b{l}_dnorm", x_in, b_norm_full[l], dh, g)
            big["b_w_in"][l] = jnp.transpose(d_win.reshape(D, N_CHIPS, cb), (1, 0, 2))
            big["b_w_out"][l] = d_wout.reshape(N_CHIPS, HW // N_CHIPS, D)
            small["b_norm_g"][l] = d_gn.reshape(D)
            small["b_f_bias"][l] = d_fb[0, :H]
            small["b_q_norm_g"][l] = d_gq.reshape(HEAD)
            small["b_k_norm_g"][l] = d_gk.reshape(HEAD)
    grad_x = g[None]

    sums0, active0 = chip_sums_of(0)
    reduced[0] = reduce_slots(0, sums0, _scatter_chip_sums("scatter0", sums0, 0), active0)
    mine = [jnp.where(core == 0, r0, r1) for r0, r1 in zip(*reduced)]
    others = _swap_with_sibling(mine)
    grads = {n: jnp.where(core == 0, jnp.stack([m_, o_]), jnp.stack([o_, m_]))
             for n, m_, o_ in zip(names, mine, others)}

    small_names = ["a_norm_g", "a_v_norm_g", "a_w_s", "a_b_s", "b_norm_g", "b_f_bias", "b_q_norm_g", "b_k_norm_g"]
    small_parts = [jnp.stack(small[n]) for n in small_names]
    small_sum = _sum_slots("reduce_small", _gather_small(_pack(small_parts)), jnp.ones((1,), jnp.int32))
    for n, a in zip(small_names, _unpack(small_sum, [p.shape for p in small_parts])):
        grads[n] = a
    nb = b_norm_g.shape[1]
    grads["b_norm_g"] = lax.dynamic_slice_in_dim(grads["b_norm_g"], chip * nb, nb, axis=1)

    weights = dict(a_norm_g=a_norm_g, a_w_in=a_w_in, a_v_norm_g=a_v_norm_g, a_w_s=a_w_s, a_b_s=a_b_s,
                   a_w_out=a_w_out, b_norm_g=b_norm_g, b_w_in=b_w_in, b_f_bias=b_f_bias,
                   b_q_norm_g=b_q_norm_g, b_k_norm_g=b_k_norm_g, b_w_out=b_w_out)
    mom1 = dict(a_norm_g=m_a_norm_g, a_w_in=m_a_w_in, a_v_norm_g=m_a_v_norm_g, a_w_s=m_a_w_s, a_b_s=m_a_b_s,
                a_w_out=m_a_w_out, b_norm_g=m_b_norm_g, b_w_in=m_b_w_in, b_f_bias=m_b_f_bias,
                b_q_norm_g=m_b_q_norm_g, b_k_norm_g=m_b_k_norm_g, b_w_out=m_b_w_out)
    mom2 = dict(a_norm_g=v_a_norm_g, a_w_in=v_a_w_in, a_v_norm_g=v_a_v_norm_g, a_w_s=v_a_w_s, a_b_s=v_a_b_s,
                a_w_out=v_a_w_out, b_norm_g=v_b_norm_g, b_w_in=v_b_w_in, b_f_bias=v_b_f_bias,
                b_q_norm_g=v_b_q_norm_g, b_k_norm_g=v_b_k_norm_g, b_w_out=v_b_w_out)
    order = ["a_norm_g", "a_w_in", "a_v_norm_g", "a_w_s", "a_b_s", "a_w_out", "b_norm_g", "b_w_in", "b_f_bias",
             "b_q_norm_g", "b_k_norm_g", "b_w_out"]
    delta, new_m, new_v = {}, {}, {}
    for n in names:
        shp = weights[n].shape
        flat = lambda a: a.reshape(shp[0] * shp[1], shp[2])
        d, nm, nv = _adamw(f"adamw_{n}", flat(weights[n]), flat(grads[n]), flat(mom1[n]), flat(mom2[n]))
        delta[n], new_m[n], new_v[n] = d.reshape(shp), nm.reshape(shp), nv.reshape(shp)
    small_shapes = [weights[n].shape for n in small_names]
    pack_w, pack_g, pack_m, pack_v = (_pack([d[n] for n in small_names]) for d in (weights, grads, mom1, mom2))
    d, nm, nv = _adamw("adamw_small", pack_w, pack_g, pack_m, pack_v)
    for dst, buf in ((delta, d), (new_m, nm), (new_v, nv)):
        for n, a in zip(small_names, _unpack(buf, small_shapes)):
            dst[n] = a

    return (loss, grad_x, *[grads[n] for n in order], *[delta[n] for n in order],
            *[new_m[n] for n in order], *[new_v[n] for n in order])
```

```python
import functools
import math

import numpy as np
import jax
import jax.numpy as jnp
from jax import lax
from jax.experimental import pallas as pl
from jax.experimental.pallas import tpu as pltpu

F32 = jnp.float32
BF16 = jnp.bfloat16
MESH = pl.DeviceIdType.MESH

EPS = 1e-6
CHUNK = 128
HEAD = 128
LANES = 128
N_CHIPS = 4
VMEM_LIMIT = 56 * 1024 * 1024

ADAM_LR = 0.001
ADAM_B1 = 0.9
ADAM_B2 = 0.999
ADAM_EPS = 1e-08
ADAM_WD = 0.01
ADAM_STEP = 10

_NT = (((1,), (1,)), ((), ()))
_TN = (((0,), (0,)), ((), ()))
_GELU_C = math.sqrt(2.0 / math.pi)

HBM_SPEC = pl.BlockSpec(memory_space=pltpu.HBM)


def _params(*sem):
    return pltpu.CompilerParams(dimension_semantics=sem, vmem_limit_bytes=VMEM_LIMIT)


def _tile(dim, pref, unit=LANES):
    t = (min(pref, dim) // unit) * unit
    while t >= unit:
        if dim % t == 0:
            return t
        t -= unit
    return dim


def _gelu(x):
    return 0.5 * x * (1.0 + jnp.tanh(_GELU_C * (x + 0.044715 * (x * x * x))))


def _gelu_and_grad(x):
    x2 = x * x
    t = jnp.tanh(_GELU_C * (x + 0.044715 * (x2 * x)))
    val = 0.5 * x * (1.0 + t)
    grad = 0.5 * (1.0 + t) + 0.5 * x * (1.0 - t * t) * (_GELU_C * (1.0 + 3.0 * 0.044715 * x2))
    return val, grad


def _sigmoid(x):
    return 1.0 / (1.0 + jnp.exp(-x))


class _View:
    def __init__(self, arr, kind="2d", lead=()):
        self.arr, self.kind, self.lead = arr, kind, tuple(lead)
        shp = arr.shape[len(self.lead):]
        if kind == "2d":
            self.R, self.C = shp
        elif kind == "col":
            self.nb, self.R, self.cb = shp
            self.C = self.nb * self.cb
        else:
            self.nb, self.rb, self.C = shp
            self.R = self.nb * self.rb

    def fit(self, tr, tc):
        if self.kind == "col":
            tc = _tile(self.cb, tc)
        elif self.kind == "row":
            tr = _tile(self.rb, tr, unit=8)
        return tr, tc

    def spec(self, tr, tc, rc_of_grid):
        lead = self.lead
        sq = (None,) * len(lead)
        if self.kind == "2d":
            return pl.BlockSpec(sq + (tr, tc), lambda *g: lead + tuple(rc_of_grid(*g)))
        if self.kind == "col":
            q = self.cb // tc

            def im(*g):
                r, c = rc_of_grid(*g)
                return lead + (c // q, r, c % q)

            return pl.BlockSpec(sq + (None, tr, tc), im)
        q = self.rb // tr

        def im(*g):
            r, c = rc_of_grid(*g)
            return lead + (r // q, r % q, c)

        return pl.BlockSpec(sq + (None, tr, tc), im)


def _matmul(name, a, b, *, ta=False, tb=False, out_dtype=F32, tm=1024, tn=1024, tk=1024,
            out_colblocks=None, residual=None, carried=None):
    M, K = (a.C, a.R) if ta else (a.R, a.C)
    N, K2 = (b.R, b.C) if tb else (b.C, b.R)
    assert K == K2, (name, K, K2)
    tm, tn, tk = _tile(M, tm), _tile(N, tn), _tile(K, tk)
    if ta:
        tk, tm = a.fit(tk, tm)
    else:
        tm, tk = a.fit(tm, tk)
    if tb:
        tn, tk2 = b.fit(tn, tk)
    else:
        tk2, tn = b.fit(tk, tn)
    if tk2 != tk:
        tk = min(tk, tk2)
        if ta:
            tk, tm = a.fit(tk, tm)
        else:
            tm, tk = a.fit(tm, tk)
    if out_colblocks:
        tn = _tile(N // out_colblocks, tn)
    assert M % tm == 0 and N % tn == 0 and K % tk == 0, (name, M, N, K, tm, tn, tk)
    nk = K // tk
    assert nk == 1 or out_dtype == F32, name
    dims = (((0 if ta else 1,), (1 if tb else 0,)), ((), ()))
    grid = (M // tm, N // tn, nk)
    kind, g_blocks, g_layer = carried if carried else (None, [], None)
    ng = len(g_blocks)
    n_in = 2 + (residual is not None)

    def body(*refs):
        a_ref, b_ref = refs[:2]
        r_ref = refs[2] if residual is not None else None
        o_ref = refs[n_in + ng]
        k = pl.program_id(2)
        if carried:
            mine, start, finish = _carried_exchange(kind, refs[n_in:n_in + ng], refs[n_in + ng + 1:n_in + 2 * ng + 1],
                                                    refs[n_in + 2 * ng + 1:], g_layer)
            at = lambda step: functools.reduce(
                jnp.logical_and, [pl.program_id(d) == (0 if step == "first" else grid[d] - 1) for d in range(3)])

            @pl.when(jnp.logical_and(mine, at("first")))
            def _():
                start()

        def product():
            return lax.dot_general(a_ref[...], b_ref[...], dims, preferred_element_type=F32)

        if nk == 1:
            total = product()
            if r_ref is not None:
                total = total + r_ref[...]
            o_ref[...] = total.astype(out_dtype)
        else:
            @pl.when(k == 0)
            def _():
                o_ref[...] = product() + r_ref[...] if r_ref is not None else product()

            @pl.when(k > 0)
            def _():
                o_ref[...] += product()

        if carried:
            @pl.when(at("last"))
            def _():
                finish()

    a_spec = a.spec(tk, tm, lambda i, j, k: (k, i)) if ta else a.spec(tm, tk, lambda i, j, k: (i, k))
    b_spec = b.spec(tn, tk, lambda i, j, k: (j, k)) if tb else b.spec(tk, tn, lambda i, j, k: (k, j))
    in_specs, args = [a_spec, b_spec], [a.arr, b.arr]
    if residual is not None:
        in_specs.append(pl.BlockSpec((tm, tn), lambda i, j, k: (i, j)))
        args.append(residual)
    in_specs += [HBM_SPEC] * ng
    args += list(g_blocks)
    if out_colblocks:
        q = (N // out_colblocks) // tn
        out_shape = jax.ShapeDtypeStruct((out_colblocks, M, N // out_colblocks), out_dtype)
        out_spec = pl.BlockSpec((None, tm, tn), lambda i, j, k: (j // q, i, j % q))
    else:
        out_shape = jax.ShapeDtypeStruct((M, N), out_dtype)
        out_spec = pl.BlockSpec((tm, tn), lambda i, j, k: (i, j))
    out = pl.pallas_call(
        body, name=name, grid=grid, in_specs=in_specs, out_specs=[out_spec] + [HBM_SPEC] * ng,
        out_shape=[out_shape] + _carried_out_shapes(kind, g_blocks),
        scratch_shapes=_carried_sems(kind, ng),
        compiler_params=_params("arbitrary", "arbitrary", "arbitrary"),
    )(*args)
    return (out[0], list(out[1:])) if carried else out[0]


def _rmsnorm_fwd(name, x, gain):
    S, D = x.shape
    tr = _tile(S, 512)

    def body(x_ref, g_ref, h_ref, ht_ref):
        xv = x_ref[...]
        r = lax.rsqrt(jnp.mean(xv * xv, axis=-1, keepdims=True) + EPS)
        h = xv * r * g_ref[...]
        h_ref[...] = h.astype(BF16)
        ht_ref[...] = h.T.astype(BF16)

    return pl.pallas_call(
        body, name=name, grid=(S // tr,),
        in_specs=[pl.BlockSpec((tr, D), lambda i: (i, 0)), pl.BlockSpec((1, D), lambda i: (0, 0))],
        out_specs=[pl.BlockSpec((tr, D), lambda i: (i, 0)), pl.BlockSpec((D, tr), lambda i: (0, i))],
        out_shape=[jax.ShapeDtypeStruct((S, D), BF16), jax.ShapeDtypeStruct((D, S), BF16)],
        compiler_params=_params("arbitrary"),
    )(x, gain.reshape(1, D))


def _rmsnorm_bwd(name, x, gain, dh, g_res):
    S, D = x.shape
    tr = _tile(S, 256, unit=8)

    def body(x_ref, g_ref, dh_ref, res_ref, dx_ref, dxb_ref, dg_ref):
        i = pl.program_id(0)
        xv = x_ref[...]
        r = lax.rsqrt(jnp.mean(xv * xv, axis=-1, keepdims=True) + EPS)
        xhat = xv * r
        dhv = dh_ref[...]
        part = jnp.sum(dhv * xhat, axis=0, keepdims=True)

        @pl.when(i == 0)
        def _():
            dg_ref[...] = part

        @pl.when(i > 0)
        def _():
            dg_ref[...] += part

        dxhat = dhv * g_ref[...]
        dx = res_ref[...] + r * (dxhat - xhat * jnp.mean(dxhat * xhat, axis=-1, keepdims=True))
        dx_ref[...] = dx
        dxb_ref[...] = dx.astype(BF16)

    row = pl.BlockSpec((tr, D), lambda i: (i, 0))
    vec = pl.BlockSpec((1, D), lambda i: (0, 0))
    return pl.pallas_call(
        body, name=name, grid=(S // tr,), in_specs=[row, vec, row, row], out_specs=[row, row, vec],
        out_shape=[jax.ShapeDtypeStruct((S, D), F32), jax.ShapeDtypeStruct((S, D), BF16),
                   jax.ShapeDtypeStruct((1, D), F32)],
        compiler_params=_params("arbitrary"),
    )(x, gain.reshape(1, D), dh, g_res)


def _loss_grad(x, target):
    S, D = x.shape
    tr = _tile(S, 512, unit=8)

    def body(x_ref, t_ref, g_ref, gb_ref, l_ref):
        i = pl.program_id(0)
        e = x_ref[...] - t_ref[...]
        g = e * (1.0 / D)
        g_ref[...] = g
        gb_ref[...] = g.astype(BF16)
        part = jnp.sum(e * e, axis=0, keepdims=True)

        @pl.when(i == 0)
        def _():
            l_ref[...] = part

        @pl.when(i > 0)
        def _():
            l_ref[...] += part

    row = pl.BlockSpec((tr, D), lambda i: (i, 0))
    vec = pl.BlockSpec((1, D), lambda i: (0, 0))
    return pl.pallas_call(
        body, name="loss_grad", grid=(S // tr,), in_specs=[row, row], out_specs=[row, row, vec],
        out_shape=[jax.ShapeDtypeStruct((S, D), F32), jax.ShapeDtypeStruct((S, D), BF16),
                   jax.ShapeDtypeStruct((1, D), F32)],
        compiler_params=_params("arbitrary"),
    )(x, target)


def _gate_fwd(name, p, v_gain, wc, bs_t):
    S, W3 = p.shape
    W = W3 // 3
    G = wc.shape[0]
    gd = W // G

    def body(p_ref, gv_ref, wc_ref, bs_ref, y_ref):
        vg = _gelu(p_ref[:, W:2 * W].astype(F32))
        r = lax.rsqrt(jnp.mean(vg * vg, axis=-1, keepdims=True) + EPS)
        vb = (vg * r * gv_ref[...]).astype(BF16)
        zp = p_ref[:, 2 * W:].astype(F32)
        gate = _gelu(p_ref[:, :W].astype(F32)) * (zp * _sigmoid(zp))
        for g in range(G):
            sl = slice(g * gd, (g + 1) * gd)
            mixed = jnp.dot(wc_ref[g], vb[:, sl], preferred_element_type=F32) + bs_ref[:, g:g + 1]
            y_ref[:, sl] = (gate[:, sl] * mixed).astype(BF16)

    return pl.pallas_call(
        body, name=name, grid=(S // CHUNK,),
        in_specs=[pl.BlockSpec((CHUNK, W3), lambda i: (i, 0)), pl.BlockSpec((1, W), lambda i: (0, 0)),
                  pl.BlockSpec((G, CHUNK, CHUNK), lambda i: (0, 0, 0)), pl.BlockSpec((CHUNK, G), lambda i: (0, 0))],
        out_specs=pl.BlockSpec((CHUNK, W), lambda i: (i, 0)),
        out_shape=jax.ShapeDtypeStruct((S, W), BF16),
        compiler_params=_params("arbitrary"),
    )(p, v_gain.reshape(1, W), wc, bs_t)


def _gate_bwd(name, p, dy, v_gain, wc, wc_t, bs_t):
    S, W3 = p.shape
    W = W3 // 3
    G = wc.shape[0]
    gd = W // G

    def body(p_ref, dy_ref, gv_ref, wc_ref, wct_ref, bs_ref, dp_ref, dws_ref, dbs_ref, dgv_ref, dv_scr):
        i = pl.program_id(0)

        @pl.when(i == 0)
        def _():
            dws_ref[...] = jnp.zeros_like(dws_ref)
            dbs_ref[...] = jnp.zeros_like(dbs_ref)
            dgv_ref[...] = jnp.zeros_like(dgv_ref)

        gu, dgu = _gelu_and_grad(p_ref[:, :W].astype(F32))
        vg, dvg_dv = _gelu_and_grad(p_ref[:, W:2 * W].astype(F32))
        zp = p_ref[:, 2 * W:].astype(F32)
        sig = _sigmoid(zp)
        sz = zp * sig
        dsz = sig * (1.0 + zp * (1.0 - sig))
        r = lax.rsqrt(jnp.mean(vg * vg, axis=-1, keepdims=True) + EPS)
        vhat = vg * r
        gv = gv_ref[...]
        vb = (vhat * gv).astype(BF16)
        dy = dy_ref[...].astype(F32)
        lane = lax.broadcasted_iota(jnp.int32, (CHUNK, LANES), 1)
        dbs = jnp.zeros((CHUNK, LANES), F32)
        for g in range(G):
            sl = slice(g * gd, (g + 1) * gd)
            vsl = vb[:, sl]
            mixed = jnp.dot(wc_ref[g], vsl, preferred_element_type=F32) + bs_ref[:, g:g + 1]
            dyg, gug, szg = dy[:, sl], gu[:, sl], sz[:, sl]
            dp_ref[:, sl] = (dyg * mixed * szg * dgu[:, sl]).astype(BF16)
            dp_ref[:, 2 * W + g * gd:2 * W + (g + 1) * gd] = (dyg * gug * mixed * dsz[:, sl]).astype(BF16)
            dm = dyg * gug * szg
            dmb = dm.astype(BF16)
            dws_ref[g] += lax.dot_general(dmb, vsl, _NT, preferred_element_type=F32)
            dbs = dbs + jnp.where(lane == g, jnp.sum(dm, axis=1, keepdims=True), 0.0)
            dv_scr[:, sl] = jnp.dot(wct_ref[g], dmb, preferred_element_type=F32)
        dbs_ref[...] += dbs
        dv = dv_scr[...]
        dgv_ref[...] += jnp.sum(dv * vhat, axis=0, keepdims=True)
        dvhat = dv * gv
        dvg = r * (dvhat - vhat * jnp.mean(dvhat * vhat, axis=-1, keepdims=True))
        dp_ref[:, W:2 * W] = (dvg * dvg_dv).astype(BF16)

    return pl.pallas_call(
        body, name=name, grid=(S // CHUNK,),
        in_specs=[pl.BlockSpec((CHUNK, W3), lambda i: (i, 0)), pl.BlockSpec((CHUNK, W), lambda i: (i, 0)),
                  pl.BlockSpec((1, W), lambda i: (0, 0)),
                  pl.BlockSpec((G, CHUNK, CHUNK), lambda i: (0, 0, 0)),
                  pl.BlockSpec((G, CHUNK, CHUNK), lambda i: (0, 0, 0)),
                  pl.BlockSpec((CHUNK, G), lambda i: (0, 0))],
        out_specs=[pl.BlockSpec((CHUNK, W3), lambda i: (i, 0)),
                   pl.BlockSpec((G, CHUNK, CHUNK), lambda i: (0, 0, 0)),
                   pl.BlockSpec((CHUNK, LANES), lambda i: (0, 0)),
                   pl.BlockSpec((1, W), lambda i: (0, 0))],
        out_shape=[jax.ShapeDtypeStruct((S, W3), BF16), jax.ShapeDtypeStruct((G, CHUNK, CHUNK), F32),
                   jax.ShapeDtypeStruct((CHUNK, LANES), F32), jax.ShapeDtypeStruct((1, W), F32)],
        scratch_shapes=[pltpu.VMEM((CHUNK, W), F32)],
        compiler_params=_params("arbitrary"),
    )(p, dy, v_gain.reshape(1, W), wc, wc_t, bs_t)


AUG = 2 * HEAD
LOG2E = 1.0 / math.log(2.0)
Q_SUM_LANE = HEAD + 3
K_SUM_LANE = HEAD


def _pieces(x, sign=1.0):
    hi, mid, lo = _split3(sign * x)
    return hi.astype(F32), mid.astype(F32), lo.astype(F32)


def _lanes(lane, start, vals, rest):
    out = rest
    for n, v in enumerate(vals):
        out = jnp.where(lane == start + n, v, out)
    return out


def _qkv_prep(name, proj, cum, q_gain, k_gain, H):
    S = proj.shape[0]
    HW = H * HEAD
    tr = _tile(S, 256, unit=8)
    sigma = (HEAD ** -0.5) * LOG2E

    def body(q_ref, k_ref, v_ref, c_ref, gq_ref, gk_ref, qa_ref, ka_ref, va_ref):
        lane = lax.broadcasted_iota(jnp.int32, (tr, HEAD), 1)
        zero = jnp.zeros((tr, HEAD), F32)
        v_aug = jnp.where(lane < 3, 1.0, zero).astype(BF16)
        for h in range(H):
            sl = slice(h * HEAD, (h + 1) * HEAD)
            a0 = h * AUG
            t = q_ref[:, sl].astype(F32)
            r = lax.rsqrt(jnp.mean(t * t, axis=-1, keepdims=True) + EPS)
            qa_ref[:, a0:a0 + HEAD] = (t * r * gq_ref[...] * sigma).astype(BF16)
            t = k_ref[:, sl].astype(F32)
            r = lax.rsqrt(jnp.mean(t * t, axis=-1, keepdims=True) + EPS)
            ka_ref[:, a0:a0 + HEAD] = (t * r * gk_ref[...]).astype(BF16)
            va_ref[:, a0:a0 + HEAD] = v_ref[:, sl]
            va_ref[:, a0 + HEAD:a0 + AUG] = v_aug
            c2 = c_ref[:, h:h + 1] * LOG2E
            qa_ref[:, a0 + HEAD:a0 + AUG] = _lanes(lane, 0, _pieces(c2) + (1.0, 1.0, 1.0), zero).astype(BF16)
            ka_ref[:, a0 + HEAD:a0 + AUG] = _lanes(
                lane, 0, (1.0, 1.0, 1.0) + _pieces(c2, -1.0) + (1.0, 1.0, 1.0), zero).astype(BF16)

    col = lambda c: pl.BlockSpec((tr, HW), lambda i: (i, c))
    vec = pl.BlockSpec((1, HEAD), lambda i: (0, 0))
    aug = pl.BlockSpec((tr, H * AUG), lambda i: (i, 0))
    return pl.pallas_call(
        body, name=name, grid=(S // tr,),
        in_specs=[col(0), col(1), col(2), pl.BlockSpec((tr, LANES), lambda i: (i, 0)), vec, vec],
        out_specs=[aug] * 3, out_shape=[jax.ShapeDtypeStruct((S, H * AUG), BF16)] * 3,
        compiler_params=_params("arbitrary"),
    )(proj, proj, proj, cum, q_gain.reshape(1, HEAD), k_gain.reshape(1, HEAD))


def _split3(x):
    hi = x.astype(BF16)
    r1 = x - hi.astype(F32)
    mid = r1.astype(BF16)
    lo = (r1 - mid.astype(F32)).astype(BF16)
    return hi, mid, lo


def _tri_sum(tri, x):
    hi, mid, lo = _split3(x)
    d = lambda t: jnp.dot(tri, t, preferred_element_type=F32)
    return d(hi) + (d(mid) + d(lo))


def _log_sigmoid(x):
    return jnp.minimum(x, 0.0) - jnp.log(1.0 + jnp.exp(-jnp.abs(x)))


def _fox_cum(name, f, bias):
    S = f.shape[0]
    tb = _tile(S, 256, unit=8)

    def body(f_ref, b_ref, c_ref):
        rr = lax.broadcasted_iota(jnp.int32, (tb, tb), 0)
        cc = lax.broadcasted_iota(jnp.int32, (tb, tb), 1)
        tri = (rr >= cc).astype(BF16)

        def step(t, carry):
            off = pl.multiple_of(t * tb, tb)
            lf = _log_sigmoid(f_ref[pl.ds(off, tb), :] + b_ref[...])
            c = _tri_sum(tri, lf) + carry
            c_ref[pl.ds(off, tb), :] = c
            return c[tb - 1:tb, :]

        lax.fori_loop(0, S // tb, step, jnp.zeros((1, LANES), F32))

    return pl.pallas_call(
        body, name=name, out_shape=jax.ShapeDtypeStruct((S, LANES), F32),
        in_specs=[pl.BlockSpec(memory_space=pltpu.VMEM)] * 2, out_specs=pl.BlockSpec(memory_space=pltpu.VMEM),
        compiler_params=pltpu.CompilerParams(vmem_limit_bytes=VMEM_LIMIT),
    )(f, bias)


def _fox_cum_bwd(name, dcum, f, bias):
    S = f.shape[0]
    tb = _tile(S, 256, unit=8)
    nb = S // tb

    def body(dc_ref, f_ref, b_ref, df_ref, db_ref):
        rr = lax.broadcasted_iota(jnp.int32, (tb, tb), 0)
        cc = lax.broadcasted_iota(jnp.int32, (tb, tb), 1)
        tri = (rr <= cc).astype(BF16)

        def step(t, carry):
            tail, dbias = carry
            off = pl.multiple_of((nb - 1 - t) * tb, tb)
            dlf = _tri_sum(tri, dc_ref[pl.ds(off, tb), :]) + tail
            d = dlf * _sigmoid(-(f_ref[pl.ds(off, tb), :] + b_ref[...]))
            df_ref[pl.ds(off, tb), :] = d.astype(BF16)
            return dlf[0:1, :], dbias + jnp.sum(d, axis=0, keepdims=True)

        z = jnp.zeros((1, LANES), F32)
        _, dbias = lax.fori_loop(0, nb, step, (z, z))
        db_ref[...] = dbias

    vm = pl.BlockSpec(memory_space=pltpu.VMEM)
    return pl.pallas_call(
        body, name=name, out_shape=[jax.ShapeDtypeStruct((S, LANES), BF16), jax.ShapeDtypeStruct((1, LANES), F32)],
        in_specs=[vm] * 3, out_specs=[vm] * 2,
        compiler_params=pltpu.CompilerParams(vmem_limit_bytes=VMEM_LIMIT),
    )(dcum, f, bias)


def _attn_fwd(name, qa, ka, va, proj, H, tq, carried=None):
    S = qa.shape[0]
    HW = H * HEAD
    nq = S // tq
    hp = 2 if H % 2 == 0 else 1
    kind, g_blocks, g_layer = carried if carried else (None, [], None)
    ng = len(g_blocks)

    def body(*refs):
        q_ref, k_ref, v_ref, z_ref = refs[:4]
        o_ref, y_ref, lse_ref = refs[4 + ng:7 + ng]
        i = pl.program_id(1)
        if carried:
            mine, start, finish = _carried_exchange(kind, refs[4:4 + ng], refs[7 + ng:7 + 2 * ng],
                                                    refs[7 + 2 * ng:], g_layer)
            hd = pl.program_id(0)

            @pl.when(jnp.logical_and(mine, jnp.logical_and(hd == 0, i == 0)))
            def _():
                start()

        def step(j, carry, masked):
            off = pl.multiple_of(j * tq, tq)
            out = []
            for n in range(hp):
                m, acc = carry[n]
                a = slice(n * AUG, (n + 1) * AUG)
                s = lax.dot_general(q_ref[:, a], k_ref[pl.ds(off, tq), a], _NT, preferred_element_type=F32)
                if masked:
                    rr = lax.broadcasted_iota(jnp.int32, (tq, tq), 0)
                    cc = lax.broadcasted_iota(jnp.int32, (tq, tq), 1)
                    s = jnp.where(rr >= cc, s, -jnp.inf)
                m_new = jnp.maximum(m, jnp.max(s, axis=1, keepdims=True))
                pr = jnp.exp2(s - m_new).astype(BF16)
                acc = jnp.exp2(m - m_new) * acc + jnp.dot(pr, v_ref[pl.ds(off, tq), a], preferred_element_type=F32)
                out.append((m_new, acc))
            return tuple(out)

        init = ((jnp.full((tq, 1), -jnp.inf, F32), jnp.zeros((tq, AUG), F32)),) * hp
        carry = lax.fori_loop(0, i, lambda j, c: step(j, c, False), init)
        carry = step(i, carry, True)
        for n in range(hp):
            m, acc = carry[n]
            sl = slice(n * HEAD, (n + 1) * HEAD)
            l = acc[:, HEAD:HEAD + 1]
            o = acc[:, :HEAD] / l
            z = z_ref[:, sl].astype(F32)
            o_ref[:, sl] = o
            y_ref[:, sl] = (o * (z * _sigmoid(z))).astype(BF16)
            lse_ref[n] = m + jnp.log(l) * LOG2E

        if carried:
            @pl.when(jnp.logical_and(hd == H // hp - 1, i == nq - 1))
            def _():
                finish()

    qspec = pl.BlockSpec((tq, hp * AUG), lambda h, i: (i, h))
    kvspec = pl.BlockSpec((S, hp * AUG), lambda h, i: (0, h))
    ospec = pl.BlockSpec((tq, hp * HEAD), lambda h, i: (i, h))
    out = pl.pallas_call(
        body, name=name, grid=(H // hp, nq),
        in_specs=[qspec, kvspec, kvspec, pl.BlockSpec((tq, hp * HEAD), lambda h, i: (i, 3 * H // hp + h))]
        + [HBM_SPEC] * ng,
        out_specs=[ospec, ospec, pl.BlockSpec((hp, tq, 1), lambda h, i: (h, i, 0))] + [HBM_SPEC] * ng,
        out_shape=[jax.ShapeDtypeStruct((S, HW), F32), jax.ShapeDtypeStruct((S, HW), BF16),
                   jax.ShapeDtypeStruct((H, S, 1), F32)] + _carried_out_shapes(kind, g_blocks),
        scratch_shapes=_carried_sems(kind, ng),
        compiler_params=_params("arbitrary", "arbitrary"),
    )(qa, ka, va, proj, *g_blocks)
    return out[0], out[1], out[2], list(out[3:])


def _attn_bwd_prep(name, dy, o, proj, qa, lse, H):
    S, HW = o.shape
    tr = _tile(S, 256, unit=8)

    def body(dy_ref, o_ref, z_ref, qa_ref, lse_ref, doa_ref, dz_ref, qab_ref):
        lane = lax.broadcasted_iota(jnp.int32, (tr, HEAD), 1)
        zero = jnp.zeros((tr, HEAD), F32)
        for h in range(H):
            sl = slice(h * HEAD, (h + 1) * HEAD)
            a0 = h * AUG
            dy = dy_ref[:, sl].astype(F32)
            z = z_ref[:, sl].astype(F32)
            o = o_ref[:, sl]
            sig = _sigmoid(z)
            dob = (dy * (z * sig)).astype(BF16)
            dz_ref[:, sl] = (dy * o * (sig * (1.0 + z * (1.0 - sig)))).astype(BF16)
            delta = jnp.sum(dob.astype(F32) * o, axis=1, keepdims=True)
            doa_ref[:, a0:a0 + HEAD] = dob
            doa_ref[:, a0 + HEAD:a0 + AUG] = _lanes(lane, 0, _pieces(delta, -1.0), zero).astype(BF16)
            qab_ref[:, a0:a0 + HEAD] = qa_ref[:, a0:a0 + HEAD]
            qab_ref[:, a0 + HEAD:a0 + AUG] = _lanes(
                lane, 6, _pieces(lse_ref[:, h:h + 1], -1.0), qa_ref[:, a0 + HEAD:a0 + AUG].astype(F32)).astype(BF16)

    row = pl.BlockSpec((tr, HW), lambda i: (i, 0))
    aug = pl.BlockSpec((tr, H * AUG), lambda i: (i, 0))
    return pl.pallas_call(
        body, name=name, grid=(S // tr,),
        in_specs=[row, row, pl.BlockSpec((tr, HW), lambda i: (i, 3)), aug, pl.BlockSpec((tr, LANES), lambda i: (i, 0))],
        out_specs=[aug, row, aug],
        out_shape=[jax.ShapeDtypeStruct((S, H * AUG), BF16), jax.ShapeDtypeStruct((S, HW), BF16),
                   jax.ShapeDtypeStruct((S, H * AUG), BF16)],
        compiler_params=_params("arbitrary"),
    )(dy, o, proj, qa, lse)


def _attn_bwd(name, qab, doa, ka, va, H, tq, carried=None):
    S = qab.shape[0]
    nq = S // tq
    sums, owner = carried if carried else ([], None)
    nt = len(sums)

    def body(*refs):
        q_ref, do_ref, k_ref, v_ref = refs[:4]
        dq_ref, dk_ref, dv_ref = refs[4 + nt:7 + nt]
        j = pl.program_id(1)
        if carried:
            mine, start, wait = _scatter_copies(refs[4:4 + nt], refs[7 + nt:7 + 2 * nt], *refs[7 + 2 * nt:], owner)
            hd = pl.program_id(0)

            @pl.when(jnp.logical_and(mine, jnp.logical_and(hd == 0, j == 0)))
            def _():
                start()

        @pl.when(j == 0)
        def _():
            dq_ref[...] = jnp.zeros_like(dq_ref)

        k = k_ref[...]
        v = v_ref[...]

        def step(i, carry, masked):
            dk_acc, dv_acc = carry
            off = pl.multiple_of(i * tq, tq)
            q = q_ref[pl.ds(off, tq), :]
            do = do_ref[pl.ds(off, tq), :]
            st = lax.dot_general(k, q, _NT, preferred_element_type=F32)
            if masked:
                rr = lax.broadcasted_iota(jnp.int32, (tq, tq), 0)
                cc = lax.broadcasted_iota(jnp.int32, (tq, tq), 1)
                st = jnp.where(cc >= rr, st, -jnp.inf)
            pt = jnp.exp2(st)
            dst = pt * lax.dot_general(v, do, _NT, preferred_element_type=F32)
            dsb = dst.astype(BF16)
            dv_acc = dv_acc + jnp.dot(pt.astype(BF16), do[:, :HEAD], preferred_element_type=F32)
            dk_acc = dk_acc + jnp.dot(dsb, q, preferred_element_type=F32)
            dq_ref[pl.ds(off, tq), :] += lax.dot_general(dsb, k, _TN, preferred_element_type=F32)
            return dk_acc, dv_acc

        carry = step(j, (jnp.zeros((tq, AUG), F32), jnp.zeros((tq, HEAD), F32)), True)
        dk_acc, dv_acc = lax.fori_loop(j + 1, nq, lambda i, c: step(i, c, False), carry)
        dk_ref[...] = dk_acc
        dv_ref[...] = dv_acc
        if carried:
            @pl.when(jnp.logical_and(mine, jnp.logical_and(hd == H - 1, j == nq - 1)))
            def _():
                wait()

    full = pl.BlockSpec((S, AUG), lambda h, j: (0, h))
    blk = pl.BlockSpec((tq, AUG), lambda h, j: (j, h))
    out = pl.pallas_call(
        body, name=name, grid=(H, nq),
        in_specs=[full, full, blk, blk] + [HBM_SPEC] * nt,
        out_specs=[full, blk, pl.BlockSpec((tq, HEAD), lambda h, j: (j, h))] + [HBM_SPEC] * nt,
        out_shape=[jax.ShapeDtypeStruct((S, H * AUG), F32), jax.ShapeDtypeStruct((S, H * AUG), F32),
                   jax.ShapeDtypeStruct((S, H * HEAD), F32)] + [jax.ShapeDtypeStruct(s.shape, F32) for s in sums],
        scratch_shapes=[pltpu.SemaphoreType.DMA((3 * nt,))] * 2 if carried else [],
        compiler_params=_params("arbitrary", "arbitrary"),
    )(qab, doa, ka, va, *sums)
    return out[0], out[1], out[2], list(out[3:])


def _qk_bwd(name, proj, dqa, dka, dv, dz, q_gain, k_gain, H):
    S = proj.shape[0]
    HW = H * HEAD
    tr = _tile(S, 256, unit=8)
    scale = HEAD ** -0.5
    factors = (scale, 1.0 / LOG2E)

    def body(q_ref, k_ref, dq_ref, dk_ref, dv_ref, dz_ref, gq_ref, gk_ref, dp_ref, dgq_ref, dgk_ref, dc_ref):
        i = pl.program_id(0)

        @pl.when(i == 0)
        def _():
            dgq_ref[...] = jnp.zeros_like(dgq_ref)
            dgk_ref[...] = jnp.zeros_like(dgk_ref)

        for n, (src, dsrc, gain, dgain) in enumerate(((q_ref, dq_ref, gq_ref, dgq_ref), (k_ref, dk_ref, gk_ref, dgk_ref))):
            acc = jnp.zeros((1, HEAD), F32)
            for h in range(H):
                sl = slice(h * HEAD, (h + 1) * HEAD)
                t = src[:, sl].astype(F32)
                r = lax.rsqrt(jnp.mean(t * t, axis=-1, keepdims=True) + EPS)
                that = t * r
                dn = dsrc[:, h * AUG:h * AUG + HEAD] * factors[n]
                acc = acc + jnp.sum(dn * that, axis=0, keepdims=True)
                dhat = dn * gain[...]
                dt = r * (dhat - that * jnp.mean(dhat * that, axis=-1, keepdims=True))
                dp_ref[:, n * HW + h * HEAD:n * HW + (h + 1) * HEAD] = dt.astype(BF16)
            dgain[...] += acc
        dp_ref[:, 2 * HW:3 * HW] = dv_ref[...].astype(BF16)
        dp_ref[:, 3 * HW:] = dz_ref[...]
        lane = lax.broadcasted_iota(jnp.int32, (tr, LANES), 1)
        dc = jnp.zeros((tr, LANES), F32)
        for h in range(H):
            qs = dq_ref[:, h * AUG + K_SUM_LANE:h * AUG + K_SUM_LANE + 1]
            ks = dk_ref[:, h * AUG + Q_SUM_LANE:h * AUG + Q_SUM_LANE + 1]
            dc = jnp.where(lane == h, qs - ks, dc)
        dc_ref[...] = dc

    col = lambda c: pl.BlockSpec((tr, HW), lambda i: (i, c))
    row = col(0)
    aug = pl.BlockSpec((tr, H * AUG), lambda i: (i, 0))
    vec = pl.BlockSpec((1, HEAD), lambda i: (0, 0))
    return pl.pallas_call(
        body, name=name, grid=(S // tr,),
        in_specs=[col(0), col(1), aug, aug, row, row, vec, vec],
        out_specs=[pl.BlockSpec((tr, 4 * HW), lambda i: (i, 0)), vec, vec, pl.BlockSpec((tr, LANES), lambda i: (i, 0))],
        out_shape=[jax.ShapeDtypeStruct((S, 4 * HW), BF16), jax.ShapeDtypeStruct((1, HEAD), F32),
                   jax.ShapeDtypeStruct((1, HEAD), F32), jax.ShapeDtypeStruct((S, LANES), F32)],
        compiler_params=_params("arbitrary"),
    )(proj, proj, dqa, dka, dv, dz, q_gain.reshape(1, HEAD), k_gain.reshape(1, HEAD))


def _row_tile(R, C, budget_bytes=1 << 20):
    cap = max(8, budget_bytes // (4 * C))
    t = (min(cap, R) // 8) * 8
    while t >= 8:
        if R % t == 0:
            return t
        t -= 8
    return R


def _add_if(name, a, b, active):
    R, C = a.shape
    tr = _row_tile(R, C)

    def body(act_ref, a_ref, b_ref, o_ref):
        @pl.when(act_ref[0] != 0)
        def _():
            o_ref[...] = a_ref[...] + b_ref[...]

    blk = pl.BlockSpec((tr, C), lambda i, s: (i * s[0], 0))
    grid_spec = pltpu.PrefetchScalarGridSpec(
        num_scalar_prefetch=1, grid=(R // tr,), in_specs=[blk, blk], out_specs=blk)
    return pl.pallas_call(
        body, name=name, grid_spec=grid_spec, out_shape=jax.ShapeDtypeStruct((R, C), F32),
        compiler_params=_params("arbitrary"),
    )(active, a, b)


def _sum_slots(name, slots, active):
    n, R, C = slots.shape
    tr = _row_tile(R, C, budget_bytes=(1 << 20) // 2)

    def body(act_ref, s_ref, o_ref):
        @pl.when(act_ref[0] != 0)
        def _():
            acc = s_ref[0]
            for k in range(1, n):
                acc = acc + s_ref[k]
            o_ref[...] = acc

    grid_spec = pltpu.PrefetchScalarGridSpec(
        num_scalar_prefetch=1, grid=(R // tr,),
        in_specs=[pl.BlockSpec((n, tr, C), lambda i, s: (0, i * s[0], 0))],
        out_specs=pl.BlockSpec((tr, C), lambda i, s: (i * s[0], 0)))
    return pl.pallas_call(
        body, name=name, grid_spec=grid_spec, out_shape=jax.ShapeDtypeStruct((R, C), F32),
        compiler_params=_params("arbitrary"),
    )(active, slots)


def _adamw(name, w, g, m, v):
    R, C = w.shape
    tr = _row_tile(R, C, budget_bytes=(1 << 20) // 2)
    c1 = 1.0 - ADAM_B1 ** ADAM_STEP
    c2 = 1.0 - ADAM_B2 ** ADAM_STEP

    def body(w_ref, g_ref, m_ref, v_ref, d_ref, nm_ref, nv_ref):
        gv = g_ref[...]
        nm = ADAM_B1 * m_ref[...] + (1.0 - ADAM_B1) * gv
        nv = ADAM_B2 * v_ref[...] + (1.0 - ADAM_B2) * (gv * gv)
        m_hat = nm / c1
        v_hat = nv / c2
        d_ref[...] = -ADAM_LR * (m_hat / (jnp.sqrt(v_hat) + ADAM_EPS) + ADAM_WD * w_ref[...])
        nm_ref[...] = nm
        nv_ref[...] = nv

    blk = pl.BlockSpec((tr, C), lambda i: (i, 0))
    return pl.pallas_call(
        body, name=name, grid=(R // tr,), in_specs=[blk] * 4, out_specs=[blk] * 3,
        out_shape=[jax.ShapeDtypeStruct((R, C), F32)] * 3,
        compiler_params=_params("arbitrary"),
    )(w, g, m, v)


def _place():
    x, y, c = lax.axis_index("x"), lax.axis_index("y"), lax.axis_index("c")
    chips = [(1 - x, y), (x, 1 - y), (1 - x, 1 - y)]
    return x, y, c, chips


def _gather_copies(ins, outs, sems, layer):
    s_send, s_recv, f_send, f_recv = sems
    nt = len(ins)
    x, y, c, chips = _place()
    me = 2 * x + y
    ids = [2 * cx + cy for cx, cy in chips]
    pairs = [(t, k) for t in range(nt) for k in range(3)]

    def over_ici(t, k, block):
        return pltpu.make_async_remote_copy(
            src_ref=ins[t], dst_ref=outs[t].at[block], send_sem=s_send.at[3 * t + k],
            recv_sem=s_recv.at[3 * t + k], device_id=(*chips[k], layer), device_id_type=MESH)

    def over_d2d(t, k):
        blk = outs[t].at[ids[k]]
        return pltpu.make_async_remote_copy(
            src_ref=blk, dst_ref=blk, send_sem=f_send.at[3 * t + k], recv_sem=f_recv.at[3 * t + k],
            device_id=(x, y, 1 - c), device_id_type=MESH)

    def start():
        for t, k in pairs:
            over_ici(t, k, me).start()

    def finish():
        @pl.when(c == layer)
        def _():
            for t, k in pairs:
                over_ici(t, k, ids[k]).wait_recv()
                over_d2d(t, k).start()
            for t, k in pairs:
                over_ici(t, k, me).wait_send()
                over_d2d(t, k).wait_send()

        @pl.when(c != layer)
        def _():
            for t, k in pairs:
                over_d2d(t, k).wait_recv()

    return c == layer, start, finish


def _gather_out_shapes(blocks):
    return [jax.ShapeDtypeStruct((N_CHIPS,) + b.shape, b.dtype) for b in blocks]


def _gather_sems(n):
    return [pltpu.SemaphoreType.DMA((3 * n,))] * 4


def _gather_weights(name, blocks, layer):
    nt = len(blocks)

    def body(*refs):
        mine, start, finish = _gather_copies(refs[:nt], refs[nt:2 * nt], refs[2 * nt:], layer)

        @pl.when(mine)
        def _():
            start()

        finish()

    return pl.pallas_call(
        body, name=name, out_shape=_gather_out_shapes(blocks),
        in_specs=[HBM_SPEC] * nt, out_specs=[HBM_SPEC] * nt, scratch_shapes=_gather_sems(nt),
    )(*blocks)


def _send_to_owner(name, grads, owner):
    nt = len(grads)

    def body(*refs):
        ins, outs = refs[:nt], refs[nt:2 * nt]
        s_send, s_recv = refs[2 * nt:]
        x, y, c, _ = _place()

        def copy(t):
            return pltpu.make_async_remote_copy(
                src_ref=ins[t], dst_ref=outs[t], send_sem=s_send.at[t], recv_sem=s_recv.at[t],
                device_id=(x, y, owner), device_id_type=MESH)

        @pl.when(c != owner)
        def _():
            for t in range(nt):
                copy(t).start()
            for t in range(nt):
                copy(t).wait_send()

        @pl.when(c == owner)
        def _():
            for t in range(nt):
                copy(t).wait_recv()

    return pl.pallas_call(
        body, name=name,
        out_shape=[jax.ShapeDtypeStruct(g.shape, F32) for g in grads],
        in_specs=[HBM_SPEC] * nt, out_specs=[HBM_SPEC] * nt,
        scratch_shapes=[pltpu.SemaphoreType.DMA((nt,))] * 2,
    )(*grads)


def _scatter_copies(ins, outs, s_send, s_recv, owner):
    nt = len(ins)
    x, y, c, chips = _place()
    me = 2 * x + y
    ids = [2 * cx + cy for cx, cy in chips]
    pairs = [(t, k) for t in range(nt) for k in range(3)]

    def copy(t, k, slot):
        return pltpu.make_async_remote_copy(
            src_ref=ins[t].at[ids[k]], dst_ref=outs[t].at[slot], send_sem=s_send.at[3 * t + k],
            recv_sem=s_recv.at[3 * t + k], device_id=(*chips[k], owner), device_id_type=MESH)

    def start():
        for t, k in pairs:
            copy(t, k, me).start()

    def wait():
        for t, k in pairs:
            copy(t, k, ids[k]).wait()

    return c == owner, start, wait


def _carried_exchange(kind, ins, outs, sems, layer):
    if kind == "gather":
        return _gather_copies(ins, outs, sems, layer)
    mine, start, wait = _scatter_copies(ins, outs, *sems, layer)
    return mine, start, lambda: pl.when(mine)(wait)


def _carried_out_shapes(kind, arrays):
    if kind == "gather":
        return _gather_out_shapes(arrays)
    return [jax.ShapeDtypeStruct(a.shape, F32) for a in arrays]


def _carried_sems(kind, n):
    if kind is None:
        return []
    return _gather_sems(n) if kind == "gather" else [pltpu.SemaphoreType.DMA((3 * n,))] * 2


def _scatter_chip_sums(name, sums, owner):
    nt = len(sums)

    def body(*refs):
        mine, start, wait = _scatter_copies(refs[:nt], refs[nt:2 * nt], *refs[2 * nt:], owner)

        @pl.when(mine)
        def _():
            start()
            wait()

    return pl.pallas_call(
        body, name=name,
        out_shape=[jax.ShapeDtypeStruct(s.shape, F32) for s in sums],
        in_specs=[HBM_SPEC] * nt, out_specs=[HBM_SPEC] * nt,
        scratch_shapes=[pltpu.SemaphoreType.DMA((3 * nt,))] * 2,
    )(*sums)


def _swap_with_sibling(reduced):
    nt = len(reduced)

    def body(*refs):
        ins, outs = refs[:nt], refs[nt:2 * nt]
        s_send, s_recv = refs[2 * nt:]
        x, y, c, _ = _place()

        def copy(t):
            return pltpu.make_async_remote_copy(
                src_ref=ins[t], dst_ref=outs[t], send_sem=s_send.at[t], recv_sem=s_recv.at[t],
                device_id=(x, y, 1 - c), device_id_type=MESH)

        for t in range(nt):
            copy(t).start()
        for t in range(nt):
            copy(t).wait()

    return pl.pallas_call(
        body, name="swap_with_sibling",
        out_shape=[jax.ShapeDtypeStruct(r.shape, F32) for r in reduced],
        in_specs=[HBM_SPEC] * nt, out_specs=[HBM_SPEC] * nt,
        scratch_shapes=[pltpu.SemaphoreType.DMA((nt,))] * 2,
    )(*reduced)


def _gather_small(buf):
    def body(in_ref, out_ref, s_send, s_recv, l_sem):
        x, y, c, _ = _place()
        flips = [(fx, fy, fc) for fx in (0, 1) for fy in (0, 1) for fc in (0, 1)][1:]

        def peer(f):
            return tuple(1 - a if flip else a for a, flip in zip((x, y, c), f))

        def slot(p):
            return 4 * p[0] + 2 * p[1] + p[2]

        local = pltpu.make_async_copy(in_ref, out_ref.at[slot((x, y, c))], l_sem)
        local.start()

        def copy(k, owner):
            return pltpu.make_async_remote_copy(
                src_ref=in_ref, dst_ref=out_ref.at[slot(owner)], send_sem=s_send.at[k], recv_sem=s_recv.at[k],
                device_id=peer(flips[k]), device_id_type=MESH)

        for k in range(7):
            copy(k, (x, y, c)).start()
        for k in range(7):
            copy(k, peer(flips[k])).wait()
        local.wait()

    return pl.pallas_call(
        body, name="gather_small", out_shape=jax.ShapeDtypeStruct((8,) + buf.shape, F32),
        in_specs=[HBM_SPEC], out_specs=HBM_SPEC,
        scratch_shapes=[pltpu.SemaphoreType.DMA((7,))] * 2 + [pltpu.SemaphoreType.DMA],
    )(buf)


def _rows128(a):
    flat = a.reshape(-1)
    rows = -(-flat.shape[0] // LANES)
    rows8 = -(-rows // 8) * 8
    flat = jnp.pad(flat, (0, rows8 * LANES - flat.shape[0]))
    return flat.reshape(rows8, LANES)


def _pack(parts):
    return jnp.concatenate([_rows128(p) for p in parts], axis=0)


def _unpack(buf, shapes):
    out, r = [], 0
    for shp in shapes:
        n = int(np.prod(shp))
        rows8 = -(-(-(-n // LANES)) // 8) * 8
        out.append(buf[r:r + rows8].reshape(-1)[:n].reshape(shp))
        r += rows8
    return out


def kernel(x, a_norm_g, a_w_in, a_v_norm_g, a_w_s, a_b_s, a_w_out, b_norm_g, b_w_in, b_f_bias, b_q_norm_g, b_k_norm_g, b_w_out, loss_target, m_a_norm_g, m_a_w_in, m_a_v_norm_g, m_a_w_s, m_a_b_s, m_a_w_out, m_b_norm_g, m_b_w_in, m_b_f_bias, m_b_q_norm_g, m_b_k_norm_g, m_b_w_out, v_a_norm_g, v_a_w_in, v_a_v_norm_g, v_a_w_s, v_a_b_s, v_a_w_out, v_b_norm_g, v_b_w_in, v_b_f_bias, v_b_q_norm_g, v_b_k_norm_g, v_b_w_out):
    xs = x[0]
    target = loss_target[0]
    S, D = xs.shape
    n_layers = a_w_in.shape[0]
    assert n_layers == 2
    W = a_v_norm_g.shape[1]
    G = a_w_s.shape[1]
    H = b_f_bias.shape[1]
    HW = H * HEAD
    tq_fwd = _tile(S, 512)
    tq_bwd = _tile(S, 512)
    core = lax.axis_index("c")
    chip = 2 * lax.axis_index("x") + lax.axis_index("y")

    own = dict(a_w_in=a_w_in.astype(BF16), a_w_out=a_w_out.astype(BF16), b_w_in=b_w_in.astype(BF16),
               b_w_out=b_w_out.astype(BF16), b_norm_g=b_norm_g.reshape(n_layers, 1, -1))
    cb = b_w_in.shape[2]
    w_ain, w_aout, w_bmain, w_bf, w_bout, b_norm_full = ([None] * n_layers for _ in range(6))

    def blocks_of(tensors, layer):
        return [own[n][layer] for n in tensors]

    def take(tensors, layer, arrived):
        for n, got, mine in zip(tensors, arrived, blocks_of(tensors, layer)):
            full = lax.dynamic_update_slice(got, mine[None], (chip, 0, 0))
            if n == "a_w_in":
                w_ain[layer] = full
            elif n == "a_w_out":
                w_aout[layer] = full
            elif n == "b_w_out":
                w_bout[layer] = full
            elif n == "b_norm_g":
                b_norm_full[layer] = full.reshape(D)
            else:
                cols = jnp.transpose(full, (1, 0, 2)).reshape(D, N_CHIPS * cb)
                w_bmain[layer] = cols[:, :4 * HW]
                w_bf[layer] = jnp.pad(cols[:, 4 * HW:], ((0, 0), (0, LANES - H)))

    first_a = ("a_w_in", "a_w_out")
    first_b = ("b_w_in", "b_w_out", "b_norm_g")
    take(first_a, 0, _gather_weights("gather_a0", blocks_of(first_a, 0), 0))
    causal = jnp.tril(jnp.ones((CHUNK, CHUNK), dtype=bool))
    wc = jnp.where(causal[None, None], a_w_s, 0).astype(BF16)
    wc_t = jnp.swapaxes(wc, 2, 3)
    bs_t = jnp.swapaxes(a_b_s, 1, 2)
    f_bias = jnp.pad(b_f_bias, ((0, 0), (0, LANES - H))).reshape(n_layers, 1, LANES)

    def view_ain(l):
        return _View(w_ain[l], "col")

    def view_aout(l):
        return _View(w_aout[l], "row")

    def view_bout(l):
        return _View(w_bout[l], "row")

    saved = []
    cur = xs
    for i in range(2 * n_layers):
        l = i // 2
        if i % 2 == 0:
            h, h_t = _rmsnorm_fwd(f"a{l}_norm", cur, a_norm_g[l])
            if i == 0:
                p, arrived = _matmul(f"a{l}_in", _View(h), view_ain(l), out_dtype=BF16, tm=1024, tn=1024, tk=2048,
                                     carried=("gather", blocks_of(first_b, 0), 0))
                take(first_b, 0, arrived)
            else:
                p = _matmul(f"a{l}_in", _View(h), view_ain(l), out_dtype=BF16, tm=1024, tn=1024, tk=2048)
            y = _gate_fwd(f"a{l}_gate", p, a_v_norm_g[l], wc[l], bs_t[l])
            nxt = _matmul(f"a{l}_out", _View(y), view_aout(l), tm=1024, tn=1024, tk=1024, residual=cur)
            saved.append((cur, h_t, p, y))
        else:
            h, h_t = _rmsnorm_fwd(f"b{l}_norm", cur, b_norm_full[l])
            proj = _matmul(f"b{l}_in", _View(h), _View(w_bmain[l]), out_dtype=BF16, tm=1024, tn=1024, tk=2048)
            f = _matmul(f"b{l}_inf", _View(h), _View(w_bf[l]), tm=1024, tn=LANES, tk=2048)
            cum = _fox_cum(f"b{l}_cum", f, f_bias[l])
            qa, ka, va = _qkv_prep(f"b{l}_qkv", proj, cum, b_q_norm_g[l], b_k_norm_g[l], H)
            if i == 1:
                o, y, lse, arrived = _attn_fwd(f"b{l}_attn", qa, ka, va, proj, H, tq_fwd,
                                               carried=("gather", blocks_of(first_a + first_b, 1), 1))
                take(first_a + first_b, 1, arrived)
            else:
                o, y, lse, _ = _attn_fwd(f"b{l}_attn", qa, ka, va, proj, H, tq_fwd)
            nxt = _matmul(f"b{l}_out", _View(y), view_bout(l), tm=1024, tn=1024, tk=1024, residual=cur)
            saved.append((cur, h_t, proj, f, qa, ka, va, o, y, lse))
        cur = nxt

    g, gb, lcols = _loss_grad(cur, target)
    loss = lax.psum(0.5 * jnp.sum(lcols) / D, ("x", "y", "c"))

    big = {"a_w_in": [None] * n_layers, "a_w_out": [None] * n_layers,
           "b_w_in": [None] * n_layers, "b_w_out": [None] * n_layers}
    small = {k: [None] * n_layers for k in
             ("a_norm_g", "a_v_norm_g", "a_w_s", "a_b_s", "b_norm_g", "b_f_bias", "b_q_norm_g", "b_k_norm_g")}
    names = ["a_w_in", "a_w_out", "b_w_in", "b_w_out"]
    reduced = [{} for _ in range(n_layers)]

    def chip_sums_of(tag, tensors, layer):
        mine = [big[n][layer] for n in tensors]
        got = _send_to_owner(f"to_owner{tag}", mine, layer)
        active = (core == layer).astype(jnp.int32).reshape(1)
        sums = []
        for n, a, b in zip(tensors, mine, got):
            shp = a.shape
            flat = lambda t: t.reshape(shp[0] * shp[1], shp[2])
            sums.append(_add_if(f"chipsum{tag}_{n}", flat(a), flat(b), active).reshape(shp))
        return sums, active

    def reduce_slots(tag, tensors, layer, sums, got, active):
        for n, g_, s_ in zip(tensors, got, sums):
            slots = lax.dynamic_update_slice(g_, lax.dynamic_index_in_dim(s_, chip, keepdims=True), (chip, 0, 0))
            reduced[layer][n] = _sum_slots(f"reduce{tag}_{n}", slots, active)

    for i in reversed(range(2 * n_layers)):
        l = i // 2
        if i % 2 == 0:
            x_in, h_t, p, y = saved[i]
            dy = _matmul(f"a{l}_dy", _View(gb), view_aout(l), tb=True, out_dtype=BF16, tm=1024, tn=1024, tk=2048)
            d_wout = _matmul(f"a{l}_dwout", _View(y), _View(gb), ta=True, tm=2048, tn=1024, tk=1024)
            big["a_w_out"][l] = d_wout.reshape(N_CHIPS, W // N_CHIPS, D)
            dp, d_ws, d_bs, d_gv = _gate_bwd(f"a{l}_dgate", p, dy, a_v_norm_g[l], wc[l], wc_t[l], bs_t[l])
            dims_dh = dict(tb=True, tm=1024, tn=1024, tk=3072)
            dims_dwin = dict(tm=2048, tn=1024, tk=1024, out_colblocks=N_CHIPS)
            if i == 0:
                early = ["b_w_in", "b_w_out", "a_w_out"]
                sums_e, active0 = chip_sums_of("0e", early, 0)
                dh, got_b = _matmul(f"a{l}_dh", _View(dp), view_ain(l), carried=("scatter", sums_e[:2], 0), **dims_dh)
                d_win, got_a = _matmul(f"a{l}_dwin", _View(h_t), _View(dp), carried=("scatter", sums_e[2:], 0),
                                       **dims_dwin)
                reduce_slots("0e", early, 0, sums_e, got_b + got_a, active0)
            else:
                dh = _matmul(f"a{l}_dh", _View(dp), view_ain(l), **dims_dh)
                d_win = _matmul(f"a{l}_dwin", _View(h_t), _View(dp), **dims_dwin)
            g, gb, d_gn = _rmsnorm_bwd(f"a{l}_dnorm", x_in, a_norm_g[l], dh, g)
            big["a_w_in"][l] = d_win
            small["a_norm_g"][l] = d_gn.reshape(D)
            small["a_v_norm_g"][l] = d_gv.reshape(W)
            small["a_w_s"][l] = jnp.where(causal[None], d_ws, 0.0)
            small["a_b_s"][l] = d_bs[:, :G].T
        else:
            x_in, h_t, proj, f, qa, ka, va, o, y, lse = saved[i]
            dy = _matmul(f"b{l}_dy", _View(gb), view_bout(l), tb=True, out_dtype=BF16, tm=1024, tn=1024, tk=2048)
            d_wout = _matmul(f"b{l}_dwout", _View(y), _View(gb), ta=True, tm=2048, tn=1024, tk=1024)
            lse_lanes = jnp.pad(lse.reshape(H, S).T, ((0, 0), (0, LANES - H)))
            doa, dz, qab = _attn_bwd_prep(f"b{l}_dprep", dy, o, proj, qa, lse_lanes, H)
            early = (i == 1)
            if early:
                sums1, active1 = chip_sums_of("1", names, 1)
            dqa, dka, dv, got1 = _attn_bwd(f"b{l}_dattn", qab, doa, ka, va, H, tq_bwd,
                                           carried=(sums1, 1) if early else None)
            if early:
                reduce_slots("1", names, 1, sums1, got1, active1)
            dproj, d_gq, d_gk, dcum = _qk_bwd(f"b{l}_dqk", proj, dqa, dka, dv, dz, b_q_norm_g[l], b_k_norm_g[l], H)
            df, d_fb = _fox_cum_bwd(f"b{l}_dcum", dcum, f, f_bias[l])
            dh_f = _matmul(f"b{l}_dhf", _View(df), _View(w_bf[l]), tb=True, tm=1024, tn=1024, tk=LANES)
            dh = _matmul(f"b{l}_dh", _View(dproj), _View(w_bmain[l]), tb=True, tm=1024, tn=1024, tk=2048,
                         residual=dh_f)
            d_wmain = _matmul(f"b{l}_dwin", _View(h_t), _View(dproj), tm=2048, tn=1024, tk=1024)
            d_wf = _matmul(f"b{l}_dwinf", _View(h_t), _View(df), tm=2048, tn=LANES, tk=1024)
            d_win = jnp.concatenate([d_wmain, d_wf[:, :H]], axis=1)
            g, gb, d_gn = _rmsnorm_bwd(f"b{l}_dnorm", x_in, b_norm_full[l], dh, g)
            big["b_w_in"][l] = jnp.transpose(d_win.reshape(D, N_CHIPS, cb), (1, 0, 2))
            big["b_w_out"][l] = d_wout.reshape(N_CHIPS, HW // N_CHIPS, D)
            small["b_norm_g"][l] = d_gn.reshape(D)
            small["b_f_bias"][l] = d_fb[0, :H]
            small["b_q_norm_g"][l] = d_gq.reshape(HEAD)
            small["b_k_norm_g"][l] = d_gk.reshape(HEAD)
    grad_x = g[None]

    late = ["a_w_in"]
    sums_l, active0 = chip_sums_of("0l", late, 0)
    reduce_slots("0l", late, 0, sums_l, _scatter_chip_sums("scatter0", sums_l, 0), active0)
    mine = [jnp.where(core == 0, reduced[0][n], reduced[1][n]) for n in names]
    others = _swap_with_sibling(mine)
    grads = {n: jnp.where(core == 0, jnp.stack([m_, o_]), jnp.stack([o_, m_]))
             for n, m_, o_ in zip(names, mine, others)}

    small_names = ["a_norm_g", "a_v_norm_g", "a_w_s", "a_b_s", "b_norm_g", "b_f_bias", "b_q_norm_g", "b_k_norm_g"]
    small_parts = [jnp.stack(small[n]) for n in small_names]
    small_sum = _sum_slots("reduce_small", _gather_small(_pack(small_parts)), jnp.ones((1,), jnp.int32))
    for n, a in zip(small_names, _unpack(small_sum, [p.shape for p in small_parts])):
        grads[n] = a
    nb = b_norm_g.shape[1]
    grads["b_norm_g"] = lax.dynamic_slice_in_dim(grads["b_norm_g"], chip * nb, nb, axis=1)

    weights = dict(a_norm_g=a_norm_g, a_w_in=a_w_in, a_v_norm_g=a_v_norm_g, a_w_s=a_w_s, a_b_s=a_b_s,
                   a_w_out=a_w_out, b_norm_g=b_norm_g, b_w_in=b_w_in, b_f_bias=b_f_bias,
                   b_q_norm_g=b_q_norm_g, b_k_norm_g=b_k_norm_g, b_w_out=b_w_out)
    mom1 = dict(a_norm_g=m_a_norm_g, a_w_in=m_a_w_in, a_v_norm_g=m_a_v_norm_g, a_w_s=m_a_w_s, a_b_s=m_a_b_s,
                a_w_out=m_a_w_out, b_norm_g=m_b_norm_g, b_w_in=m_b_w_in, b_f_bias=m_b_f_bias,
                b_q_norm_g=m_b_q_norm_g, b_k_norm_g=m_b_k_norm_g, b_w_out=m_b_w_out)
    mom2 = dict(a_norm_g=v_a_norm_g, a_w_in=v_a_w_in, a_v_norm_g=v_a_v_norm_g, a_w_s=v_a_w_s, a_b_s=v_a_b_s,
                a_w_out=v_a_w_out, b_norm_g=v_b_norm_g, b_w_in=v_b_w_in, b_f_bias=v_b_f_bias,
                b_q_norm_g=v_b_q_norm_g, b_k_norm_g=v_b_k_norm_g, b_w_out=v_b_w_out)
    order = ["a_norm_g", "a_w_in", "a_v_norm_g", "a_w_s", "a_b_s", "a_w_out", "b_norm_g", "b_w_in", "b_f_bias",
             "b_q_norm_g", "b_k_norm_g", "b_w_out"]
    delta, new_m, new_v = {}, {}, {}
    for n in names:
        shp = weights[n].shape
        flat = lambda a: a.reshape(shp[0] * shp[1], shp[2])
        d, nm, nv = _adamw(f"adamw_{n}", flat(weights[n]), flat(grads[n]), flat(mom1[n]), flat(mom2[n]))
        delta[n], new_m[n], new_v[n] = d.reshape(shp), nm.reshape(shp), nv.reshape(shp)
    small_shapes = [weights[n].shape for n in small_names]
    pack_w, pack_g, pack_m, pack_v = (_pack([d[n] for n in small_names]) for d in (weights, grads, mom1, mom2))
    d, nm, nv = _adamw("adamw_small", pack_w, pack_g, pack_m, pack_v)
    for dst, buf in ((delta, d), (new_m, nm), (new_v, nv)):
        for n, a in zip(small_names, _unpack(buf, small_shapes)):
            dst[n] = a

    return (loss, grad_x, *[grads[n] for n in order], *[delta[n] for n in order],
            *[new_m[n] for n in order], *[new_v[n] for n in order])
```

```python
import functools
import math

import numpy as np
import jax
import jax.numpy as jnp
from jax import lax
from jax.experimental import pallas as pl
from jax.experimental.pallas import tpu as pltpu

F32 = jnp.float32
BF16 = jnp.bfloat16
MESH = pl.DeviceIdType.MESH

EPS = 1e-6
CHUNK = 128
HEAD = 128
LANES = 128
N_CHIPS = 4
VMEM_LIMIT = 56 * 1024 * 1024

ADAM_LR = 0.001
ADAM_B1 = 0.9
ADAM_B2 = 0.999
ADAM_EPS = 1e-08
ADAM_WD = 0.01
ADAM_STEP = 10

_NT = (((1,), (1,)), ((), ()))
_TN = (((0,), (0,)), ((), ()))
_GELU_C = math.sqrt(2.0 / math.pi)

HBM_SPEC = pl.BlockSpec(memory_space=pltpu.HBM)


def _params(*sem):
    return pltpu.CompilerParams(dimension_semantics=sem, vmem_limit_bytes=VMEM_LIMIT)


def _tile(dim, pref, unit=LANES):
    t = (min(pref, dim) // unit) * unit
    while t >= unit:
        if dim % t == 0:
            return t
        t -= unit
    return dim


def _gelu(x):
    return 0.5 * x * (1.0 + jnp.tanh(_GELU_C * (x + 0.044715 * (x * x * x))))


def _gelu_and_grad(x):
    x2 = x * x
    t = jnp.tanh(_GELU_C * (x + 0.044715 * (x2 * x)))
    val = 0.5 * x * (1.0 + t)
    grad = 0.5 * (1.0 + t) + 0.5 * x * (1.0 - t * t) * (_GELU_C * (1.0 + 3.0 * 0.044715 * x2))
    return val, grad


def _sigmoid(x):
    return 1.0 / (1.0 + jnp.exp(-x))


class _View:
    def __init__(self, arr, kind="2d", lead=()):
        self.arr, self.kind, self.lead = arr, kind, tuple(lead)
        shp = arr.shape[len(self.lead):]
        if kind == "2d":
            self.R, self.C = shp
        elif kind == "col":
            self.nb, self.R, self.cb = shp
            self.C = self.nb * self.cb
        else:
            self.nb, self.rb, self.C = shp
            self.R = self.nb * self.rb

    def fit(self, tr, tc):
        if self.kind == "col":
            tc = _tile(self.cb, tc)
        elif self.kind == "row":
            tr = _tile(self.rb, tr, unit=8)
        return tr, tc

    def spec(self, tr, tc, rc_of_grid):
        lead = self.lead
        sq = (None,) * len(lead)
        if self.kind == "2d":
            return pl.BlockSpec(sq + (tr, tc), lambda *g: lead + tuple(rc_of_grid(*g)))
        if self.kind == "col":
            q = self.cb // tc

            def im(*g):
                r, c = rc_of_grid(*g)
                return lead + (c // q, r, c % q)

            return pl.BlockSpec(sq + (None, tr, tc), im)
        q = self.rb // tr

        def im(*g):
            r, c = rc_of_grid(*g)
            return lead + (r // q, r % q, c)

        return pl.BlockSpec(sq + (None, tr, tc), im)


def _matmul(name, a, b, *, ta=False, tb=False, out_dtype=F32, tm=1024, tn=1024, tk=1024,
            out_colblocks=None, residual=None, carried=None):
    M, K = (a.C, a.R) if ta else (a.R, a.C)
    N, K2 = (b.R, b.C) if tb else (b.C, b.R)
    assert K == K2, (name, K, K2)
    tm, tn, tk = _tile(M, tm), _tile(N, tn), _tile(K, tk)
    if ta:
        tk, tm = a.fit(tk, tm)
    else:
        tm, tk = a.fit(tm, tk)
    if tb:
        tn, tk2 = b.fit(tn, tk)
    else:
        tk2, tn = b.fit(tk, tn)
    if tk2 != tk:
        tk = min(tk, tk2)
        if ta:
            tk, tm = a.fit(tk, tm)
        else:
            tm, tk = a.fit(tm, tk)
    if out_colblocks:
        tn = _tile(N // out_colblocks, tn)
    assert M % tm == 0 and N % tn == 0 and K % tk == 0, (name, M, N, K, tm, tn, tk)
    nk = K // tk
    assert nk == 1 or out_dtype == F32, name
    dims = (((0 if ta else 1,), (1 if tb else 0,)), ((), ()))
    grid = (M // tm, N // tn, nk)
    kind, g_blocks, g_layer = carried if carried else (None, [], None)
    ng = len(g_blocks)
    n_in = 2 + (residual is not None)

    def body(*refs):
        a_ref, b_ref = refs[:2]
        r_ref = refs[2] if residual is not None else None
        o_ref = refs[n_in + ng]
        k = pl.program_id(2)
        if carried:
            mine, start, finish = _carried_exchange(kind, refs[n_in:n_in + ng], refs[n_in + ng + 1:n_in + 2 * ng + 1],
                                                    refs[n_in + 2 * ng + 1:], g_layer)
            at = lambda step: functools.reduce(
                jnp.logical_and, [pl.program_id(d) == (0 if step == "first" else grid[d] - 1) for d in range(3)])

            @pl.when(jnp.logical_and(mine, at("first")))
            def _():
                start()

        def product():
            return lax.dot_general(a_ref[...], b_ref[...], dims, preferred_element_type=F32)

        if nk == 1:
            total = product()
            if r_ref is not None:
                total = total + r_ref[...]
            o_ref[...] = total.astype(out_dtype)
        else:
            @pl.when(k == 0)
            def _():
                o_ref[...] = product() + r_ref[...] if r_ref is not None else product()

            @pl.when(k > 0)
            def _():
                o_ref[...] += product()

        if carried:
            @pl.when(at("last"))
            def _():
                finish()

    a_spec = a.spec(tk, tm, lambda i, j, k: (k, i)) if ta else a.spec(tm, tk, lambda i, j, k: (i, k))
    b_spec = b.spec(tn, tk, lambda i, j, k: (j, k)) if tb else b.spec(tk, tn, lambda i, j, k: (k, j))
    in_specs, args = [a_spec, b_spec], [a.arr, b.arr]
    if residual is not None:
        in_specs.append(pl.BlockSpec((tm, tn), lambda i, j, k: (i, j)))
        args.append(residual)
    in_specs += [HBM_SPEC] * ng
    args += list(g_blocks)
    if out_colblocks:
        q = (N // out_colblocks) // tn
        out_shape = jax.ShapeDtypeStruct((out_colblocks, M, N // out_colblocks), out_dtype)
        out_spec = pl.BlockSpec((None, tm, tn), lambda i, j, k: (j // q, i, j % q))
    else:
        out_shape = jax.ShapeDtypeStruct((M, N), out_dtype)
        out_spec = pl.BlockSpec((tm, tn), lambda i, j, k: (i, j))
    out = pl.pallas_call(
        body, name=name, grid=grid, in_specs=in_specs, out_specs=[out_spec] + [HBM_SPEC] * ng,
        out_shape=[out_shape] + _carried_out_shapes(kind, g_blocks),
        scratch_shapes=_carried_sems(kind, ng),
        compiler_params=_params("arbitrary", "arbitrary", "arbitrary"),
    )(*args)
    return (out[0], list(out[1:])) if carried else out[0]


def _rmsnorm_fwd(name, x, gain):
    S, D = x.shape
    tr = _tile(S, 512)

    def body(x_ref, g_ref, h_ref, ht_ref):
        xv = x_ref[...]
        r = lax.rsqrt(jnp.mean(xv * xv, axis=-1, keepdims=True) + EPS)
        h = xv * r * g_ref[...]
        h_ref[...] = h.astype(BF16)
        ht_ref[...] = h.T.astype(BF16)

    return pl.pallas_call(
        body, name=name, grid=(S // tr,),
        in_specs=[pl.BlockSpec((tr, D), lambda i: (i, 0)), pl.BlockSpec((1, D), lambda i: (0, 0))],
        out_specs=[pl.BlockSpec((tr, D), lambda i: (i, 0)), pl.BlockSpec((D, tr), lambda i: (0, i))],
        out_shape=[jax.ShapeDtypeStruct((S, D), BF16), jax.ShapeDtypeStruct((D, S), BF16)],
        compiler_params=_params("arbitrary"),
    )(x, gain.reshape(1, D))


def _rmsnorm_bwd(name, x, gain, dh, g_res):
    S, D = x.shape
    tr = _tile(S, 256, unit=8)

    def body(x_ref, g_ref, dh_ref, res_ref, dx_ref, dxb_ref, dg_ref):
        i = pl.program_id(0)
        xv = x_ref[...]
        r = lax.rsqrt(jnp.mean(xv * xv, axis=-1, keepdims=True) + EPS)
        xhat = xv * r
        dhv = dh_ref[...]
        part = jnp.sum(dhv * xhat, axis=0, keepdims=True)

        @pl.when(i == 0)
        def _():
            dg_ref[...] = part

        @pl.when(i > 0)
        def _():
            dg_ref[...] += part

        dxhat = dhv * g_ref[...]
        dx = res_ref[...] + r * (dxhat - xhat * jnp.mean(dxhat * xhat, axis=-1, keepdims=True))
        dx_ref[...] = dx
        dxb_ref[...] = dx.astype(BF16)

    row = pl.BlockSpec((tr, D), lambda i: (i, 0))
    vec = pl.BlockSpec((1, D), lambda i: (0, 0))
    return pl.pallas_call(
        body, name=name, grid=(S // tr,), in_specs=[row, vec, row, row], out_specs=[row, row, vec],
        out_shape=[jax.ShapeDtypeStruct((S, D), F32), jax.ShapeDtypeStruct((S, D), BF16),
                   jax.ShapeDtypeStruct((1, D), F32)],
        compiler_params=_params("arbitrary"),
    )(x, gain.reshape(1, D), dh, g_res)


def _loss_grad(x, target):
    S, D = x.shape
    tr = _tile(S, 512, unit=8)

    def body(x_ref, t_ref, g_ref, gb_ref, l_ref):
        i = pl.program_id(0)
        e = x_ref[...] - t_ref[...]
        g = e * (1.0 / D)
        g_ref[...] = g
        gb_ref[...] = g.astype(BF16)
        part = jnp.sum(e * e, axis=0, keepdims=True)

        @pl.when(i == 0)
        def _():
            l_ref[...] = part

        @pl.when(i > 0)
        def _():
            l_ref[...] += part

    row = pl.BlockSpec((tr, D), lambda i: (i, 0))
    vec = pl.BlockSpec((1, D), lambda i: (0, 0))
    return pl.pallas_call(
        body, name="loss_grad", grid=(S // tr,), in_specs=[row, row], out_specs=[row, row, vec],
        out_shape=[jax.ShapeDtypeStruct((S, D), F32), jax.ShapeDtypeStruct((S, D), BF16),
                   jax.ShapeDtypeStruct((1, D), F32)],
        compiler_params=_params("arbitrary"),
    )(x, target)


def _gate_fwd(name, p, v_gain, wc, bs_t):
    S, W3 = p.shape
    W = W3 // 3
    G = wc.shape[0]
    gd = W // G

    def body(p_ref, gv_ref, wc_ref, bs_ref, y_ref):
        vg = _gelu(p_ref[:, W:2 * W].astype(F32))
        r = lax.rsqrt(jnp.mean(vg * vg, axis=-1, keepdims=True) + EPS)
        vb = (vg * r * gv_ref[...]).astype(BF16)
        zp = p_ref[:, 2 * W:].astype(F32)
        gate = _gelu(p_ref[:, :W].astype(F32)) * (zp * _sigmoid(zp))
        for g in range(G):
            sl = slice(g * gd, (g + 1) * gd)
            mixed = jnp.dot(wc_ref[g], vb[:, sl], preferred_element_type=F32) + bs_ref[:, g:g + 1]
            y_ref[:, sl] = (gate[:, sl] * mixed).astype(BF16)

    return pl.pallas_call(
        body, name=name, grid=(S // CHUNK,),
        in_specs=[pl.BlockSpec((CHUNK, W3), lambda i: (i, 0)), pl.BlockSpec((1, W), lambda i: (0, 0)),
                  pl.BlockSpec((G, CHUNK, CHUNK), lambda i: (0, 0, 0)), pl.BlockSpec((CHUNK, G), lambda i: (0, 0))],
        out_specs=pl.BlockSpec((CHUNK, W), lambda i: (i, 0)),
        out_shape=jax.ShapeDtypeStruct((S, W), BF16),
        compiler_params=_params("arbitrary"),
    )(p, v_gain.reshape(1, W), wc, bs_t)


def _gate_bwd(name, p, dy, v_gain, wc, wc_t, bs_t):
    S, W3 = p.shape
    W = W3 // 3
    G = wc.shape[0]
    gd = W // G

    def body(p_ref, dy_ref, gv_ref, wc_ref, wct_ref, bs_ref, dp_ref, dws_ref, dbs_ref, dgv_ref, dv_scr):
        i = pl.program_id(0)

        @pl.when(i == 0)
        def _():
            dws_ref[...] = jnp.zeros_like(dws_ref)
            dbs_ref[...] = jnp.zeros_like(dbs_ref)
            dgv_ref[...] = jnp.zeros_like(dgv_ref)

        gu, dgu = _gelu_and_grad(p_ref[:, :W].astype(F32))
        vg, dvg_dv = _gelu_and_grad(p_ref[:, W:2 * W].astype(F32))
        zp = p_ref[:, 2 * W:].astype(F32)
        sig = _sigmoid(zp)
        sz = zp * sig
        dsz = sig * (1.0 + zp * (1.0 - sig))
        r = lax.rsqrt(jnp.mean(vg * vg, axis=-1, keepdims=True) + EPS)
        vhat = vg * r
        gv = gv_ref[...]
        vb = (vhat * gv).astype(BF16)
        dy = dy_ref[...].astype(F32)
        lane = lax.broadcasted_iota(jnp.int32, (CHUNK, LANES), 1)
        dbs = jnp.zeros((CHUNK, LANES), F32)
        for g in range(G):
            sl = slice(g * gd, (g + 1) * gd)
            vsl = vb[:, sl]
            mixed = jnp.dot(wc_ref[g], vsl, preferred_element_type=F32) + bs_ref[:, g:g + 1]
            dyg, gug, szg = dy[:, sl], gu[:, sl], sz[:, sl]
            dp_ref[:, sl] = (dyg * mixed * szg * dgu[:, sl]).astype(BF16)
            dp_ref[:, 2 * W + g * gd:2 * W + (g + 1) * gd] = (dyg * gug * mixed * dsz[:, sl]).astype(BF16)
            dm = dyg * gug * szg
            dmb = dm.astype(BF16)
            dws_ref[g] += lax.dot_general(dmb, vsl, _NT, preferred_element_type=F32)
            dbs = dbs + jnp.where(lane == g, jnp.sum(dm, axis=1, keepdims=True), 0.0)
            dv_scr[:, sl] = jnp.dot(wct_ref[g], dmb, preferred_element_type=F32)
        dbs_ref[...] += dbs
        dv = dv_scr[...]
        dgv_ref[...] += jnp.sum(dv * vhat, axis=0, keepdims=True)
        dvhat = dv * gv
        dvg = r * (dvhat - vhat * jnp.mean(dvhat * vhat, axis=-1, keepdims=True))
        dp_ref[:, W:2 * W] = (dvg * dvg_dv).astype(BF16)

    return pl.pallas_call(
        body, name=name, grid=(S // CHUNK,),
        in_specs=[pl.BlockSpec((CHUNK, W3), lambda i: (i, 0)), pl.BlockSpec((CHUNK, W), lambda i: (i, 0)),
                  pl.BlockSpec((1, W), lambda i: (0, 0)),
                  pl.BlockSpec((G, CHUNK, CHUNK), lambda i: (0, 0, 0)),
                  pl.BlockSpec((G, CHUNK, CHUNK), lambda i: (0, 0, 0)),
                  pl.BlockSpec((CHUNK, G), lambda i: (0, 0))],
        out_specs=[pl.BlockSpec((CHUNK, W3), lambda i: (i, 0)),
                   pl.BlockSpec((G, CHUNK, CHUNK), lambda i: (0, 0, 0)),
                   pl.BlockSpec((CHUNK, LANES), lambda i: (0, 0)),
                   pl.BlockSpec((1, W), lambda i: (0, 0))],
        out_shape=[jax.ShapeDtypeStruct((S, W3), BF16), jax.ShapeDtypeStruct((G, CHUNK, CHUNK), F32),
                   jax.ShapeDtypeStruct((CHUNK, LANES), F32), jax.ShapeDtypeStruct((1, W), F32)],
        scratch_shapes=[pltpu.VMEM((CHUNK, W), F32)],
        compiler_params=_params("arbitrary"),
    )(p, dy, v_gain.reshape(1, W), wc, wc_t, bs_t)


AUG = 2 * HEAD
LOG2E = 1.0 / math.log(2.0)
Q_SUM_LANE = HEAD + 3
K_SUM_LANE = HEAD


def _pieces(x, sign=1.0):
    hi, mid, lo = _split3(sign * x)
    return hi.astype(F32), mid.astype(F32), lo.astype(F32)


def _lanes(lane, start, vals, rest):
    out = rest
    for n, v in enumerate(vals):
        out = jnp.where(lane == start + n, v, out)
    return out


def _qkv_prep(name, proj, cum, q_gain, k_gain, H):
    S = proj.shape[0]
    HW = H * HEAD
    tr = _tile(S, 256, unit=8)
    sigma = (HEAD ** -0.5) * LOG2E

    def body(q_ref, k_ref, v_ref, c_ref, gq_ref, gk_ref, qa_ref, ka_ref, va_ref):
        lane = lax.broadcasted_iota(jnp.int32, (tr, HEAD), 1)
        zero = jnp.zeros((tr, HEAD), F32)
        v_aug = jnp.where(lane < 3, 1.0, zero).astype(BF16)
        for h in range(H):
            sl = slice(h * HEAD, (h + 1) * HEAD)
            a0 = h * AUG
            t = q_ref[:, sl].astype(F32)
            r = lax.rsqrt(jnp.mean(t * t, axis=-1, keepdims=True) + EPS)
            qa_ref[:, a0:a0 + HEAD] = (t * r * gq_ref[...] * sigma).astype(BF16)
            t = k_ref[:, sl].astype(F32)
            r = lax.rsqrt(jnp.mean(t * t, axis=-1, keepdims=True) + EPS)
            ka_ref[:, a0:a0 + HEAD] = (t * r * gk_ref[...]).astype(BF16)
            va_ref[:, a0:a0 + HEAD] = v_ref[:, sl]
            va_ref[:, a0 + HEAD:a0 + AUG] = v_aug
            c2 = c_ref[:, h:h + 1] * LOG2E
            qa_ref[:, a0 + HEAD:a0 + AUG] = _lanes(lane, 0, _pieces(c2) + (1.0, 1.0, 1.0), zero).astype(BF16)
            ka_ref[:, a0 + HEAD:a0 + AUG] = _lanes(
                lane, 0, (1.0, 1.0, 1.0) + _pieces(c2, -1.0) + (1.0, 1.0, 1.0), zero).astype(BF16)

    col = lambda c: pl.BlockSpec((tr, HW), lambda i: (i, c))
    vec = pl.BlockSpec((1, HEAD), lambda i: (0, 0))
    aug = pl.BlockSpec((tr, H * AUG), lambda i: (i, 0))
    return pl.pallas_call(
        body, name=name, grid=(S // tr,),
        in_specs=[col(0), col(1), col(2), pl.BlockSpec((tr, LANES), lambda i: (i, 0)), vec, vec],
        out_specs=[aug] * 3, out_shape=[jax.ShapeDtypeStruct((S, H * AUG), BF16)] * 3,
        compiler_params=_params("arbitrary"),
    )(proj, proj, proj, cum, q_gain.reshape(1, HEAD), k_gain.reshape(1, HEAD))


def _split3(x):
    hi = x.astype(BF16)
    r1 = x - hi.astype(F32)
    mid = r1.astype(BF16)
    lo = (r1 - mid.astype(F32)).astype(BF16)
    return hi, mid, lo


def _tri_sum(tri, x):
    hi, mid, lo = _split3(x)
    d = lambda t: jnp.dot(tri, t, preferred_element_type=F32)
    return d(hi) + (d(mid) + d(lo))


def _log_sigmoid(x):
    return jnp.minimum(x, 0.0) - jnp.log(1.0 + jnp.exp(-jnp.abs(x)))


def _fox_cum(name, f, bias):
    S = f.shape[0]
    tb = _tile(S, 256, unit=8)

    def body(f_ref, b_ref, c_ref):
        rr = lax.broadcasted_iota(jnp.int32, (tb, tb), 0)
        cc = lax.broadcasted_iota(jnp.int32, (tb, tb), 1)
        tri = (rr >= cc).astype(BF16)

        def step(t, carry):
            off = pl.multiple_of(t * tb, tb)
            lf = _log_sigmoid(f_ref[pl.ds(off, tb), :] + b_ref[...])
            c = _tri_sum(tri, lf) + carry
            c_ref[pl.ds(off, tb), :] = c
            return c[tb - 1:tb, :]

        lax.fori_loop(0, S // tb, step, jnp.zeros((1, LANES), F32))

    return pl.pallas_call(
        body, name=name, out_shape=jax.ShapeDtypeStruct((S, LANES), F32),
        in_specs=[pl.BlockSpec(memory_space=pltpu.VMEM)] * 2, out_specs=pl.BlockSpec(memory_space=pltpu.VMEM),
        compiler_params=pltpu.CompilerParams(vmem_limit_bytes=VMEM_LIMIT),
    )(f, bias)


def _fox_cum_bwd(name, dcum, f, bias):
    S = f.shape[0]
    tb = _tile(S, 256, unit=8)
    nb = S // tb

    def body(dc_ref, f_ref, b_ref, df_ref, db_ref):
        rr = lax.broadcasted_iota(jnp.int32, (tb, tb), 0)
        cc = lax.broadcasted_iota(jnp.int32, (tb, tb), 1)
        tri = (rr <= cc).astype(BF16)

        def step(t, carry):
            tail, dbias = carry
            off = pl.multiple_of((nb - 1 - t) * tb, tb)
            dlf = _tri_sum(tri, dc_ref[pl.ds(off, tb), :]) + tail
            d = dlf * _sigmoid(-(f_ref[pl.ds(off, tb), :] + b_ref[...]))
            df_ref[pl.ds(off, tb), :] = d.astype(BF16)
            return dlf[0:1, :], dbias + jnp.sum(d, axis=0, keepdims=True)

        z = jnp.zeros((1, LANES), F32)
        _, dbias = lax.fori_loop(0, nb, step, (z, z))
        db_ref[...] = dbias

    vm = pl.BlockSpec(memory_space=pltpu.VMEM)
    return pl.pallas_call(
        body, name=name, out_shape=[jax.ShapeDtypeStruct((S, LANES), BF16), jax.ShapeDtypeStruct((1, LANES), F32)],
        in_specs=[vm] * 3, out_specs=[vm] * 2,
        compiler_params=pltpu.CompilerParams(vmem_limit_bytes=VMEM_LIMIT),
    )(dcum, f, bias)


def _attn_fwd(name, qa, ka, va, proj, H, tq, carried=None):
    S = qa.shape[0]
    HW = H * HEAD
    nq = S // tq
    hp = 2 if H % 2 == 0 else 1
    tw = 2 * tq if S % (2 * tq) == 0 else tq
    kind, g_blocks, g_layer = carried if carried else (None, [], None)
    ng = len(g_blocks)

    def body(*refs):
        q_ref, k_ref, v_ref, z_ref = refs[:4]
        o_ref, y_ref, lse_ref = refs[4 + ng:7 + ng]
        i = pl.program_id(1)
        if carried:
            mine, start, finish = _carried_exchange(kind, refs[4:4 + ng], refs[7 + ng:7 + 2 * ng],
                                                    refs[7 + 2 * ng:], g_layer)
            hd = pl.program_id(0)

            @pl.when(jnp.logical_and(mine, jnp.logical_and(hd == 0, i == 0)))
            def _():
                start()

        def step(j, carry, masked):
            off = pl.multiple_of(j * tw, tw)
            out = []
            for n in range(hp):
                m, acc = carry[n]
                a = slice(n * AUG, (n + 1) * AUG)
                s = lax.dot_general(q_ref[:, a], k_ref[pl.ds(off, tw), a], _NT, preferred_element_type=F32)
                if masked:
                    qry = i * tq + lax.broadcasted_iota(jnp.int32, (tq, tw), 0)
                    key = j * tw + lax.broadcasted_iota(jnp.int32, (tq, tw), 1)
                    s = jnp.where(qry >= key, s, -jnp.inf)
                m_new = jnp.maximum(m, jnp.max(s, axis=1, keepdims=True))
                pr = jnp.exp2(s - m_new).astype(BF16)
                acc = jnp.exp2(m - m_new) * acc + jnp.dot(pr, v_ref[pl.ds(off, tw), a], preferred_element_type=F32)
                out.append((m_new, acc))
            return tuple(out)

        init = ((jnp.full((tq, 1), -jnp.inf, F32), jnp.zeros((tq, AUG), F32)),) * hp
        below = (i * tq) // tw
        carry = lax.fori_loop(0, below, lambda j, c: step(j, c, False), init)
        carry = step(below, carry, True)
        for n in range(hp):
            m, acc = carry[n]
            sl = slice(n * HEAD, (n + 1) * HEAD)
            l = acc[:, HEAD:HEAD + 1]
            o = acc[:, :HEAD] / l
            z = z_ref[:, sl].astype(F32)
            o_ref[:, sl] = o
            y_ref[:, sl] = (o * (z * _sigmoid(z))).astype(BF16)
            lse_ref[n] = m + jnp.log(l) * LOG2E

        if carried:
            @pl.when(jnp.logical_and(hd == H // hp - 1, i == nq - 1))
            def _():
                finish()

    qspec = pl.BlockSpec((tq, hp * AUG), lambda h, i: (i, h))
    kvspec = pl.BlockSpec((S, hp * AUG), lambda h, i: (0, h))
    ospec = pl.BlockSpec((tq, hp * HEAD), lambda h, i: (i, h))
    out = pl.pallas_call(
        body, name=name, grid=(H // hp, nq),
        in_specs=[qspec, kvspec, kvspec, pl.BlockSpec((tq, hp * HEAD), lambda h, i: (i, 3 * H // hp + h))]
        + [HBM_SPEC] * ng,
        out_specs=[ospec, ospec, pl.BlockSpec((hp, tq, 1), lambda h, i: (h, i, 0))] + [HBM_SPEC] * ng,
        out_shape=[jax.ShapeDtypeStruct((S, HW), F32), jax.ShapeDtypeStruct((S, HW), BF16),
                   jax.ShapeDtypeStruct((H, S, 1), F32)] + _carried_out_shapes(kind, g_blocks),
        scratch_shapes=_carried_sems(kind, ng),
        compiler_params=_params("arbitrary", "arbitrary"),
    )(qa, ka, va, proj, *g_blocks)
    return out[0], out[1], out[2], list(out[3:])


def _attn_bwd_prep(name, dy, o, proj, qa, lse, H):
    S, HW = o.shape
    tr = _tile(S, 256, unit=8)

    def body(dy_ref, o_ref, z_ref, qa_ref, lse_ref, doa_ref, dz_ref, qab_ref):
        lane = lax.broadcasted_iota(jnp.int32, (tr, HEAD), 1)
        zero = jnp.zeros((tr, HEAD), F32)
        for h in range(H):
            sl = slice(h * HEAD, (h + 1) * HEAD)
            a0 = h * AUG
            dy = dy_ref[:, sl].astype(F32)
            z = z_ref[:, sl].astype(F32)
            o = o_ref[:, sl]
            sig = _sigmoid(z)
            dob = (dy * (z * sig)).astype(BF16)
            dz_ref[:, sl] = (dy * o * (sig * (1.0 + z * (1.0 - sig)))).astype(BF16)
            delta = jnp.sum(dob.astype(F32) * o, axis=1, keepdims=True)
            doa_ref[:, a0:a0 + HEAD] = dob
            doa_ref[:, a0 + HEAD:a0 + AUG] = _lanes(lane, 0, _pieces(delta, -1.0), zero).astype(BF16)
            qab_ref[:, a0:a0 + HEAD] = qa_ref[:, a0:a0 + HEAD]
            qab_ref[:, a0 + HEAD:a0 + AUG] = _lanes(
                lane, 6, _pieces(lse_ref[:, h:h + 1], -1.0), qa_ref[:, a0 + HEAD:a0 + AUG].astype(F32)).astype(BF16)

    row = pl.BlockSpec((tr, HW), lambda i: (i, 0))
    aug = pl.BlockSpec((tr, H * AUG), lambda i: (i, 0))
    return pl.pallas_call(
        body, name=name, grid=(S // tr,),
        in_specs=[row, row, pl.BlockSpec((tr, HW), lambda i: (i, 3)), aug, pl.BlockSpec((tr, LANES), lambda i: (i, 0))],
        out_specs=[aug, row, aug],
        out_shape=[jax.ShapeDtypeStruct((S, H * AUG), BF16), jax.ShapeDtypeStruct((S, HW), BF16),
                   jax.ShapeDtypeStruct((S, H * AUG), BF16)],
        compiler_params=_params("arbitrary"),
    )(dy, o, proj, qa, lse)


def _attn_bwd(name, qab, doa, ka, va, H, tq, carried=None):
    S = qab.shape[0]
    nq = S // tq
    tw = 2 * tq if S % (2 * tq) == 0 else tq
    sums, owner = carried if carried else ([], None)
    nt = len(sums)

    def body(*refs):
        q_ref, do_ref, k_ref, v_ref = refs[:4]
        dq_ref, dk_ref, dv_ref = refs[4 + nt:7 + nt]
        j = pl.program_id(1)
        if carried:
            mine, start, wait = _scatter_copies(refs[4:4 + nt], refs[7 + nt:7 + 2 * nt], *refs[7 + 2 * nt:], owner)
            hd = pl.program_id(0)

            @pl.when(jnp.logical_and(mine, jnp.logical_and(hd == 0, j == 0)))
            def _():
                start()

        @pl.when(j == 0)
        def _():
            dq_ref[...] = jnp.zeros_like(dq_ref)

        k = k_ref[...]
        v = v_ref[...]

        def step(i, carry, masked):
            dk_acc, dv_acc = carry
            off = pl.multiple_of(i * tw, tw)
            q = q_ref[pl.ds(off, tw), :]
            do = do_ref[pl.ds(off, tw), :]
            st = lax.dot_general(k, q, _NT, preferred_element_type=F32)
            if masked:
                key = j * tq + lax.broadcasted_iota(jnp.int32, (tq, tw), 0)
                qry = i * tw + lax.broadcasted_iota(jnp.int32, (tq, tw), 1)
                st = jnp.where(qry >= key, st, -jnp.inf)
            pt = jnp.exp2(st)
            dst = pt * lax.dot_general(v, do, _NT, preferred_element_type=F32)
            dsb = dst.astype(BF16)
            dv_acc = dv_acc + jnp.dot(pt.astype(BF16), do[:, :HEAD], preferred_element_type=F32)
            dk_acc = dk_acc + jnp.dot(dsb, q, preferred_element_type=F32)
            dq_ref[pl.ds(off, tw), :] += lax.dot_general(dsb, k, _TN, preferred_element_type=F32)
            return dk_acc, dv_acc

        first = (j * tq) // tw
        carry = step(first, (jnp.zeros((tq, AUG), F32), jnp.zeros((tq, HEAD), F32)), True)
        dk_acc, dv_acc = lax.fori_loop(first + 1, S // tw, lambda i, c: step(i, c, False), carry)
        dk_ref[...] = dk_acc
        dv_ref[...] = dv_acc
        if carried:
            @pl.when(jnp.logical_and(mine, jnp.logical_and(hd == H - 1, j == nq - 1)))
            def _():
                wait()

    full = pl.BlockSpec((S, AUG), lambda h, j: (0, h))
    blk = pl.BlockSpec((tq, AUG), lambda h, j: (j, h))
    out = pl.pallas_call(
        body, name=name, grid=(H, nq),
        in_specs=[full, full, blk, blk] + [HBM_SPEC] * nt,
        out_specs=[full, blk, pl.BlockSpec((tq, HEAD), lambda h, j: (j, h))] + [HBM_SPEC] * nt,
        out_shape=[jax.ShapeDtypeStruct((S, H * AUG), F32), jax.ShapeDtypeStruct((S, H * AUG), F32),
                   jax.ShapeDtypeStruct((S, H * HEAD), F32)] + [jax.ShapeDtypeStruct(s.shape, F32) for s in sums],
        scratch_shapes=[pltpu.SemaphoreType.DMA((3 * nt,))] * 2 if carried else [],
        compiler_params=_params("arbitrary", "arbitrary"),
    )(qab, doa, ka, va, *sums)
    return out[0], out[1], out[2], list(out[3:])


def _qk_bwd(name, proj, dqa, dka, dv, dz, q_gain, k_gain, H):
    S = proj.shape[0]
    HW = H * HEAD
    tr = _tile(S, 256, unit=8)
    scale = HEAD ** -0.5
    factors = (scale, 1.0 / LOG2E)

    def body(q_ref, k_ref, dq_ref, dk_ref, dv_ref, dz_ref, gq_ref, gk_ref, dp_ref, dgq_ref, dgk_ref, dc_ref):
        i = pl.program_id(0)

        @pl.when(i == 0)
        def _():
            dgq_ref[...] = jnp.zeros_like(dgq_ref)
            dgk_ref[...] = jnp.zeros_like(dgk_ref)

        for n, (src, dsrc, gain, dgain) in enumerate(((q_ref, dq_ref, gq_ref, dgq_ref), (k_ref, dk_ref, gk_ref, dgk_ref))):
            acc = jnp.zeros((1, HEAD), F32)
            for h in range(H):
                sl = slice(h * HEAD, (h + 1) * HEAD)
                t = src[:, sl].astype(F32)
                r = lax.rsqrt(jnp.mean(t * t, axis=-1, keepdims=True) + EPS)
                that = t * r
                dn = dsrc[:, h * AUG:h * AUG + HEAD] * factors[n]
                acc = acc + jnp.sum(dn * that, axis=0, keepdims=True)
                dhat = dn * gain[...]
                dt = r * (dhat - that * jnp.mean(dhat * that, axis=-1, keepdims=True))
                dp_ref[:, n * HW + h * HEAD:n * HW + (h + 1) * HEAD] = dt.astype(BF16)
            dgain[...] += acc
        dp_ref[:, 2 * HW:3 * HW] = dv_ref[...].astype(BF16)
        dp_ref[:, 3 * HW:] = dz_ref[...]
        lane = lax.broadcasted_iota(jnp.int32, (tr, LANES), 1)
        dc = jnp.zeros((tr, LANES), F32)
        for h in range(H):
            qs = dq_ref[:, h * AUG + K_SUM_LANE:h * AUG + K_SUM_LANE + 1]
            ks = dk_ref[:, h * AUG + Q_SUM_LANE:h * AUG + Q_SUM_LANE + 1]
            dc = jnp.where(lane == h, qs - ks, dc)
        dc_ref[...] = dc

    col = lambda c: pl.BlockSpec((tr, HW), lambda i: (i, c))
    row = col(0)
    aug = pl.BlockSpec((tr, H * AUG), lambda i: (i, 0))
    vec = pl.BlockSpec((1, HEAD), lambda i: (0, 0))
    return pl.pallas_call(
        body, name=name, grid=(S // tr,),
        in_specs=[col(0), col(1), aug, aug, row, row, vec, vec],
        out_specs=[pl.BlockSpec((tr, 4 * HW), lambda i: (i, 0)), vec, vec, pl.BlockSpec((tr, LANES), lambda i: (i, 0))],
        out_shape=[jax.ShapeDtypeStruct((S, 4 * HW), BF16), jax.ShapeDtypeStruct((1, HEAD), F32),
                   jax.ShapeDtypeStruct((1, HEAD), F32), jax.ShapeDtypeStruct((S, LANES), F32)],
        compiler_params=_params("arbitrary"),
    )(proj, proj, dqa, dka, dv, dz, q_gain.reshape(1, HEAD), k_gain.reshape(1, HEAD))


def _row_tile(R, C, budget_bytes=1 << 20):
    cap = max(8, budget_bytes // (4 * C))
    t = (min(cap, R) // 8) * 8
    while t >= 8:
        if R % t == 0:
            return t
        t -= 8
    return R


def _add_if(name, a, b, active):
    R, C = a.shape
    tr = _row_tile(R, C)

    def body(act_ref, a_ref, b_ref, o_ref):
        @pl.when(act_ref[0] != 0)
        def _():
            o_ref[...] = a_ref[...] + b_ref[...]

    blk = pl.BlockSpec((tr, C), lambda i, s: (i * s[0], 0))
    grid_spec = pltpu.PrefetchScalarGridSpec(
        num_scalar_prefetch=1, grid=(R // tr,), in_specs=[blk, blk], out_specs=blk)
    return pl.pallas_call(
        body, name=name, grid_spec=grid_spec, out_shape=jax.ShapeDtypeStruct((R, C), F32),
        compiler_params=_params("arbitrary"),
    )(active, a, b)


def _sum_slots(name, slots, active):
    n, R, C = slots.shape
    tr = _row_tile(R, C, budget_bytes=(1 << 20) // 2)

    def body(act_ref, s_ref, o_ref):
        @pl.when(act_ref[0] != 0)
        def _():
            acc = s_ref[0]
            for k in range(1, n):
                acc = acc + s_ref[k]
            o_ref[...] = acc

    grid_spec = pltpu.PrefetchScalarGridSpec(
        num_scalar_prefetch=1, grid=(R // tr,),
        in_specs=[pl.BlockSpec((n, tr, C), lambda i, s: (0, i * s[0], 0))],
        out_specs=pl.BlockSpec((tr, C), lambda i, s: (i * s[0], 0)))
    return pl.pallas_call(
        body, name=name, grid_spec=grid_spec, out_shape=jax.ShapeDtypeStruct((R, C), F32),
        compiler_params=_params("arbitrary"),
    )(active, slots)


def _adamw(name, w, g, m, v):
    R, C = w.shape
    tr = _row_tile(R, C, budget_bytes=(1 << 20) // 2)
    c1 = 1.0 - ADAM_B1 ** ADAM_STEP
    c2 = 1.0 - ADAM_B2 ** ADAM_STEP

    def body(w_ref, g_ref, m_ref, v_ref, d_ref, nm_ref, nv_ref):
        gv = g_ref[...]
        nm = ADAM_B1 * m_ref[...] + (1.0 - ADAM_B1) * gv
        nv = ADAM_B2 * v_ref[...] + (1.0 - ADAM_B2) * (gv * gv)
        m_hat = nm / c1
        v_hat = nv / c2
        d_ref[...] = -ADAM_LR * (m_hat / (jnp.sqrt(v_hat) + ADAM_EPS) + ADAM_WD * w_ref[...])
        nm_ref[...] = nm
        nv_ref[...] = nv

    blk = pl.BlockSpec((tr, C), lambda i: (i, 0))
    return pl.pallas_call(
        body, name=name, grid=(R // tr,), in_specs=[blk] * 4, out_specs=[blk] * 3,
        out_shape=[jax.ShapeDtypeStruct((R, C), F32)] * 3,
        compiler_params=_params("arbitrary"),
    )(w, g, m, v)


def _place():
    x, y, c = lax.axis_index("x"), lax.axis_index("y"), lax.axis_index("c")
    chips = [(1 - x, y), (x, 1 - y), (1 - x, 1 - y)]
    return x, y, c, chips


def _gather_copies(ins, outs, sems, layer):
    s_send, s_recv, f_send, f_recv = sems
    nt = len(ins)
    x, y, c, chips = _place()
    me = 2 * x + y
    ids = [2 * cx + cy for cx, cy in chips]
    pairs = [(t, k) for t in range(nt) for k in range(3)]

    def over_ici(t, k, block):
        return pltpu.make_async_remote_copy(
            src_ref=ins[t], dst_ref=outs[t].at[block], send_sem=s_send.at[3 * t + k],
            recv_sem=s_recv.at[3 * t + k], device_id=(*chips[k], layer), device_id_type=MESH)

    def over_d2d(t, k):
        blk = outs[t].at[ids[k]]
        return pltpu.make_async_remote_copy(
            src_ref=blk, dst_ref=blk, send_sem=f_send.at[3 * t + k], recv_sem=f_recv.at[3 * t + k],
            device_id=(x, y, 1 - c), device_id_type=MESH)

    def start():
        for t, k in pairs:
            over_ici(t, k, me).start()

    def finish():
        @pl.when(c == layer)
        def _():
            for t, k in pairs:
                over_ici(t, k, ids[k]).wait_recv()
                over_d2d(t, k).start()
            for t, k in pairs:
                over_ici(t, k, me).wait_send()
                over_d2d(t, k).wait_send()

        @pl.when(c != layer)
        def _():
            for t, k in pairs:
                over_d2d(t, k).wait_recv()

    return c == layer, start, finish


def _gather_out_shapes(blocks):
    return [jax.ShapeDtypeStruct((N_CHIPS,) + b.shape, b.dtype) for b in blocks]


def _gather_sems(n):
    return [pltpu.SemaphoreType.DMA((3 * n,))] * 4


def _gather_weights(name, blocks, layer):
    nt = len(blocks)

    def body(*refs):
        mine, start, finish = _gather_copies(refs[:nt], refs[nt:2 * nt], refs[2 * nt:], layer)

        @pl.when(mine)
        def _():
            start()

        finish()

    return pl.pallas_call(
        body, name=name, out_shape=_gather_out_shapes(blocks),
        in_specs=[HBM_SPEC] * nt, out_specs=[HBM_SPEC] * nt, scratch_shapes=_gather_sems(nt),
    )(*blocks)


def _send_to_owner(name, grads, owner):
    nt = len(grads)

    def body(*refs):
        ins, outs = refs[:nt], refs[nt:2 * nt]
        s_send, s_recv = refs[2 * nt:]
        x, y, c, _ = _place()

        def copy(t):
            return pltpu.make_async_remote_copy(
                src_ref=ins[t], dst_ref=outs[t], send_sem=s_send.at[t], recv_sem=s_recv.at[t],
                device_id=(x, y, owner), device_id_type=MESH)

        @pl.when(c != owner)
        def _():
            for t in range(nt):
                copy(t).start()
            for t in range(nt):
                copy(t).wait_send()

        @pl.when(c == owner)
        def _():
            for t in range(nt):
                copy(t).wait_recv()

    return pl.pallas_call(
        body, name=name,
        out_shape=[jax.ShapeDtypeStruct(g.shape, F32) for g in grads],
        in_specs=[HBM_SPEC] * nt, out_specs=[HBM_SPEC] * nt,
        scratch_shapes=[pltpu.SemaphoreType.DMA((nt,))] * 2,
    )(*grads)


def _scatter_copies(ins, outs, s_send, s_recv, owner):
    nt = len(ins)
    x, y, c, chips = _place()
    me = 2 * x + y
    ids = [2 * cx + cy for cx, cy in chips]
    pairs = [(t, k) for t in range(nt) for k in range(3)]

    def copy(t, k, slot):
        return pltpu.make_async_remote_copy(
            src_ref=ins[t].at[ids[k]], dst_ref=outs[t].at[slot], send_sem=s_send.at[3 * t + k],
            recv_sem=s_recv.at[3 * t + k], device_id=(*chips[k], owner), device_id_type=MESH)

    def start():
        for t, k in pairs:
            copy(t, k, me).start()

    def wait():
        for t, k in pairs:
            copy(t, k, ids[k]).wait()

    return c == owner, start, wait


def _carried_exchange(kind, ins, outs, sems, layer):
    if kind == "gather":
        return _gather_copies(ins, outs, sems, layer)
    mine, start, wait = _scatter_copies(ins, outs, *sems, layer)
    return mine, start, lambda: pl.when(mine)(wait)


def _carried_out_shapes(kind, arrays):
    if kind == "gather":
        return _gather_out_shapes(arrays)
    return [jax.ShapeDtypeStruct(a.shape, F32) for a in arrays]


def _carried_sems(kind, n):
    if kind is None:
        return []
    return _gather_sems(n) if kind == "gather" else [pltpu.SemaphoreType.DMA((3 * n,))] * 2


def _scatter_chip_sums(name, sums, owner):
    nt = len(sums)

    def body(*refs):
        mine, start, wait = _scatter_copies(refs[:nt], refs[nt:2 * nt], *refs[2 * nt:], owner)

        @pl.when(mine)
        def _():
            start()
            wait()

    return pl.pallas_call(
        body, name=name,
        out_shape=[jax.ShapeDtypeStruct(s.shape, F32) for s in sums],
        in_specs=[HBM_SPEC] * nt, out_specs=[HBM_SPEC] * nt,
        scratch_shapes=[pltpu.SemaphoreType.DMA((3 * nt,))] * 2,
    )(*sums)


def _swap_with_sibling(reduced):
    nt = len(reduced)

    def body(*refs):
        ins, outs = refs[:nt], refs[nt:2 * nt]
        s_send, s_recv = refs[2 * nt:]
        x, y, c, _ = _place()

        def copy(t):
            return pltpu.make_async_remote_copy(
                src_ref=ins[t], dst_ref=outs[t], send_sem=s_send.at[t], recv_sem=s_recv.at[t],
                device_id=(x, y, 1 - c), device_id_type=MESH)

        for t in range(nt):
            copy(t).start()
        for t in range(nt):
            copy(t).wait()

    return pl.pallas_call(
        body, name="swap_with_sibling",
        out_shape=[jax.ShapeDtypeStruct(r.shape, F32) for r in reduced],
        in_specs=[HBM_SPEC] * nt, out_specs=[HBM_SPEC] * nt,
        scratch_shapes=[pltpu.SemaphoreType.DMA((nt,))] * 2,
    )(*reduced)


def _gather_small(buf):
    def body(in_ref, out_ref, s_send, s_recv, l_sem):
        x, y, c, _ = _place()
        flips = [(fx, fy, fc) for fx in (0, 1) for fy in (0, 1) for fc in (0, 1)][1:]

        def peer(f):
            return tuple(1 - a if flip else a for a, flip in zip((x, y, c), f))

        def slot(p):
            return 4 * p[0] + 2 * p[1] + p[2]

        local = pltpu.make_async_copy(in_ref, out_ref.at[slot((x, y, c))], l_sem)
        local.start()

        def copy(k, owner):
            return pltpu.make_async_remote_copy(
                src_ref=in_ref, dst_ref=out_ref.at[slot(owner)], send_sem=s_send.at[k], recv_sem=s_recv.at[k],
                device_id=peer(flips[k]), device_id_type=MESH)

        for k in range(7):
            copy(k, (x, y, c)).start()
        for k in range(7):
            copy(k, peer(flips[k])).wait()
        local.wait()

    return pl.pallas_call(
        body, name="gather_small", out_shape=jax.ShapeDtypeStruct((8,) + buf.shape, F32),
        in_specs=[HBM_SPEC], out_specs=HBM_SPEC,
        scratch_shapes=[pltpu.SemaphoreType.DMA((7,))] * 2 + [pltpu.SemaphoreType.DMA],
    )(buf)


def _rows128(a):
    flat = a.reshape(-1)
    rows = -(-flat.shape[0] // LANES)
    rows8 = -(-rows // 8) * 8
    flat = jnp.pad(flat, (0, rows8 * LANES - flat.shape[0]))
    return flat.reshape(rows8, LANES)


def _pack(parts):
    return jnp.concatenate([_rows128(p) for p in parts], axis=0)


def _unpack(buf, shapes):
    out, r = [], 0
    for shp in shapes:
        n = int(np.prod(shp))
        rows8 = -(-(-(-n // LANES)) // 8) * 8
        out.append(buf[r:r + rows8].reshape(-1)[:n].reshape(shp))
        r += rows8
    return out


def kernel(x, a_norm_g, a_w_in, a_v_norm_g, a_w_s, a_b_s, a_w_out, b_norm_g, b_w_in, b_f_bias, b_q_norm_g, b_k_norm_g, b_w_out, loss_target, m_a_norm_g, m_a_w_in, m_a_v_norm_g, m_a_w_s, m_a_b_s, m_a_w_out, m_b_norm_g, m_b_w_in, m_b_f_bias, m_b_q_norm_g, m_b_k_norm_g, m_b_w_out, v_a_norm_g, v_a_w_in, v_a_v_norm_g, v_a_w_s, v_a_b_s, v_a_w_out, v_b_norm_g, v_b_w_in, v_b_f_bias, v_b_q_norm_g, v_b_k_norm_g, v_b_w_out):
    xs = x[0]
    target = loss_target[0]
    S, D = xs.shape
    n_layers = a_w_in.shape[0]
    assert n_layers == 2
    W = a_v_norm_g.shape[1]
    G = a_w_s.shape[1]
    H = b_f_bias.shape[1]
    HW = H * HEAD
    tq_fwd = _tile(S, 512)
    tq_bwd = _tile(S, 512)
    core = lax.axis_index("c")
    chip = 2 * lax.axis_index("x") + lax.axis_index("y")

    own = dict(a_w_in=a_w_in.astype(BF16), a_w_out=a_w_out.astype(BF16), b_w_in=b_w_in.astype(BF16),
               b_w_out=b_w_out.astype(BF16), b_norm_g=b_norm_g.reshape(n_layers, 1, -1))
    cb = b_w_in.shape[2]
    w_ain, w_aout, w_bmain, w_bf, w_bout, b_norm_full = ([None] * n_layers for _ in range(6))

    def blocks_of(tensors, layer):
        return [own[n][layer] for n in tensors]

    def take(tensors, layer, arrived):
        for n, got, mine in zip(tensors, arrived, blocks_of(tensors, layer)):
            full = lax.dynamic_update_slice(got, mine[None], (chip, 0, 0))
            if n == "a_w_in":
                w_ain[layer] = full
            elif n == "a_w_out":
                w_aout[layer] = full
            elif n == "b_w_out":
                w_bout[layer] = full
            elif n == "b_norm_g":
                b_norm_full[layer] = full.reshape(D)
            else:
                cols = jnp.transpose(full, (1, 0, 2)).reshape(D, N_CHIPS * cb)
                w_bmain[layer] = cols[:, :4 * HW]
                w_bf[layer] = jnp.pad(cols[:, 4 * HW:], ((0, 0), (0, LANES - H)))

    first_a = ("a_w_in", "a_w_out")
    first_b = ("b_w_in", "b_w_out", "b_norm_g")
    take(first_a, 0, _gather_weights("gather_a0", blocks_of(first_a, 0), 0))
    causal = jnp.tril(jnp.ones((CHUNK, CHUNK), dtype=bool))
    wc = jnp.where(causal[None, None], a_w_s, 0).astype(BF16)
    wc_t = jnp.swapaxes(wc, 2, 3)
    bs_t = jnp.swapaxes(a_b_s, 1, 2)
    f_bias = jnp.pad(b_f_bias, ((0, 0), (0, LANES - H))).reshape(n_layers, 1, LANES)

    def view_ain(l):
        return _View(w_ain[l], "col")

    def view_aout(l):
        return _View(w_aout[l], "row")

    def view_bout(l):
        return _View(w_bout[l], "row")

    saved = []
    cur = xs
    for i in range(2 * n_layers):
        l = i // 2
        if i % 2 == 0:
            h, h_t = _rmsnorm_fwd(f"a{l}_norm", cur, a_norm_g[l])
            if i == 0:
                p, arrived = _matmul(f"a{l}_in", _View(h), view_ain(l), out_dtype=BF16, tm=1024, tn=1024, tk=2048,
                                     carried=("gather", blocks_of(first_b, 0), 0))
                take(first_b, 0, arrived)
            else:
                p = _matmul(f"a{l}_in", _View(h), view_ain(l), out_dtype=BF16, tm=1024, tn=1024, tk=2048)
            y = _gate_fwd(f"a{l}_gate", p, a_v_norm_g[l], wc[l], bs_t[l])
            nxt = _matmul(f"a{l}_out", _View(y), view_aout(l), tm=1024, tn=1024, tk=1024, residual=cur)
            saved.append((cur, h_t, p, y))
        else:
            h, h_t = _rmsnorm_fwd(f"b{l}_norm", cur, b_norm_full[l])
            proj = _matmul(f"b{l}_in", _View(h), _View(w_bmain[l]), out_dtype=BF16, tm=1024, tn=1024, tk=2048)
            f = _matmul(f"b{l}_inf", _View(h), _View(w_bf[l]), tm=1024, tn=LANES, tk=2048)
            cum = _fox_cum(f"b{l}_cum", f, f_bias[l])
            qa, ka, va = _qkv_prep(f"b{l}_qkv", proj, cum, b_q_norm_g[l], b_k_norm_g[l], H)
            if i == 1:
                o, y, lse, arrived = _attn_fwd(f"b{l}_attn", qa, ka, va, proj, H, tq_fwd,
                                               carried=("gather", blocks_of(first_a + first_b, 1), 1))
                take(first_a + first_b, 1, arrived)
            else:
                o, y, lse, _ = _attn_fwd(f"b{l}_attn", qa, ka, va, proj, H, tq_fwd)
            nxt = _matmul(f"b{l}_out", _View(y), view_bout(l), tm=1024, tn=1024, tk=1024, residual=cur)
            saved.append((cur, h_t, proj, f, qa, ka, va, o, y, lse))
        cur = nxt

    g, gb, lcols = _loss_grad(cur, target)
    loss = lax.psum(0.5 * jnp.sum(lcols) / D, ("x", "y", "c"))

    big = {"a_w_in": [None] * n_layers, "a_w_out": [None] * n_layers,
           "b_w_in": [None] * n_layers, "b_w_out": [None] * n_layers}
    small = {k: [None] * n_layers for k in
             ("a_norm_g", "a_v_norm_g", "a_w_s", "a_b_s", "b_norm_g", "b_f_bias", "b_q_norm_g", "b_k_norm_g")}
    names = ["a_w_in", "a_w_out", "b_w_in", "b_w_out"]
    reduced = [{} for _ in range(n_layers)]

    def chip_sums_of(tag, tensors, layer):
        mine = [big[n][layer] for n in tensors]
        got = _send_to_owner(f"to_owner{tag}", mine, layer)
        active = (core == layer).astype(jnp.int32).reshape(1)
        sums = []
        for n, a, b in zip(tensors, mine, got):
            shp = a.shape
            flat = lambda t: t.reshape(shp[0] * shp[1], shp[2])
            sums.append(_add_if(f"chipsum{tag}_{n}", flat(a), flat(b), active).reshape(shp))
        return sums, active

    def reduce_slots(tag, tensors, layer, sums, got, active):
        for n, g_, s_ in zip(tensors, got, sums):
            slots = lax.dynamic_update_slice(g_, lax.dynamic_index_in_dim(s_, chip, keepdims=True), (chip, 0, 0))
            reduced[layer][n] = _sum_slots(f"reduce{tag}_{n}", slots, active)

    for i in reversed(range(2 * n_layers)):
        l = i // 2
        if i % 2 == 0:
            x_in, h_t, p, y = saved[i]
            dy = _matmul(f"a{l}_dy", _View(gb), view_aout(l), tb=True, out_dtype=BF16, tm=1024, tn=1024, tk=2048)
            d_wout = _matmul(f"a{l}_dwout", _View(y), _View(gb), ta=True, tm=2048, tn=1024, tk=1024)
            big["a_w_out"][l] = d_wout.reshape(N_CHIPS, W // N_CHIPS, D)
            dp, d_ws, d_bs, d_gv = _gate_bwd(f"a{l}_dgate", p, dy, a_v_norm_g[l], wc[l], wc_t[l], bs_t[l])
            dims_dh = dict(tb=True, tm=1024, tn=1024, tk=3072)
            dims_dwin = dict(tm=2048, tn=1024, tk=1024, out_colblocks=N_CHIPS)
            if i == 0:
                early = ["b_w_in", "b_w_out", "a_w_out"]
                sums_e, active0 = chip_sums_of("0e", early, 0)
                dh, got_b = _matmul(f"a{l}_dh", _View(dp), view_ain(l), carried=("scatter", sums_e[:2], 0), **dims_dh)
                d_win, got_a = _matmul(f"a{l}_dwin", _View(h_t), _View(dp), carried=("scatter", sums_e[2:], 0),
                                       **dims_dwin)
                reduce_slots("0e", early, 0, sums_e, got_b + got_a, active0)
            else:
                dh = _matmul(f"a{l}_dh", _View(dp), view_ain(l), **dims_dh)
                d_win = _matmul(f"a{l}_dwin", _View(h_t), _View(dp), **dims_dwin)
            g, gb, d_gn = _rmsnorm_bwd(f"a{l}_dnorm", x_in, a_norm_g[l], dh, g)
            big["a_w_in"][l] = d_win
            small["a_norm_g"][l] = d_gn.reshape(D)
            small["a_v_norm_g"][l] = d_gv.reshape(W)
            small["a_w_s"][l] = jnp.where(causal[None], d_ws, 0.0)
            small["a_b_s"][l] = d_bs[:, :G].T
        else:
            x_in, h_t, proj, f, qa, ka, va, o, y, lse = saved[i]
            dy = _matmul(f"b{l}_dy", _View(gb), view_bout(l), tb=True, out_dtype=BF16, tm=1024, tn=1024, tk=2048)
            d_wout = _matmul(f"b{l}_dwout", _View(y), _View(gb), ta=True, tm=2048, tn=1024, tk=1024)
            lse_lanes = jnp.pad(lse.reshape(H, S).T, ((0, 0), (0, LANES - H)))
            doa, dz, qab = _attn_bwd_prep(f"b{l}_dprep", dy, o, proj, qa, lse_lanes, H)
            early = (i == 1)
            if early:
                sums1, active1 = chip_sums_of("1", names, 1)
            dqa, dka, dv, got1 = _attn_bwd(f"b{l}_dattn", qab, doa, ka, va, H, tq_bwd,
                                           carried=(sums1, 1) if early else None)
            if early:
                reduce_slots("1", names, 1, sums1, got1, active1)
            dproj, d_gq, d_gk, dcum = _qk_bwd(f"b{l}_dqk", proj, dqa, dka, dv, dz, b_q_norm_g[l], b_k_norm_g[l], H)
            df, d_fb = _fox_cum_bwd(f"b{l}_dcum", dcum, f, f_bias[l])
            dh_f = _matmul(f"b{l}_dhf", _View(df), _View(w_bf[l]), tb=True, tm=1024, tn=1024, tk=LANES)
            dh = _matmul(f"b{l}_dh", _View(dproj), _View(w_bmain[l]), tb=True, tm=1024, tn=1024, tk=2048,
                         residual=dh_f)
            d_wmain = _matmul(f"b{l}_dwin", _View(h_t), _View(dproj), tm=2048, tn=1024, tk=1024)
            d_wf = _matmul(f"b{l}_dwinf", _View(h_t), _View(df), tm=2048, tn=LANES, tk=1024)
            d_win = jnp.concatenate([d_wmain, d_wf[:, :H]], axis=1)
            g, gb, d_gn = _rmsnorm_bwd(f"b{l}_dnorm", x_in, b_norm_full[l], dh, g)
            big["b_w_in"][l] = jnp.transpose(d_win.reshape(D, N_CHIPS, cb), (1, 0, 2))
            big["b_w_out"][l] = d_wout.reshape(N_CHIPS, HW // N_CHIPS, D)
            small["b_norm_g"][l] = d_gn.reshape(D)
            small["b_f_bias"][l] = d_fb[0, :H]
            small["b_q_norm_g"][l] = d_gq.reshape(HEAD)
            small["b_k_norm_g"][l] = d_gk.reshape(HEAD)
    grad_x = g[None]

    late = ["a_w_in"]
    sums_l, active0 = chip_sums_of("0l", late, 0)
    reduce_slots("0l", late, 0, sums_l, _scatter_chip_sums("scatter0", sums_l, 0), active0)
    mine = [jnp.where(core == 0, reduced[0][n], reduced[1][n]) for n in names]
    others = _swap_with_sibling(mine)
    grads = {n: jnp.where(core == 0, jnp.stack([m_, o_]), jnp.stack([o_, m_]))
             for n, m_, o_ in zip(names, mine, others)}

    small_names = ["a_norm_g", "a_v_norm_g", "a_w_s", "a_b_s", "b_norm_g", "b_f_bias", "b_q_norm_g", "b_k_norm_g"]
    small_parts = [jnp.stack(small[n]) for n in small_names]
    small_sum = _sum_slots("reduce_small", _gather_small(_pack(small_parts)), jnp.ones((1,), jnp.int32))
    for n, a in zip(small_names, _unpack(small_sum, [p.shape for p in small_parts])):
        grads[n] = a
    nb = b_norm_g.shape[1]
    grads["b_norm_g"] = lax.dynamic_slice_in_dim(grads["b_norm_g"], chip * nb, nb, axis=1)

    weights = dict(a_norm_g=a_norm_g, a_w_in=a_w_in, a_v_norm_g=a_v_norm_g, a_w_s=a_w_s, a_b_s=a_b_s,
                   a_w_out=a_w_out, b_norm_g=b_norm_g, b_w_in=b_w_in, b_f_bias=b_f_bias,
                   b_q_norm_g=b_q_norm_g, b_k_norm_g=b_k_norm_g, b_w_out=b_w_out)
    mom1 = dict(a_norm_g=m_a_norm_g, a_w_in=m_a_w_in, a_v_norm_g=m_a_v_norm_g, a_w_s=m_a_w_s, a_b_s=m_a_b_s,
                a_w_out=m_a_w_out, b_norm_g=m_b_norm_g, b_w_in=m_b_w_in, b_f_bias=m_b_f_bias,
                b_q_norm_g=m_b_q_norm_g, b_k_norm_g=m_b_k_norm_g, b_w_out=m_b_w_out)
    mom2 = dict(a_norm_g=v_a_norm_g, a_w_in=v_a_w_in, a_v_norm_g=v_a_v_norm_g, a_w_s=v_a_w_s, a_b_s=v_a_b_s,
                a_w_out=v_a_w_out, b_norm_g=v_b_norm_g, b_w_in=v_b_w_in, b_f_bias=v_b_f_bias,
                b_q_norm_g=v_b_q_norm_g, b_k_norm_g=v_b_k_norm_g, b_w_out=v_b_w_out)
    order = ["a_norm_g", "a_w_in", "a_v_norm_g", "a_w_s", "a_b_s", "a_w_out", "b_norm_g", "b_w_in", "b_f_bias",
             "b_q_norm_g", "b_k_norm_g", "b_w_out"]
    delta, new_m, new_v = {}, {}, {}
    for n in names:
        shp = weights[n].shape
        flat = lambda a: a.reshape(shp[0] * shp[1], shp[2])
        d, nm, nv = _adamw(f"adamw_{n}", flat(weights[n]), flat(grads[n]), flat(mom1[n]), flat(mom2[n]))
        delta[n], new_m[n], new_v[n] = d.reshape(shp), nm.reshape(shp), nv.reshape(shp)
    small_shapes = [weights[n].shape for n in small_names]
    pack_w, pack_g, pack_m, pack_v = (_pack([d[n] for n in small_names]) for d in (weights, grads, mom1, mom2))
    d, nm, nv = _adamw("adamw_small", pack_w, pack_g, pack_m, pack_v)
    for dst, buf in ((delta, d), (new_m, nm), (new_v, nv)):
        for n, a in zip(small_names, _unpack(buf, small_shapes)):
            dst[n] = a

    return (loss, grad_x, *[grads[n] for n in order], *[delta[n] for n in order],
            *[new_m[n] for n in order], *[new_v[n] for n in order])
```

```python
import functools
import math

import numpy as np
import jax
import jax.numpy as jnp
from jax import lax
from jax.experimental import pallas as pl
from jax.experimental.pallas import tpu as pltpu

F32 = jnp.float32
BF16 = jnp.bfloat16
MESH = pl.DeviceIdType.MESH

EPS = 1e-6
CHUNK = 128
HEAD = 128
LANES = 128
N_CHIPS = 4
VMEM_LIMIT = 56 * 1024 * 1024

ADAM_LR = 0.001
ADAM_B1 = 0.9
ADAM_B2 = 0.999
ADAM_EPS = 1e-08
ADAM_WD = 0.01
ADAM_STEP = 10

_NT = (((1,), (1,)), ((), ()))
_TN = (((0,), (0,)), ((), ()))
_GELU_C = math.sqrt(2.0 / math.pi)

HBM_SPEC = pl.BlockSpec(memory_space=pltpu.HBM)


def _params(*sem):
    return pltpu.CompilerParams(dimension_semantics=sem, vmem_limit_bytes=VMEM_LIMIT)


def _tile(dim, pref, unit=LANES):
    t = (min(pref, dim) // unit) * unit
    while t >= unit:
        if dim % t == 0:
            return t
        t -= unit
    return dim


def _gelu(x):
    return 0.5 * x * (1.0 + jnp.tanh(_GELU_C * (x + 0.044715 * (x * x * x))))


def _gelu_and_grad(x):
    x2 = x * x
    t = jnp.tanh(_GELU_C * (x + 0.044715 * (x2 * x)))
    val = 0.5 * x * (1.0 + t)
    grad = 0.5 * (1.0 + t) + 0.5 * x * (1.0 - t * t) * (_GELU_C * (1.0 + 3.0 * 0.044715 * x2))
    return val, grad


def _sigmoid(x):
    return 1.0 / (1.0 + jnp.exp(-x))


class _View:
    def __init__(self, arr, kind="2d", lead=()):
        self.arr, self.kind, self.lead = arr, kind, tuple(lead)
        shp = arr.shape[len(self.lead):]
        if kind == "2d":
            self.R, self.C = shp
        elif kind == "col":
            self.nb, self.R, self.cb = shp
            self.C = self.nb * self.cb
        else:
            self.nb, self.rb, self.C = shp
            self.R = self.nb * self.rb

    def fit(self, tr, tc):
        if self.kind == "col":
            tc = _tile(self.cb, tc)
        elif self.kind == "row":
            tr = _tile(self.rb, tr, unit=8)
        return tr, tc

    def spec(self, tr, tc, rc_of_grid):
        lead = self.lead
        sq = (None,) * len(lead)
        if self.kind == "2d":
            return pl.BlockSpec(sq + (tr, tc), lambda *g: lead + tuple(rc_of_grid(*g)))
        if self.kind == "col":
            q = self.cb // tc

            def im(*g):
                r, c = rc_of_grid(*g)
                return lead + (c // q, r, c % q)

            return pl.BlockSpec(sq + (None, tr, tc), im)
        q = self.rb // tr

        def im(*g):
            r, c = rc_of_grid(*g)
            return lead + (r // q, r % q, c)

        return pl.BlockSpec(sq + (None, tr, tc), im)


def _matmul(name, a, b, *, ta=False, tb=False, out_dtype=F32, tm=1024, tn=1024, tk=1024,
            out_colblocks=None, residual=None, carried=None):
    M, K = (a.C, a.R) if ta else (a.R, a.C)
    N, K2 = (b.R, b.C) if tb else (b.C, b.R)
    assert K == K2, (name, K, K2)
    tm, tn, tk = _tile(M, tm), _tile(N, tn), _tile(K, tk)
    if ta:
        tk, tm = a.fit(tk, tm)
    else:
        tm, tk = a.fit(tm, tk)
    if tb:
        tn, tk2 = b.fit(tn, tk)
    else:
        tk2, tn = b.fit(tk, tn)
    if tk2 != tk:
        tk = min(tk, tk2)
        if ta:
            tk, tm = a.fit(tk, tm)
        else:
            tm, tk = a.fit(tm, tk)
    if out_colblocks:
        tn = _tile(N // out_colblocks, tn)
    assert M % tm == 0 and N % tn == 0 and K % tk == 0, (name, M, N, K, tm, tn, tk)
    nk = K // tk
    assert nk == 1 or out_dtype == F32, name
    dims = (((0 if ta else 1,), (1 if tb else 0,)), ((), ()))
    grid = (M // tm, N // tn, nk)
    kind, g_blocks, g_layer = carried if carried else (None, [], None)
    ng = len(g_blocks)
    n_in = 2 + (residual is not None)

    def body(*refs):
        a_ref, b_ref = refs[:2]
        r_ref = refs[2] if residual is not None else None
        o_ref = refs[n_in + ng]
        k = pl.program_id(2)
        if carried:
            mine, start, finish = _carried_exchange(kind, refs[n_in:n_in + ng], refs[n_in + ng + 1:n_in + 2 * ng + 1],
                                                    refs[n_in + 2 * ng + 1:], g_layer)
            at = lambda step: functools.reduce(
                jnp.logical_and, [pl.program_id(d) == (0 if step == "first" else grid[d] - 1) for d in range(3)])

            @pl.when(jnp.logical_and(mine, at("first")))
            def _():
                start()

        def product():
            return lax.dot_general(a_ref[...], b_ref[...], dims, preferred_element_type=F32)

        if nk == 1:
            total = product()
            if r_ref is not None:
                total = total + r_ref[...]
            o_ref[...] = total.astype(out_dtype)
        else:
            @pl.when(k == 0)
            def _():
                o_ref[...] = product() + r_ref[...] if r_ref is not None else product()

            @pl.when(k > 0)
            def _():
                o_ref[...] += product()

        if carried:
            @pl.when(at("last"))
            def _():
                finish()

    a_spec = a.spec(tk, tm, lambda i, j, k: (k, i)) if ta else a.spec(tm, tk, lambda i, j, k: (i, k))
    b_spec = b.spec(tn, tk, lambda i, j, k: (j, k)) if tb else b.spec(tk, tn, lambda i, j, k: (k, j))
    in_specs, args = [a_spec, b_spec], [a.arr, b.arr]
    if residual is not None:
        in_specs.append(pl.BlockSpec((tm, tn), lambda i, j, k: (i, j)))
        args.append(residual)
    in_specs += [HBM_SPEC] * ng
    args += list(g_blocks)
    if out_colblocks:
        q = (N // out_colblocks) // tn
        out_shape = jax.ShapeDtypeStruct((out_colblocks, M, N // out_colblocks), out_dtype)
        out_spec = pl.BlockSpec((None, tm, tn), lambda i, j, k: (j // q, i, j % q))
    else:
        out_shape = jax.ShapeDtypeStruct((M, N), out_dtype)
        out_spec = pl.BlockSpec((tm, tn), lambda i, j, k: (i, j))
    out = pl.pallas_call(
        body, name=name, grid=grid, in_specs=in_specs, out_specs=[out_spec] + [HBM_SPEC] * ng,
        out_shape=[out_shape] + _carried_out_shapes(kind, g_blocks),
        scratch_shapes=_carried_sems(kind, ng),
        compiler_params=_params("arbitrary", "arbitrary", "arbitrary"),
    )(*args)
    return (out[0], list(out[1:])) if carried else out[0]


def _rmsnorm_fwd(name, x, gain):
    S, D = x.shape
    tr = _tile(S, 512)

    def body(x_ref, g_ref, h_ref, ht_ref):
        xv = x_ref[...]
        r = lax.rsqrt(jnp.mean(xv * xv, axis=-1, keepdims=True) + EPS)
        h = xv * r * g_ref[...]
        h_ref[...] = h.astype(BF16)
        ht_ref[...] = h.T.astype(BF16)

    return pl.pallas_call(
        body, name=name, grid=(S // tr,),
        in_specs=[pl.BlockSpec((tr, D), lambda i: (i, 0)), pl.BlockSpec((1, D), lambda i: (0, 0))],
        out_specs=[pl.BlockSpec((tr, D), lambda i: (i, 0)), pl.BlockSpec((D, tr), lambda i: (0, i))],
        out_shape=[jax.ShapeDtypeStruct((S, D), BF16), jax.ShapeDtypeStruct((D, S), BF16)],
        compiler_params=_params("arbitrary"),
    )(x, gain.reshape(1, D))


def _rmsnorm_bwd(name, x, gain, dh, g_res):
    S, D = x.shape
    tr = _tile(S, 256, unit=8)

    def body(x_ref, g_ref, dh_ref, res_ref, dx_ref, dxb_ref, dg_ref):
        i = pl.program_id(0)
        xv = x_ref[...]
        r = lax.rsqrt(jnp.mean(xv * xv, axis=-1, keepdims=True) + EPS)
        xhat = xv * r
        dhv = dh_ref[...]
        part = jnp.sum(dhv * xhat, axis=0, keepdims=True)

        @pl.when(i == 0)
        def _():
            dg_ref[...] = part

        @pl.when(i > 0)
        def _():
            dg_ref[...] += part

        dxhat = dhv * g_ref[...]
        dx = res_ref[...] + r * (dxhat - xhat * jnp.mean(dxhat * xhat, axis=-1, keepdims=True))
        dx_ref[...] = dx
        dxb_ref[...] = dx.astype(BF16)

    row = pl.BlockSpec((tr, D), lambda i: (i, 0))
    vec = pl.BlockSpec((1, D), lambda i: (0, 0))
    return pl.pallas_call(
        body, name=name, grid=(S // tr,), in_specs=[row, vec, row, row], out_specs=[row, row, vec],
        out_shape=[jax.ShapeDtypeStruct((S, D), F32), jax.ShapeDtypeStruct((S, D), BF16),
                   jax.ShapeDtypeStruct((1, D), F32)],
        compiler_params=_params("arbitrary"),
    )(x, gain.reshape(1, D), dh, g_res)


def _loss_grad(x, target):
    S, D = x.shape
    tr = _tile(S, 512, unit=8)

    def body(x_ref, t_ref, g_ref, gb_ref, l_ref):
        i = pl.program_id(0)
        e = x_ref[...] - t_ref[...]
        g = e * (1.0 / D)
        g_ref[...] = g
        gb_ref[...] = g.astype(BF16)
        part = jnp.sum(e * e, axis=0, keepdims=True)

        @pl.when(i == 0)
        def _():
            l_ref[...] = part

        @pl.when(i > 0)
        def _():
            l_ref[...] += part

    row = pl.BlockSpec((tr, D), lambda i: (i, 0))
    vec = pl.BlockSpec((1, D), lambda i: (0, 0))
    return pl.pallas_call(
        body, name="loss_grad", grid=(S // tr,), in_specs=[row, row], out_specs=[row, row, vec],
        out_shape=[jax.ShapeDtypeStruct((S, D), F32), jax.ShapeDtypeStruct((S, D), BF16),
                   jax.ShapeDtypeStruct((1, D), F32)],
        compiler_params=_params("arbitrary"),
    )(x, target)


def _gate_fwd(name, p, v_gain, wc, bs_t):
    S, W3 = p.shape
    W = W3 // 3
    G = wc.shape[0]
    gd = W // G

    def body(p_ref, gv_ref, wc_ref, bs_ref, y_ref):
        vg = _gelu(p_ref[:, W:2 * W].astype(F32))
        r = lax.rsqrt(jnp.mean(vg * vg, axis=-1, keepdims=True) + EPS)
        vb = (vg * r * gv_ref[...]).astype(BF16)
        zp = p_ref[:, 2 * W:].astype(F32)
        gate = _gelu(p_ref[:, :W].astype(F32)) * (zp * _sigmoid(zp))
        for g in range(G):
            sl = slice(g * gd, (g + 1) * gd)
            mixed = jnp.dot(wc_ref[g], vb[:, sl], preferred_element_type=F32) + bs_ref[:, g:g + 1]
            y_ref[:, sl] = (gate[:, sl] * mixed).astype(BF16)

    return pl.pallas_call(
        body, name=name, grid=(S // CHUNK,),
        in_specs=[pl.BlockSpec((CHUNK, W3), lambda i: (i, 0)), pl.BlockSpec((1, W), lambda i: (0, 0)),
                  pl.BlockSpec((G, CHUNK, CHUNK), lambda i: (0, 0, 0)), pl.BlockSpec((CHUNK, G), lambda i: (0, 0))],
        out_specs=pl.BlockSpec((CHUNK, W), lambda i: (i, 0)),
        out_shape=jax.ShapeDtypeStruct((S, W), BF16),
        compiler_params=_params("arbitrary"),
    )(p, v_gain.reshape(1, W), wc, bs_t)


def _gate_bwd(name, p, dy, v_gain, wc, wc_t, bs_t):
    S, W3 = p.shape
    W = W3 // 3
    G = wc.shape[0]
    gd = W // G

    def body(p_ref, dy_ref, gv_ref, wc_ref, wct_ref, bs_ref, dp_ref, dws_ref, dbs_ref, dgv_ref, dv_scr):
        i = pl.program_id(0)

        @pl.when(i == 0)
        def _():
            dws_ref[...] = jnp.zeros_like(dws_ref)
            dbs_ref[...] = jnp.zeros_like(dbs_ref)
            dgv_ref[...] = jnp.zeros_like(dgv_ref)

        gu, dgu = _gelu_and_grad(p_ref[:, :W].astype(F32))
        vg, dvg_dv = _gelu_and_grad(p_ref[:, W:2 * W].astype(F32))
        zp = p_ref[:, 2 * W:].astype(F32)
        sig = _sigmoid(zp)
        sz = zp * sig
        dsz = sig * (1.0 + zp * (1.0 - sig))
        r = lax.rsqrt(jnp.mean(vg * vg, axis=-1, keepdims=True) + EPS)
        vhat = vg * r
        gv = gv_ref[...]
        vb = (vhat * gv).astype(BF16)
        dy = dy_ref[...].astype(F32)
        lane = lax.broadcasted_iota(jnp.int32, (CHUNK, LANES), 1)
        dbs = jnp.zeros((CHUNK, LANES), F32)
        for g in range(G):
            sl = slice(g * gd, (g + 1) * gd)
            vsl = vb[:, sl]
            mixed = jnp.dot(wc_ref[g], vsl, preferred_element_type=F32) + bs_ref[:, g:g + 1]
            dyg, gug, szg = dy[:, sl], gu[:, sl], sz[:, sl]
            dp_ref[:, sl] = (dyg * mixed * szg * dgu[:, sl]).astype(BF16)
            dp_ref[:, 2 * W + g * gd:2 * W + (g + 1) * gd] = (dyg * gug * mixed * dsz[:, sl]).astype(BF16)
            dm = dyg * gug * szg
            dmb = dm.astype(BF16)
            dws_ref[g] += lax.dot_general(dmb, vsl, _NT, preferred_element_type=F32)
            dbs = dbs + jnp.where(lane == g, jnp.sum(dm, axis=1, keepdims=True), 0.0)
            dv_scr[:, sl] = jnp.dot(wct_ref[g], dmb, preferred_element_type=F32)
        dbs_ref[...] += dbs
        dv = dv_scr[...]
        dgv_ref[...] += jnp.sum(dv * vhat, axis=0, keepdims=True)
        dvhat = dv * gv
        dvg = r * (dvhat - vhat * jnp.mean(dvhat * vhat, axis=-1, keepdims=True))
        dp_ref[:, W:2 * W] = (dvg * dvg_dv).astype(BF16)

    return pl.pallas_call(
        body, name=name, grid=(S // CHUNK,),
        in_specs=[pl.BlockSpec((CHUNK, W3), lambda i: (i, 0)), pl.BlockSpec((CHUNK, W), lambda i: (i, 0)),
                  pl.BlockSpec((1, W), lambda i: (0, 0)),
                  pl.BlockSpec((G, CHUNK, CHUNK), lambda i: (0, 0, 0)),
                  pl.BlockSpec((G, CHUNK, CHUNK), lambda i: (0, 0, 0)),
                  pl.BlockSpec((CHUNK, G), lambda i: (0, 0))],
        out_specs=[pl.BlockSpec((CHUNK, W3), lambda i: (i, 0)),
                   pl.BlockSpec((G, CHUNK, CHUNK), lambda i: (0, 0, 0)),
                   pl.BlockSpec((CHUNK, LANES), lambda i: (0, 0)),
                   pl.BlockSpec((1, W), lambda i: (0, 0))],
        out_shape=[jax.ShapeDtypeStruct((S, W3), BF16), jax.ShapeDtypeStruct((G, CHUNK, CHUNK), F32),
                   jax.ShapeDtypeStruct((CHUNK, LANES), F32), jax.ShapeDtypeStruct((1, W), F32)],
        scratch_shapes=[pltpu.VMEM((CHUNK, W), F32)],
        compiler_params=_params("arbitrary"),
    )(p, dy, v_gain.reshape(1, W), wc, wc_t, bs_t)


AUG = 2 * HEAD
LOG2E = 1.0 / math.log(2.0)
Q_SUM_LANE = HEAD + 3
K_SUM_LANE = HEAD


def _pieces(x, sign=1.0):
    hi, mid, lo = _split3(sign * x)
    return hi.astype(F32), mid.astype(F32), lo.astype(F32)


def _lanes(lane, start, vals, rest):
    out = rest
    for n, v in enumerate(vals):
        out = jnp.where(lane == start + n, v, out)
    return out


def _qkv_prep(name, proj, cum, q_gain, k_gain, H):
    S = proj.shape[0]
    HW = H * HEAD
    tr = _tile(S, 256, unit=8)
    sigma = (HEAD ** -0.5) * LOG2E

    def body(q_ref, k_ref, v_ref, c_ref, gq_ref, gk_ref, qa_ref, ka_ref, va_ref):
        lane = lax.broadcasted_iota(jnp.int32, (tr, HEAD), 1)
        zero = jnp.zeros((tr, HEAD), F32)
        v_aug = jnp.where(lane < 3, 1.0, zero).astype(BF16)
        for h in range(H):
            sl = slice(h * HEAD, (h + 1) * HEAD)
            a0 = h * AUG
            t = q_ref[:, sl].astype(F32)
            r = lax.rsqrt(jnp.mean(t * t, axis=-1, keepdims=True) + EPS)
            qa_ref[:, a0:a0 + HEAD] = (t * r * gq_ref[...] * sigma).astype(BF16)
            t = k_ref[:, sl].astype(F32)
            r = lax.rsqrt(jnp.mean(t * t, axis=-1, keepdims=True) + EPS)
            ka_ref[:, a0:a0 + HEAD] = (t * r * gk_ref[...]).astype(BF16)
            va_ref[:, a0:a0 + HEAD] = v_ref[:, sl]
            va_ref[:, a0 + HEAD:a0 + AUG] = v_aug
            c2 = c_ref[:, h:h + 1] * LOG2E
            qa_ref[:, a0 + HEAD:a0 + AUG] = _lanes(lane, 0, _pieces(c2) + (1.0, 1.0, 1.0), zero).astype(BF16)
            ka_ref[:, a0 + HEAD:a0 + AUG] = _lanes(
                lane, 0, (1.0, 1.0, 1.0) + _pieces(c2, -1.0) + (1.0, 1.0, 1.0), zero).astype(BF16)

    col = lambda c: pl.BlockSpec((tr, HW), lambda i: (i, c))
    vec = pl.BlockSpec((1, HEAD), lambda i: (0, 0))
    aug = pl.BlockSpec((tr, H * AUG), lambda i: (i, 0))
    return pl.pallas_call(
        body, name=name, grid=(S // tr,),
        in_specs=[col(0), col(1), col(2), pl.BlockSpec((tr, LANES), lambda i: (i, 0)), vec, vec],
        out_specs=[aug] * 3, out_shape=[jax.ShapeDtypeStruct((S, H * AUG), BF16)] * 3,
        compiler_params=_params("arbitrary"),
    )(proj, proj, proj, cum, q_gain.reshape(1, HEAD), k_gain.reshape(1, HEAD))


def _split3(x):
    hi = x.astype(BF16)
    r1 = x - hi.astype(F32)
    mid = r1.astype(BF16)
    lo = (r1 - mid.astype(F32)).astype(BF16)
    return hi, mid, lo


def _tri_sum(tri, x):
    hi, mid, lo = _split3(x)
    d = lambda t: jnp.dot(tri, t, preferred_element_type=F32)
    return d(hi) + (d(mid) + d(lo))


def _log_sigmoid(x):
    return jnp.minimum(x, 0.0) - jnp.log(1.0 + jnp.exp(-jnp.abs(x)))


def _fox_cum(name, f, bias):
    S = f.shape[0]
    tb = _tile(S, 256, unit=8)

    def body(f_ref, b_ref, c_ref):
        rr = lax.broadcasted_iota(jnp.int32, (tb, tb), 0)
        cc = lax.broadcasted_iota(jnp.int32, (tb, tb), 1)
        tri = (rr >= cc).astype(BF16)

        def step(t, carry):
            off = pl.multiple_of(t * tb, tb)
            lf = _log_sigmoid(f_ref[pl.ds(off, tb), :] + b_ref[...])
            c = _tri_sum(tri, lf) + carry
            c_ref[pl.ds(off, tb), :] = c
            return c[tb - 1:tb, :]

        lax.fori_loop(0, S // tb, step, jnp.zeros((1, LANES), F32))

    return pl.pallas_call(
        body, name=name, out_shape=jax.ShapeDtypeStruct((S, LANES), F32),
        in_specs=[pl.BlockSpec(memory_space=pltpu.VMEM)] * 2, out_specs=pl.BlockSpec(memory_space=pltpu.VMEM),
        compiler_params=pltpu.CompilerParams(vmem_limit_bytes=VMEM_LIMIT),
    )(f, bias)


def _fox_cum_bwd(name, dcum, f, bias):
    S = f.shape[0]
    tb = _tile(S, 256, unit=8)
    nb = S // tb

    def body(dc_ref, f_ref, b_ref, df_ref, db_ref):
        rr = lax.broadcasted_iota(jnp.int32, (tb, tb), 0)
        cc = lax.broadcasted_iota(jnp.int32, (tb, tb), 1)
        tri = (rr <= cc).astype(BF16)

        def step(t, carry):
            tail, dbias = carry
            off = pl.multiple_of((nb - 1 - t) * tb, tb)
            dlf = _tri_sum(tri, dc_ref[pl.ds(off, tb), :]) + tail
            d = dlf * _sigmoid(-(f_ref[pl.ds(off, tb), :] + b_ref[...]))
            df_ref[pl.ds(off, tb), :] = d.astype(BF16)
            return dlf[0:1, :], dbias + jnp.sum(d, axis=0, keepdims=True)

        z = jnp.zeros((1, LANES), F32)
        _, dbias = lax.fori_loop(0, nb, step, (z, z))
        db_ref[...] = dbias

    vm = pl.BlockSpec(memory_space=pltpu.VMEM)
    return pl.pallas_call(
        body, name=name, out_shape=[jax.ShapeDtypeStruct((S, LANES), BF16), jax.ShapeDtypeStruct((1, LANES), F32)],
        in_specs=[vm] * 3, out_specs=[vm] * 2,
        compiler_params=pltpu.CompilerParams(vmem_limit_bytes=VMEM_LIMIT),
    )(dcum, f, bias)


def _attn_fwd(name, qa, ka, va, proj, H, tq, carried=None):
    S = qa.shape[0]
    HW = H * HEAD
    nq = S // tq
    hp = 2 if H % 2 == 0 else 1
    tw = 2 * tq if S % (2 * tq) == 0 else tq
    kind, g_blocks, g_layer = carried if carried else (None, [], None)
    ng = len(g_blocks)

    def body(*refs):
        q_ref, k_ref, v_ref, z_ref = refs[:4]
        o_ref, y_ref, lse_ref = refs[4 + ng:7 + ng]
        i = pl.program_id(1)
        if carried:
            mine, start, finish = _carried_exchange(kind, refs[4:4 + ng], refs[7 + ng:7 + 2 * ng],
                                                    refs[7 + 2 * ng:], g_layer)
            hd = pl.program_id(0)

            @pl.when(jnp.logical_and(mine, jnp.logical_and(hd == 0, i == 0)))
            def _():
                start()

        def step(j, carry, masked):
            off = pl.multiple_of(j * tw, tw)
            out = []
            for n in range(hp):
                m, acc = carry[n]
                a = slice(n * AUG, (n + 1) * AUG)
                s = lax.dot_general(q_ref[:, a], k_ref[pl.ds(off, tw), a], _NT, preferred_element_type=F32)
                if masked:
                    qry = i * tq + lax.broadcasted_iota(jnp.int32, (tq, tw), 0)
                    key = j * tw + lax.broadcasted_iota(jnp.int32, (tq, tw), 1)
                    s = jnp.where(qry >= key, s, -jnp.inf)
                m_new = jnp.maximum(m, jnp.max(s, axis=1, keepdims=True))
                pr = jnp.exp2(s - m_new).astype(BF16)
                acc = jnp.exp2(m - m_new) * acc + jnp.dot(pr, v_ref[pl.ds(off, tw), a], preferred_element_type=F32)
                out.append((m_new, acc))
            return tuple(out)

        init = ((jnp.full((tq, 1), -jnp.inf, F32), jnp.zeros((tq, AUG), F32)),) * hp
        below = (i * tq) // tw
        carry = lax.fori_loop(0, below, lambda j, c: step(j, c, False), init)
        carry = step(below, carry, True)
        for n in range(hp):
            m, acc = carry[n]
            sl = slice(n * HEAD, (n + 1) * HEAD)
            l = acc[:, HEAD:HEAD + 1]
            o = acc[:, :HEAD] / l
            z = z_ref[:, sl].astype(F32)
            o_ref[:, sl] = o
            y_ref[:, sl] = (o * (z * _sigmoid(z))).astype(BF16)
            lse_ref[n] = m + jnp.log(l) * LOG2E

        if carried:
            @pl.when(jnp.logical_and(hd == H // hp - 1, i == nq - 1))
            def _():
                finish()

    qspec = pl.BlockSpec((tq, hp * AUG), lambda h, i: (i, h))
    kvspec = pl.BlockSpec((S, hp * AUG), lambda h, i: (0, h))
    ospec = pl.BlockSpec((tq, hp * HEAD), lambda h, i: (i, h))
    out = pl.pallas_call(
        body, name=name, grid=(H // hp, nq),
        in_specs=[qspec, kvspec, kvspec, pl.BlockSpec((tq, hp * HEAD), lambda h, i: (i, 3 * H // hp + h))]
        + [HBM_SPEC] * ng,
        out_specs=[ospec, ospec, pl.BlockSpec((hp, tq, 1), lambda h, i: (h, i, 0))] + [HBM_SPEC] * ng,
        out_shape=[jax.ShapeDtypeStruct((S, HW), F32), jax.ShapeDtypeStruct((S, HW), BF16),
                   jax.ShapeDtypeStruct((H, S, 1), F32)] + _carried_out_shapes(kind, g_blocks),
        scratch_shapes=_carried_sems(kind, ng),
        compiler_params=_params("arbitrary", "arbitrary"),
    )(qa, ka, va, proj, *g_blocks)
    return out[0], out[1], out[2], list(out[3:])


def _attn_bwd_prep(name, dy, o, proj, qa, lse, H):
    S, HW = o.shape
    tr = _tile(S, 256, unit=8)

    def body(dy_ref, o_ref, z_ref, qa_ref, lse_ref, doa_ref, dz_ref, qab_ref):
        lane = lax.broadcasted_iota(jnp.int32, (tr, HEAD), 1)
        zero = jnp.zeros((tr, HEAD), F32)
        for h in range(H):
            sl = slice(h * HEAD, (h + 1) * HEAD)
            a0 = h * AUG
            dy = dy_ref[:, sl].astype(F32)
            z = z_ref[:, sl].astype(F32)
            o = o_ref[:, sl]
            sig = _sigmoid(z)
            dob = (dy * (z * sig)).astype(BF16)
            dz_ref[:, sl] = (dy * o * (sig * (1.0 + z * (1.0 - sig)))).astype(BF16)
            delta = jnp.sum(dob.astype(F32) * o, axis=1, keepdims=True)
            doa_ref[:, a0:a0 + HEAD] = dob
            doa_ref[:, a0 + HEAD:a0 + AUG] = _lanes(lane, 0, _pieces(delta, -1.0), zero).astype(BF16)
            qab_ref[:, a0:a0 + HEAD] = qa_ref[:, a0:a0 + HEAD]
            qab_ref[:, a0 + HEAD:a0 + AUG] = _lanes(
                lane, 6, _pieces(lse_ref[:, h:h + 1], -1.0), qa_ref[:, a0 + HEAD:a0 + AUG].astype(F32)).astype(BF16)

    row = pl.BlockSpec((tr, HW), lambda i: (i, 0))
    aug = pl.BlockSpec((tr, H * AUG), lambda i: (i, 0))
    return pl.pallas_call(
        body, name=name, grid=(S // tr,),
        in_specs=[row, row, pl.BlockSpec((tr, HW), lambda i: (i, 3)), aug, pl.BlockSpec((tr, LANES), lambda i: (i, 0))],
        out_specs=[aug, row, aug],
        out_shape=[jax.ShapeDtypeStruct((S, H * AUG), BF16), jax.ShapeDtypeStruct((S, HW), BF16),
                   jax.ShapeDtypeStruct((S, H * AUG), BF16)],
        compiler_params=_params("arbitrary"),
    )(dy, o, proj, qa, lse)


def _attn_bwd(name, qab, doa, ka, va, H, tq, carried=None):
    S = qab.shape[0]
    nq = S // tq
    tw = 2 * tq if S % (2 * tq) == 0 else tq
    sums, owner = carried if carried else ([], None)
    nt = len(sums)

    def body(*refs):
        q_ref, do_ref, k_ref, v_ref = refs[:4]
        dq_ref, dk_ref, dv_ref = refs[4 + nt:7 + nt]
        j = pl.program_id(1)
        if carried:
            mine, start, wait = _scatter_copies(refs[4:4 + nt], refs[7 + nt:7 + 2 * nt], *refs[7 + 2 * nt:], owner)
            hd = pl.program_id(0)

            @pl.when(jnp.logical_and(mine, jnp.logical_and(hd == 0, j == 0)))
            def _():
                start()

        @pl.when(j == 0)
        def _():
            dq_ref[...] = jnp.zeros_like(dq_ref)

        k = k_ref[...]
        v = v_ref[...]

        def step(i, carry, masked):
            dk_acc, dv_acc = carry
            off = pl.multiple_of(i * tw, tw)
            q = q_ref[pl.ds(off, tw), :]
            do = do_ref[pl.ds(off, tw), :]
            st = lax.dot_general(k, q, _NT, preferred_element_type=F32)
            if masked:
                key = j * tq + lax.broadcasted_iota(jnp.int32, (tq, tw), 0)
                qry = i * tw + lax.broadcasted_iota(jnp.int32, (tq, tw), 1)
                st = jnp.where(qry >= key, st, -jnp.inf)
            pt = jnp.exp2(st)
            dst = pt * lax.dot_general(v, do, _NT, preferred_element_type=F32)
            dsb = dst.astype(BF16)
            dv_acc = dv_acc + jnp.dot(pt.astype(BF16), do[:, :HEAD], preferred_element_type=F32)
            dk_acc = dk_acc + jnp.dot(dsb, q, preferred_element_type=F32)
            dq_ref[pl.ds(off, tw), :] += lax.dot_general(dsb, k, _TN, preferred_element_type=F32)
            return dk_acc, dv_acc

        first = (j * tq) // tw
        carry = step(first, (jnp.zeros((tq, AUG), F32), jnp.zeros((tq, HEAD), F32)), True)
        dk_acc, dv_acc = lax.fori_loop(first + 1, S // tw, lambda i, c: step(i, c, False), carry)
        dk_ref[...] = dk_acc
        dv_ref[...] = dv_acc
        if carried:
            @pl.when(jnp.logical_and(mine, jnp.logical_and(hd == H - 1, j == nq - 1)))
            def _():
                wait()

    full = pl.BlockSpec((S, AUG), lambda h, j: (0, h))
    blk = pl.BlockSpec((tq, AUG), lambda h, j: (j, h))
    out = pl.pallas_call(
        body, name=name, grid=(H, nq),
        in_specs=[full, full, blk, blk] + [HBM_SPEC] * nt,
        out_specs=[full, blk, pl.BlockSpec((tq, HEAD), lambda h, j: (j, h))] + [HBM_SPEC] * nt,
        out_shape=[jax.ShapeDtypeStruct((S, H * AUG), F32), jax.ShapeDtypeStruct((S, H * AUG), F32),
                   jax.ShapeDtypeStruct((S, H * HEAD), F32)] + [jax.ShapeDtypeStruct(s.shape, F32) for s in sums],
        scratch_shapes=[pltpu.SemaphoreType.DMA((3 * nt,))] * 2 if carried else [],
        compiler_params=_params("arbitrary", "arbitrary"),
    )(qab, doa, ka, va, *sums)
    return out[0], out[1], out[2], list(out[3:])


def _qk_bwd(name, proj, dqa, dka, dv, dz, q_gain, k_gain, H):
    S = proj.shape[0]
    HW = H * HEAD
    tr = _tile(S, 256, unit=8)
    scale = HEAD ** -0.5
    factors = (scale, 1.0 / LOG2E)

    def body(q_ref, k_ref, dq_ref, dk_ref, dv_ref, dz_ref, gq_ref, gk_ref, dp_ref, dgq_ref, dgk_ref, dc_ref):
        i = pl.program_id(0)

        @pl.when(i == 0)
        def _():
            dgq_ref[...] = jnp.zeros_like(dgq_ref)
            dgk_ref[...] = jnp.zeros_like(dgk_ref)

        for n, (src, dsrc, gain, dgain) in enumerate(((q_ref, dq_ref, gq_ref, dgq_ref), (k_ref, dk_ref, gk_ref, dgk_ref))):
            acc = jnp.zeros((1, HEAD), F32)
            for h in range(H):
                sl = slice(h * HEAD, (h + 1) * HEAD)
                t = src[:, sl].astype(F32)
                r = lax.rsqrt(jnp.mean(t * t, axis=-1, keepdims=True) + EPS)
                that = t * r
                dn = dsrc[:, h * AUG:h * AUG + HEAD] * factors[n]
                acc = acc + jnp.sum(dn * that, axis=0, keepdims=True)
                dhat = dn * gain[...]
                dt = r * (dhat - that * jnp.mean(dhat * that, axis=-1, keepdims=True))
                dp_ref[:, n * HW + h * HEAD:n * HW + (h + 1) * HEAD] = dt.astype(BF16)
            dgain[...] += acc
        dp_ref[:, 2 * HW:3 * HW] = dv_ref[...].astype(BF16)
        dp_ref[:, 3 * HW:] = dz_ref[...]
        lane = lax.broadcasted_iota(jnp.int32, (tr, LANES), 1)
        dc = jnp.zeros((tr, LANES), F32)
        for h in range(H):
            qs = dq_ref[:, h * AUG + K_SUM_LANE:h * AUG + K_SUM_LANE + 1]
            ks = dk_ref[:, h * AUG + Q_SUM_LANE:h * AUG + Q_SUM_LANE + 1]
            dc = jnp.where(lane == h, qs - ks, dc)
        dc_ref[...] = dc

    col = lambda c: pl.BlockSpec((tr, HW), lambda i: (i, c))
    row = col(0)
    aug = pl.BlockSpec((tr, H * AUG), lambda i: (i, 0))
    vec = pl.BlockSpec((1, HEAD), lambda i: (0, 0))
    return pl.pallas_call(
        body, name=name, grid=(S // tr,),
        in_specs=[col(0), col(1), aug, aug, row, row, vec, vec],
        out_specs=[pl.BlockSpec((tr, 4 * HW), lambda i: (i, 0)), vec, vec, pl.BlockSpec((tr, LANES), lambda i: (i, 0))],
        out_shape=[jax.ShapeDtypeStruct((S, 4 * HW), BF16), jax.ShapeDtypeStruct((1, HEAD), F32),
                   jax.ShapeDtypeStruct((1, HEAD), F32), jax.ShapeDtypeStruct((S, LANES), F32)],
        compiler_params=_params("arbitrary"),
    )(proj, proj, dqa, dka, dv, dz, q_gain.reshape(1, HEAD), k_gain.reshape(1, HEAD))


def _row_tile(R, C, budget_bytes=1 << 20):
    cap = max(8, budget_bytes // (4 * C))
    t = (min(cap, R) // 8) * 8
    while t >= 8:
        if R % t == 0:
            return t
        t -= 8
    return R


def _add_if(name, a, b, active):
    R, C = a.shape
    tr = _row_tile(R, C)

    def body(act_ref, a_ref, b_ref, o_ref):
        @pl.when(act_ref[0] != 0)
        def _():
            o_ref[...] = a_ref[...] + b_ref[...]

    blk = pl.BlockSpec((tr, C), lambda i, s: (i * s[0], 0))
    grid_spec = pltpu.PrefetchScalarGridSpec(
        num_scalar_prefetch=1, grid=(R // tr,), in_specs=[blk, blk], out_specs=blk)
    return pl.pallas_call(
        body, name=name, grid_spec=grid_spec, out_shape=jax.ShapeDtypeStruct((R, C), F32),
        compiler_params=_params("arbitrary"),
    )(active, a, b)


def _sum_slots(name, slots, active):
    n, R, C = slots.shape
    tr = _row_tile(R, C, budget_bytes=(1 << 20) // 2)

    def body(act_ref, s_ref, o_ref):
        @pl.when(act_ref[0] != 0)
        def _():
            acc = s_ref[0]
            for k in range(1, n):
                acc = acc + s_ref[k]
            o_ref[...] = acc

    grid_spec = pltpu.PrefetchScalarGridSpec(
        num_scalar_prefetch=1, grid=(R // tr,),
        in_specs=[pl.BlockSpec((n, tr, C), lambda i, s: (0, i * s[0], 0))],
        out_specs=pl.BlockSpec((tr, C), lambda i, s: (i * s[0], 0)))
    return pl.pallas_call(
        body, name=name, grid_spec=grid_spec, out_shape=jax.ShapeDtypeStruct((R, C), F32),
        compiler_params=_params("arbitrary"),
    )(active, slots)


def _adamw(name, w, g, m, v):
    R, C = w.shape
    tr = _row_tile(R, C, budget_bytes=(1 << 20) // 2)
    c1 = 1.0 - ADAM_B1 ** ADAM_STEP
    c2 = 1.0 - ADAM_B2 ** ADAM_STEP

    def body(w_ref, g_ref, m_ref, v_ref, d_ref, nm_ref, nv_ref):
        gv = g_ref[...]
        nm = ADAM_B1 * m_ref[...] + (1.0 - ADAM_B1) * gv
        nv = ADAM_B2 * v_ref[...] + (1.0 - ADAM_B2) * (gv * gv)
        m_hat = nm / c1
        v_hat = nv / c2
        d_ref[...] = -ADAM_LR * (m_hat / (jnp.sqrt(v_hat) + ADAM_EPS) + ADAM_WD * w_ref[...])
        nm_ref[...] = nm
        nv_ref[...] = nv

    blk = pl.BlockSpec((tr, C), lambda i: (i, 0))
    return pl.pallas_call(
        body, name=name, grid=(R // tr,), in_specs=[blk] * 4, out_specs=[blk] * 3,
        out_shape=[jax.ShapeDtypeStruct((R, C), F32)] * 3,
        compiler_params=_params("arbitrary"),
    )(w, g, m, v)


def _place():
    x, y, c = lax.axis_index("x"), lax.axis_index("y"), lax.axis_index("c")
    chips = [(1 - x, y), (x, 1 - y), (1 - x, 1 - y)]
    return x, y, c, chips


def _gather_copies(ins, outs, sems, layer):
    s_send, s_recv, f_send, f_recv = sems
    nt = len(ins)
    x, y, c, chips = _place()
    me = 2 * x + y
    ids = [2 * cx + cy for cx, cy in chips]
    pairs = [(t, k) for t in range(nt) for k in range(3)]

    def over_ici(t, k, block):
        return pltpu.make_async_remote_copy(
            src_ref=ins[t], dst_ref=outs[t].at[block], send_sem=s_send.at[3 * t + k],
            recv_sem=s_recv.at[3 * t + k], device_id=(*chips[k], layer), device_id_type=MESH)

    def over_d2d(t, k):
        blk = outs[t].at[ids[k]]
        return pltpu.make_async_remote_copy(
            src_ref=blk, dst_ref=blk, send_sem=f_send.at[3 * t + k], recv_sem=f_recv.at[3 * t + k],
            device_id=(x, y, 1 - c), device_id_type=MESH)

    def start():
        for t, k in pairs:
            over_ici(t, k, me).start()

    def finish():
        @pl.when(c == layer)
        def _():
            for t, k in pairs:
                over_ici(t, k, ids[k]).wait_recv()
                over_d2d(t, k).start()
            for t, k in pairs:
                over_ici(t, k, me).wait_send()
                over_d2d(t, k).wait_send()

        @pl.when(c != layer)
        def _():
            for t, k in pairs:
                over_d2d(t, k).wait_recv()

    return c == layer, start, finish


def _gather_out_shapes(blocks):
    return [jax.ShapeDtypeStruct((N_CHIPS,) + b.shape, b.dtype) for b in blocks]


def _gather_sems(n):
    return [pltpu.SemaphoreType.DMA((3 * n,))] * 4


def _gather_weights(name, blocks, layer):
    nt = len(blocks)

    def body(*refs):
        mine, start, finish = _gather_copies(refs[:nt], refs[nt:2 * nt], refs[2 * nt:], layer)

        @pl.when(mine)
        def _():
            start()

        finish()

    return pl.pallas_call(
        body, name=name, out_shape=_gather_out_shapes(blocks),
        in_specs=[HBM_SPEC] * nt, out_specs=[HBM_SPEC] * nt, scratch_shapes=_gather_sems(nt),
    )(*blocks)


def _send_to_owner(name, grads, owner):
    nt = len(grads)

    def body(*refs):
        ins, outs = refs[:nt], refs[nt:2 * nt]
        s_send, s_recv = refs[2 * nt:]
        x, y, c, _ = _place()

        def copy(t):
            return pltpu.make_async_remote_copy(
                src_ref=ins[t], dst_ref=outs[t], send_sem=s_send.at[t], recv_sem=s_recv.at[t],
                device_id=(x, y, owner), device_id_type=MESH)

        @pl.when(c != owner)
        def _():
            for t in range(nt):
                copy(t).start()
            for t in range(nt):
                copy(t).wait_send()

        @pl.when(c == owner)
        def _():
            for t in range(nt):
                copy(t).wait_recv()

    return pl.pallas_call(
        body, name=name,
        out_shape=[jax.ShapeDtypeStruct(g.shape, F32) for g in grads],
        in_specs=[HBM_SPEC] * nt, out_specs=[HBM_SPEC] * nt,
        scratch_shapes=[pltpu.SemaphoreType.DMA((nt,))] * 2,
    )(*grads)


def _scatter_copies(ins, outs, s_send, s_recv, owner):
    nt = len(ins)
    x, y, c, chips = _place()
    me = 2 * x + y
    ids = [2 * cx + cy for cx, cy in chips]
    pairs = [(t, k) for t in range(nt) for k in range(3)]

    def copy(t, k, slot):
        return pltpu.make_async_remote_copy(
            src_ref=ins[t].at[ids[k]], dst_ref=outs[t].at[slot], send_sem=s_send.at[3 * t + k],
            recv_sem=s_recv.at[3 * t + k], device_id=(*chips[k], owner), device_id_type=MESH)

    def start():
        for t, k in pairs:
            copy(t, k, me).start()

    def wait():
        for t, k in pairs:
            copy(t, k, ids[k]).wait()

    return c == owner, start, wait


def _carried_exchange(kind, ins, outs, sems, layer):
    if kind == "gather":
        return _gather_copies(ins, outs, sems, layer)
    mine, start, wait = _scatter_copies(ins, outs, *sems, layer)
    return mine, start, lambda: pl.when(mine)(wait)


def _carried_out_shapes(kind, arrays):
    if kind == "gather":
        return _gather_out_shapes(arrays)
    return [jax.ShapeDtypeStruct(a.shape, F32) for a in arrays]


def _carried_sems(kind, n):
    if kind is None:
        return []
    return _gather_sems(n) if kind == "gather" else [pltpu.SemaphoreType.DMA((3 * n,))] * 2


def _last_exchanges(reduced, small):
    nt = len(reduced)

    def body(*refs):
        ins, small_ref = refs[:nt], refs[nt]
        outs, slots_ref = refs[nt + 1:2 * nt + 1], refs[2 * nt + 1]
        s_send, s_recv, a_send, a_recv = refs[2 * nt + 2:]
        x, y, c, _ = _place()
        flips = [(fx, fy, fc) for fx in (0, 1) for fy in (0, 1) for fc in (0, 1)][1:]

        def peer(f):
            return tuple(1 - a if flip else a for a, flip in zip((x, y, c), f))

        def slot(p):
            return 4 * p[0] + 2 * p[1] + p[2]

        def swap(t):
            return pltpu.make_async_remote_copy(
                src_ref=ins[t], dst_ref=outs[t], send_sem=s_send.at[t], recv_sem=s_recv.at[t],
                device_id=(x, y, 1 - c), device_id_type=MESH)

        def to_all(k, owner):
            return pltpu.make_async_remote_copy(
                src_ref=small_ref, dst_ref=slots_ref.at[slot(owner)], send_sem=a_send.at[k], recv_sem=a_recv.at[k],
                device_id=peer(flips[k]), device_id_type=MESH)

        for t in range(nt):
            swap(t).start()
        for k in range(7):
            to_all(k, (x, y, c)).start()
        for t in range(nt):
            swap(t).wait()
        for k in range(7):
            to_all(k, peer(flips[k])).wait()

    out = pl.pallas_call(
        body, name="last_exchanges",
        out_shape=[jax.ShapeDtypeStruct(r.shape, F32) for r in reduced]
        + [jax.ShapeDtypeStruct((8,) + small.shape, F32)],
        in_specs=[HBM_SPEC] * (nt + 1), out_specs=[HBM_SPEC] * (nt + 1),
        scratch_shapes=[pltpu.SemaphoreType.DMA((nt,))] * 2 + [pltpu.SemaphoreType.DMA((7,))] * 2,
    )(*reduced, small)
    return list(out[:nt]), out[nt]


def _rows128(a):
    flat = a.reshape(-1)
    rows = -(-flat.shape[0] // LANES)
    rows8 = -(-rows // 8) * 8
    flat = jnp.pad(flat, (0, rows8 * LANES - flat.shape[0]))
    return flat.reshape(rows8, LANES)


def _pack(parts):
    return jnp.concatenate([_rows128(p) for p in parts], axis=0)


def _unpack(buf, shapes):
    out, r = [], 0
    for shp in shapes:
        n = int(np.prod(shp))
        rows8 = -(-(-(-n // LANES)) // 8) * 8
        out.append(buf[r:r + rows8].reshape(-1)[:n].reshape(shp))
        r += rows8
    return out


def kernel(x, a_norm_g, a_w_in, a_v_norm_g, a_w_s, a_b_s, a_w_out, b_norm_g, b_w_in, b_f_bias, b_q_norm_g, b_k_norm_g, b_w_out, loss_target, m_a_norm_g, m_a_w_in, m_a_v_norm_g, m_a_w_s, m_a_b_s, m_a_w_out, m_b_norm_g, m_b_w_in, m_b_f_bias, m_b_q_norm_g, m_b_k_norm_g, m_b_w_out, v_a_norm_g, v_a_w_in, v_a_v_norm_g, v_a_w_s, v_a_b_s, v_a_w_out, v_b_norm_g, v_b_w_in, v_b_f_bias, v_b_q_norm_g, v_b_k_norm_g, v_b_w_out):
    xs = x[0]
    target = loss_target[0]
    S, D = xs.shape
    n_layers = a_w_in.shape[0]
    assert n_layers == 2
    W = a_v_norm_g.shape[1]
    G = a_w_s.shape[1]
    H = b_f_bias.shape[1]
    HW = H * HEAD
    tq_fwd = _tile(S, 512)
    tq_bwd = _tile(S, 512)
    core = lax.axis_index("c")
    chip = 2 * lax.axis_index("x") + lax.axis_index("y")

    own = dict(a_w_in=a_w_in.astype(BF16), a_w_out=a_w_out.astype(BF16), b_w_in=b_w_in.astype(BF16),
               b_w_out=b_w_out.astype(BF16), b_norm_g=b_norm_g.reshape(n_layers, 1, -1))
    cb = b_w_in.shape[2]
    w_ain, w_aout, w_bmain, w_bf, w_bout, b_norm_full = ([None] * n_layers for _ in range(6))

    def blocks_of(tensors, layer):
        return [own[n][layer] for n in tensors]

    def take(tensors, layer, arrived):
        for n, got, mine in zip(tensors, arrived, blocks_of(tensors, layer)):
            full = lax.dynamic_update_slice(got, mine[None], (chip, 0, 0))
            if n == "a_w_in":
                w_ain[layer] = full
            elif n == "a_w_out":
                w_aout[layer] = full
            elif n == "b_w_out":
                w_bout[layer] = full
            elif n == "b_norm_g":
                b_norm_full[layer] = full.reshape(D)
            else:
                cols = jnp.transpose(full, (1, 0, 2)).reshape(D, N_CHIPS * cb)
                w_bmain[layer] = cols[:, :4 * HW]
                w_bf[layer] = jnp.pad(cols[:, 4 * HW:], ((0, 0), (0, LANES - H)))

    first_a = ("a_w_in", "a_w_out")
    first_b = ("b_w_in", "b_w_out", "b_norm_g")
    take(first_a, 0, _gather_weights("gather_a0", blocks_of(first_a, 0), 0))
    causal = jnp.tril(jnp.ones((CHUNK, CHUNK), dtype=bool))
    wc = jnp.where(causal[None, None], a_w_s, 0).astype(BF16)
    wc_t = jnp.swapaxes(wc, 2, 3)
    bs_t = jnp.swapaxes(a_b_s, 1, 2)
    f_bias = jnp.pad(b_f_bias, ((0, 0), (0, LANES - H))).reshape(n_layers, 1, LANES)

    def view_ain(l):
        return _View(w_ain[l], "col")

    def view_aout(l):
        return _View(w_aout[l], "row")

    def view_bout(l):
        return _View(w_bout[l], "row")

    saved = []
    cur = xs
    for i in range(2 * n_layers):
        l = i // 2
        if i % 2 == 0:
            h, h_t = _rmsnorm_fwd(f"a{l}_norm", cur, a_norm_g[l])
            if i == 0:
                p, arrived = _matmul(f"a{l}_in", _View(h), view_ain(l), out_dtype=BF16, tm=1024, tn=1024, tk=2048,
                                     carried=("gather", blocks_of(first_b, 0), 0))
                take(first_b, 0, arrived)
            else:
                p = _matmul(f"a{l}_in", _View(h), view_ain(l), out_dtype=BF16, tm=1024, tn=1024, tk=2048)
            y = _gate_fwd(f"a{l}_gate", p, a_v_norm_g[l], wc[l], bs_t[l])
            nxt = _matmul(f"a{l}_out", _View(y), view_aout(l), tm=1024, tn=1024, tk=1024, residual=cur)
            saved.append((cur, h_t, p, y))
        else:
            h, h_t = _rmsnorm_fwd(f"b{l}_norm", cur, b_norm_full[l])
            proj = _matmul(f"b{l}_in", _View(h), _View(w_bmain[l]), out_dtype=BF16, tm=1024, tn=1024, tk=2048)
            f = _matmul(f"b{l}_inf", _View(h), _View(w_bf[l]), tm=1024, tn=LANES, tk=2048)
            cum = _fox_cum(f"b{l}_cum", f, f_bias[l])
            qa, ka, va = _qkv_prep(f"b{l}_qkv", proj, cum, b_q_norm_g[l], b_k_norm_g[l], H)
            if i == 1:
                o, y, lse, arrived = _attn_fwd(f"b{l}_attn", qa, ka, va, proj, H, tq_fwd,
                                               carried=("gather", blocks_of(first_a + first_b, 1), 1))
                take(first_a + first_b, 1, arrived)
            else:
                o, y, lse, _ = _attn_fwd(f"b{l}_attn", qa, ka, va, proj, H, tq_fwd)
            nxt = _matmul(f"b{l}_out", _View(y), view_bout(l), tm=1024, tn=1024, tk=1024, residual=cur)
            saved.append((cur, h_t, proj, f, qa, ka, va, o, y, lse))
        cur = nxt

    g, gb, lcols = _loss_grad(cur, target)
    loss = lax.psum(0.5 * jnp.sum(lcols) / D, ("x", "y", "c"))

    big = {"a_w_in": [None] * n_layers, "a_w_out": [None] * n_layers,
           "b_w_in": [None] * n_layers, "b_w_out": [None] * n_layers}
    small = {k: [None] * n_layers for k in
             ("a_norm_g", "a_v_norm_g", "a_w_s", "a_b_s", "b_norm_g", "b_f_bias", "b_q_norm_g", "b_k_norm_g")}
    names = ["a_w_in", "a_w_out", "b_w_in", "b_w_out"]
    reduced = [{} for _ in range(n_layers)]

    def chip_sums_of(tag, tensors, layer):
        mine = [big[n][layer] for n in tensors]
        got = _send_to_owner(f"to_owner{tag}", mine, layer)
        active = (core == layer).astype(jnp.int32).reshape(1)
        sums = []
        for n, a, b in zip(tensors, mine, got):
            shp = a.shape
            flat = lambda t: t.reshape(shp[0] * shp[1], shp[2])
            sums.append(_add_if(f"chipsum{tag}_{n}", flat(a), flat(b), active).reshape(shp))
        return sums, active

    def reduce_slots(tag, tensors, layer, sums, got, active):
        for n, g_, s_ in zip(tensors, got, sums):
            slots = lax.dynamic_update_slice(g_, lax.dynamic_index_in_dim(s_, chip, keepdims=True), (chip, 0, 0))
            reduced[layer][n] = _sum_slots(f"reduce{tag}_{n}", slots, active)

    for i in reversed(range(2 * n_layers)):
        l = i // 2
        if i % 2 == 0:
            x_in, h_t, p, y = saved[i]
            dy = _matmul(f"a{l}_dy", _View(gb), view_aout(l), tb=True, out_dtype=BF16, tm=1024, tn=1024, tk=2048)
            d_wout = _matmul(f"a{l}_dwout", _View(y), _View(gb), ta=True, tm=2048, tn=1024, tk=1024)
            big["a_w_out"][l] = d_wout.reshape(N_CHIPS, W // N_CHIPS, D)
            dp, d_ws, d_bs, d_gv = _gate_bwd(f"a{l}_dgate", p, dy, a_v_norm_g[l], wc[l], wc_t[l], bs_t[l])
            dims_dh = dict(tb=True, tm=1024, tn=1024, tk=3072)
            dims_dwin = dict(tm=2048, tn=1024, tk=1024, out_colblocks=N_CHIPS)
            if i == 0:
                early, late = ["b_w_in", "b_w_out", "a_w_out"], ["a_w_in"]
                sums_e, active0 = chip_sums_of("0e", early, 0)
                d_win, got_e = _matmul(f"a{l}_dwin", _View(h_t), _View(dp), carried=("scatter", sums_e[:2], 0),
                                       **dims_dwin)
                big["a_w_in"][l] = d_win
                sums_l, _ = chip_sums_of("0l", late, 0)
                dh, got_l = _matmul(f"a{l}_dh", _View(dp), view_ain(l), carried=("scatter", sums_e[2:] + sums_l, 0),
                                    **dims_dh)
                reduce_slots("0e", early, 0, sums_e, got_e + got_l[:1], active0)
                reduce_slots("0l", late, 0, sums_l, got_l[1:], active0)
            else:
                dh = _matmul(f"a{l}_dh", _View(dp), view_ain(l), **dims_dh)
                d_win = _matmul(f"a{l}_dwin", _View(h_t), _View(dp), **dims_dwin)
                big["a_w_in"][l] = d_win
            g, gb, d_gn = _rmsnorm_bwd(f"a{l}_dnorm", x_in, a_norm_g[l], dh, g)
            small["a_norm_g"][l] = d_gn.reshape(D)
            small["a_v_norm_g"][l] = d_gv.reshape(W)
            small["a_w_s"][l] = jnp.where(causal[None], d_ws, 0.0)
            small["a_b_s"][l] = d_bs[:, :G].T
        else:
            x_in, h_t, proj, f, qa, ka, va, o, y, lse = saved[i]
            dy = _matmul(f"b{l}_dy", _View(gb), view_bout(l), tb=True, out_dtype=BF16, tm=1024, tn=1024, tk=2048)
            d_wout = _matmul(f"b{l}_dwout", _View(y), _View(gb), ta=True, tm=2048, tn=1024, tk=1024)
            lse_lanes = jnp.pad(lse.reshape(H, S).T, ((0, 0), (0, LANES - H)))
            doa, dz, qab = _attn_bwd_prep(f"b{l}_dprep", dy, o, proj, qa, lse_lanes, H)
            early = (i == 1)
            if early:
                sums1, active1 = chip_sums_of("1", names, 1)
            dqa, dka, dv, got1 = _attn_bwd(f"b{l}_dattn", qab, doa, ka, va, H, tq_bwd,
                                           carried=(sums1, 1) if early else None)
            if early:
                reduce_slots("1", names, 1, sums1, got1, active1)
            dproj, d_gq, d_gk, dcum = _qk_bwd(f"b{l}_dqk", proj, dqa, dka, dv, dz, b_q_norm_g[l], b_k_norm_g[l], H)
            df, d_fb = _fox_cum_bwd(f"b{l}_dcum", dcum, f, f_bias[l])
            dh_f = _matmul(f"b{l}_dhf", _View(df), _View(w_bf[l]), tb=True, tm=1024, tn=1024, tk=LANES)
            dh = _matmul(f"b{l}_dh", _View(dproj), _View(w_bmain[l]), tb=True, tm=1024, tn=1024, tk=2048,
                         residual=dh_f)
            d_wmain = _matmul(f"b{l}_dwin", _View(h_t), _View(dproj), tm=2048, tn=1024, tk=1024)
            d_wf = _matmul(f"b{l}_dwinf", _View(h_t), _View(df), tm=2048, tn=LANES, tk=1024)
            d_win = jnp.concatenate([d_wmain, d_wf[:, :H]], axis=1)
            g, gb, d_gn = _rmsnorm_bwd(f"b{l}_dnorm", x_in, b_norm_full[l], dh, g)
            big["b_w_in"][l] = jnp.transpose(d_win.reshape(D, N_CHIPS, cb), (1, 0, 2))
            big["b_w_out"][l] = d_wout.reshape(N_CHIPS, HW // N_CHIPS, D)
            small["b_norm_g"][l] = d_gn.reshape(D)
            small["b_f_bias"][l] = d_fb[0, :H]
            small["b_q_norm_g"][l] = d_gq.reshape(HEAD)
            small["b_k_norm_g"][l] = d_gk.reshape(HEAD)
    grad_x = g[None]

    small_names = ["a_norm_g", "a_v_norm_g", "a_w_s", "a_b_s", "b_norm_g", "b_f_bias", "b_q_norm_g", "b_k_norm_g"]
    small_parts = [jnp.stack(small[n]) for n in small_names]
    small_mine = _pack(small_parts)
    mine = [jnp.where(core == 0, reduced[0][n], reduced[1][n]) for n in names]
    others, small_all = _last_exchanges(mine, small_mine)
    grads = {n: jnp.where(core == 0, jnp.stack([m_, o_]), jnp.stack([o_, m_]))
             for n, m_, o_ in zip(names, mine, others)}
    small_all = lax.dynamic_update_slice(small_all, small_mine[None], (2 * chip + core, 0, 0))
    small_sum = _sum_slots("reduce_small", small_all, jnp.ones((1,), jnp.int32))
    for n, a in zip(small_names, _unpack(small_sum, [p.shape for p in small_parts])):
        grads[n] = a
    nb = b_norm_g.shape[1]
    grads["b_norm_g"] = lax.dynamic_slice_in_dim(grads["b_norm_g"], chip * nb, nb, axis=1)

    weights = dict(a_norm_g=a_norm_g, a_w_in=a_w_in, a_v_norm_g=a_v_norm_g, a_w_s=a_w_s, a_b_s=a_b_s,
                   a_w_out=a_w_out, b_norm_g=b_norm_g, b_w_in=b_w_in, b_f_bias=b_f_bias,
                   b_q_norm_g=b_q_norm_g, b_k_norm_g=b_k_norm_g, b_w_out=b_w_out)
    mom1 = dict(a_norm_g=m_a_norm_g, a_w_in=m_a_w_in, a_v_norm_g=m_a_v_norm_g, a_w_s=m_a_w_s, a_b_s=m_a_b_s,
                a_w_out=m_a_w_out, b_norm_g=m_b_norm_g, b_w_in=m_b_w_in, b_f_bias=m_b_f_bias,
                b_q_norm_g=m_b_q_norm_g, b_k_norm_g=m_b_k_norm_g, b_w_out=m_b_w_out)
    mom2 = dict(a_norm_g=v_a_norm_g, a_w_in=v_a_w_in, a_v_norm_g=v_a_v_norm_g, a_w_s=v_a_w_s, a_b_s=v_a_b_s,
                a_w_out=v_a_w_out, b_norm_g=v_b_norm_g, b_w_in=v_b_w_in, b_f_bias=v_b_f_bias,
                b_q_norm_g=v_b_q_norm_g, b_k_norm_g=v_b_k_norm_g, b_w_out=v_b_w_out)
    order = ["a_norm_g", "a_w_in", "a_v_norm_g", "a_w_s", "a_b_s", "a_w_out", "b_norm_g", "b_w_in", "b_f_bias",
             "b_q_norm_g", "b_k_norm_g", "b_w_out"]
    delta, new_m, new_v = {}, {}, {}
    for n in names:
        shp = weights[n].shape
        flat = lambda a: a.reshape(shp[0] * shp[1], shp[2])
        d, nm, nv = _adamw(f"adamw_{n}", flat(weights[n]), flat(grads[n]), flat(mom1[n]), flat(mom2[n]))
        delta[n], new_m[n], new_v[n] = d.reshape(shp), nm.reshape(shp), nv.reshape(shp)
    small_shapes = [weights[n].shape for n in small_names]
    pack_w, pack_g, pack_m, pack_v = (_pack([d[n] for n in small_names]) for d in (weights, grads, mom1, mom2))
    d, nm, nv = _adamw("adamw_small", pack_w, pack_g, pack_m, pack_v)
    for dst, buf in ((delta, d), (new_m, nm), (new_v, nv)):
        for n, a in zip(small_names, _unpack(buf, small_shapes)):
            dst[n] = a

    return (loss, grad_x, *[grads[n] for n in order], *[delta[n] for n in order],
            *[new_m[n] for n in order], *[new_v[n] for n in order])
```

```python
import functools
import math

import numpy as np
import jax
import jax.numpy as jnp
from jax import lax
from jax.experimental import pallas as pl
from jax.experimental.pallas import tpu as pltpu

F32 = jnp.float32
BF16 = jnp.bfloat16
MESH = pl.DeviceIdType.MESH

EPS = 1e-6
CHUNK = 128
HEAD = 128
LANES = 128
N_CHIPS = 4
VMEM_LIMIT = 56 * 1024 * 1024

ADAM_LR = 0.001
ADAM_B1 = 0.9
ADAM_B2 = 0.999
ADAM_EPS = 1e-08
ADAM_WD = 0.01
ADAM_STEP = 10

_NT = (((1,), (1,)), ((), ()))
_TN = (((0,), (0,)), ((), ()))
_GELU_C = math.sqrt(2.0 / math.pi)

HBM_SPEC = pl.BlockSpec(memory_space=pltpu.HBM)


def _params(*sem):
    return pltpu.CompilerParams(dimension_semantics=sem, vmem_limit_bytes=VMEM_LIMIT)


def _tile(dim, pref, unit=LANES):
    t = (min(pref, dim) // unit) * unit
    while t >= unit:
        if dim % t == 0:
            return t
        t -= unit
    return dim


def _gelu(x):
    return 0.5 * x * (1.0 + jnp.tanh(_GELU_C * (x + 0.044715 * (x * x * x))))


def _gelu_and_grad(x):
    x2 = x * x
    t = jnp.tanh(_GELU_C * (x + 0.044715 * (x2 * x)))
    val = 0.5 * x * (1.0 + t)
    grad = 0.5 * (1.0 + t) + 0.5 * x * (1.0 - t * t) * (_GELU_C * (1.0 + 3.0 * 0.044715 * x2))
    return val, grad


def _sigmoid(x):
    return 1.0 / (1.0 + jnp.exp(-x))


class _View:
    def __init__(self, arr, kind="2d", lead=()):
        self.arr, self.kind, self.lead = arr, kind, tuple(lead)
        shp = arr.shape[len(self.lead):]
        if kind == "2d":
            self.R, self.C = shp
        elif kind == "col":
            self.nb, self.R, self.cb = shp
            self.C = self.nb * self.cb
        else:
            self.nb, self.rb, self.C = shp
            self.R = self.nb * self.rb

    def fit(self, tr, tc):
        if self.kind == "col":
            tc = _tile(self.cb, tc)
        elif self.kind == "row":
            tr = _tile(self.rb, tr, unit=8)
        return tr, tc

    def spec(self, tr, tc, rc_of_grid):
        lead = self.lead
        sq = (None,) * len(lead)
        if self.kind == "2d":
            return pl.BlockSpec(sq + (tr, tc), lambda *g: lead + tuple(rc_of_grid(*g)))
        if self.kind == "col":
            q = self.cb // tc

            def im(*g):
                r, c = rc_of_grid(*g)
                return lead + (c // q, r, c % q)

            return pl.BlockSpec(sq + (None, tr, tc), im)
        q = self.rb // tr

        def im(*g):
            r, c = rc_of_grid(*g)
            return lead + (r // q, r % q, c)

        return pl.BlockSpec(sq + (None, tr, tc), im)


def _matmul(name, a, b, *, ta=False, tb=False, out_dtype=F32, tm=1024, tn=1024, tk=1024,
            out_colblocks=None, residual=None, carried=None):
    M, K = (a.C, a.R) if ta else (a.R, a.C)
    N, K2 = (b.R, b.C) if tb else (b.C, b.R)
    assert K == K2, (name, K, K2)
    tm, tn, tk = _tile(M, tm), _tile(N, tn), _tile(K, tk)
    if ta:
        tk, tm = a.fit(tk, tm)
    else:
        tm, tk = a.fit(tm, tk)
    if tb:
        tn, tk2 = b.fit(tn, tk)
    else:
        tk2, tn = b.fit(tk, tn)
    if tk2 != tk:
        tk = min(tk, tk2)
        if ta:
            tk, tm = a.fit(tk, tm)
        else:
            tm, tk = a.fit(tm, tk)
    if out_colblocks:
        tn = _tile(N // out_colblocks, tn)
    assert M % tm == 0 and N % tn == 0 and K % tk == 0, (name, M, N, K, tm, tn, tk)
    nk = K // tk
    assert nk == 1 or out_dtype == F32, name
    dims = (((0 if ta else 1,), (1 if tb else 0,)), ((), ()))
    grid = (M // tm, N // tn, nk)
    kind, g_blocks, g_layer = carried if carried else (None, [], None)
    ng = len(g_blocks)
    n_in = 2 + (residual is not None)

    def body(*refs):
        a_ref, b_ref = refs[:2]
        r_ref = refs[2] if residual is not None else None
        o_ref = refs[n_in + ng]
        k = pl.program_id(2)
        if carried:
            mine, start, finish = _carried_exchange(kind, refs[n_in:n_in + ng], refs[n_in + ng + 1:n_in + 2 * ng + 1],
                                                    refs[n_in + 2 * ng + 1:], g_layer)
            at = lambda step: functools.reduce(
                jnp.logical_and, [pl.program_id(d) == (0 if step == "first" else grid[d] - 1) for d in range(3)])

            @pl.when(jnp.logical_and(mine, at("first")))
            def _():
                start()

        def product():
            return lax.dot_general(a_ref[...], b_ref[...], dims, preferred_element_type=F32)

        if nk == 1:
            total = product()
            if r_ref is not None:
                total = total + r_ref[...]
            o_ref[...] = total.astype(out_dtype)
        else:
            @pl.when(k == 0)
            def _():
                o_ref[...] = product() + r_ref[...] if r_ref is not None else product()

            @pl.when(k > 0)
            def _():
                o_ref[...] += product()

        if carried:
            @pl.when(at("last"))
            def _():
                finish()

    a_spec = a.spec(tk, tm, lambda i, j, k: (k, i)) if ta else a.spec(tm, tk, lambda i, j, k: (i, k))
    b_spec = b.spec(tn, tk, lambda i, j, k: (j, k)) if tb else b.spec(tk, tn, lambda i, j, k: (k, j))
    in_specs, args = [a_spec, b_spec], [a.arr, b.arr]
    if residual is not None:
        in_specs.append(pl.BlockSpec((tm, tn), lambda i, j, k: (i, j)))
        args.append(residual)
    in_specs += [HBM_SPEC] * ng
    args += list(g_blocks)
    if out_colblocks:
        q = (N // out_colblocks) // tn
        out_shape = jax.ShapeDtypeStruct((out_colblocks, M, N // out_colblocks), out_dtype)
        out_spec = pl.BlockSpec((None, tm, tn), lambda i, j, k: (j // q, i, j % q))
    else:
        out_shape = jax.ShapeDtypeStruct((M, N), out_dtype)
        out_spec = pl.BlockSpec((tm, tn), lambda i, j, k: (i, j))
    out = pl.pallas_call(
        body, name=name, grid=grid, in_specs=in_specs, out_specs=[out_spec] + [HBM_SPEC] * ng,
        out_shape=[out_shape] + _carried_out_shapes(kind, g_blocks),
        scratch_shapes=_carried_sems(kind, ng),
        compiler_params=_params("arbitrary", "arbitrary", "arbitrary"),
    )(*args)
    return (out[0], list(out[1:])) if carried else out[0]


def _rmsnorm_fwd(name, x, gain):
    S, D = x.shape
    tr = _tile(S, 512)

    def body(x_ref, g_ref, h_ref, ht_ref):
        xv = x_ref[...]
        r = lax.rsqrt(jnp.mean(xv * xv, axis=-1, keepdims=True) + EPS)
        h = xv * r * g_ref[...]
        h_ref[...] = h.astype(BF16)
        ht_ref[...] = h.T.astype(BF16)

    return pl.pallas_call(
        body, name=name, grid=(S // tr,),
        in_specs=[pl.BlockSpec((tr, D), lambda i: (i, 0)), pl.BlockSpec((1, D), lambda i: (0, 0))],
        out_specs=[pl.BlockSpec((tr, D), lambda i: (i, 0)), pl.BlockSpec((D, tr), lambda i: (0, i))],
        out_shape=[jax.ShapeDtypeStruct((S, D), BF16), jax.ShapeDtypeStruct((D, S), BF16)],
        compiler_params=_params("arbitrary"),
    )(x, gain.reshape(1, D))


def _rmsnorm_bwd(name, x, gain, dh, g_res):
    S, D = x.shape
    tr = _tile(S, 256, unit=8)

    def body(x_ref, g_ref, dh_ref, res_ref, dx_ref, dxb_ref, dg_ref):
        i = pl.program_id(0)
        xv = x_ref[...]
        r = lax.rsqrt(jnp.mean(xv * xv, axis=-1, keepdims=True) + EPS)
        xhat = xv * r
        dhv = dh_ref[...]
        part = jnp.sum(dhv * xhat, axis=0, keepdims=True)

        @pl.when(i == 0)
        def _():
            dg_ref[...] = part

        @pl.when(i > 0)
        def _():
            dg_ref[...] += part

        dxhat = dhv * g_ref[...]
        dx = res_ref[...] + r * (dxhat - xhat * jnp.mean(dxhat * xhat, axis=-1, keepdims=True))
        dx_ref[...] = dx
        dxb_ref[...] = dx.astype(BF16)

    row = pl.BlockSpec((tr, D), lambda i: (i, 0))
    vec = pl.BlockSpec((1, D), lambda i: (0, 0))
    return pl.pallas_call(
        body, name=name, grid=(S // tr,), in_specs=[row, vec, row, row], out_specs=[row, row, vec],
        out_shape=[jax.ShapeDtypeStruct((S, D), F32), jax.ShapeDtypeStruct((S, D), BF16),
                   jax.ShapeDtypeStruct((1, D), F32)],
        compiler_params=_params("arbitrary"),
    )(x, gain.reshape(1, D), dh, g_res)


def _loss_grad(x, target):
    S, D = x.shape
    tr = _tile(S, 512, unit=8)

    def body(x_ref, t_ref, g_ref, gb_ref, l_ref):
        i = pl.program_id(0)
        e = x_ref[...] - t_ref[...]
        g = e * (1.0 / D)
        g_ref[...] = g
        gb_ref[...] = g.astype(BF16)
        part = jnp.sum(e * e, axis=0, keepdims=True)

        @pl.when(i == 0)
        def _():
            l_ref[...] = part

        @pl.when(i > 0)
        def _():
            l_ref[...] += part

    row = pl.BlockSpec((tr, D), lambda i: (i, 0))
    vec = pl.BlockSpec((1, D), lambda i: (0, 0))
    return pl.pallas_call(
        body, name="loss_grad", grid=(S // tr,), in_specs=[row, row], out_specs=[row, row, vec],
        out_shape=[jax.ShapeDtypeStruct((S, D), F32), jax.ShapeDtypeStruct((S, D), BF16),
                   jax.ShapeDtypeStruct((1, D), F32)],
        compiler_params=_params("arbitrary"),
    )(x, target)


def _gate_fwd(name, p, v_gain, wc, bs_t):
    S, W3 = p.shape
    W = W3 // 3
    G = wc.shape[0]
    gd = W // G

    def body(p_ref, gv_ref, wc_ref, bs_ref, y_ref):
        vg = _gelu(p_ref[:, W:2 * W].astype(F32))
        r = lax.rsqrt(jnp.mean(vg * vg, axis=-1, keepdims=True) + EPS)
        vb = (vg * r * gv_ref[...]).astype(BF16)
        zp = p_ref[:, 2 * W:].astype(F32)
        gate = _gelu(p_ref[:, :W].astype(F32)) * (zp * _sigmoid(zp))
        for g in range(G):
            sl = slice(g * gd, (g + 1) * gd)
            mixed = jnp.dot(wc_ref[g], vb[:, sl], preferred_element_type=F32) + bs_ref[:, g:g + 1]
            y_ref[:, sl] = (gate[:, sl] * mixed).astype(BF16)

    return pl.pallas_call(
        body, name=name, grid=(S // CHUNK,),
        in_specs=[pl.BlockSpec((CHUNK, W3), lambda i: (i, 0)), pl.BlockSpec((1, W), lambda i: (0, 0)),
                  pl.BlockSpec((G, CHUNK, CHUNK), lambda i: (0, 0, 0)), pl.BlockSpec((CHUNK, G), lambda i: (0, 0))],
        out_specs=pl.BlockSpec((CHUNK, W), lambda i: (i, 0)),
        out_shape=jax.ShapeDtypeStruct((S, W), BF16),
        compiler_params=_params("arbitrary"),
    )(p, v_gain.reshape(1, W), wc, bs_t)


def _gate_bwd(name, p, dy, v_gain, wc, wc_t, bs_t):
    S, W3 = p.shape
    W = W3 // 3
    G = wc.shape[0]
    gd = W // G

    def body(p_ref, dy_ref, gv_ref, wc_ref, wct_ref, bs_ref, dp_ref, dws_ref, dbs_ref, dgv_ref, dv_scr):
        i = pl.program_id(0)

        @pl.when(i == 0)
        def _():
            dws_ref[...] = jnp.zeros_like(dws_ref)
            dbs_ref[...] = jnp.zeros_like(dbs_ref)
            dgv_ref[...] = jnp.zeros_like(dgv_ref)

        gu, dgu = _gelu_and_grad(p_ref[:, :W].astype(F32))
        vg, dvg_dv = _gelu_and_grad(p_ref[:, W:2 * W].astype(F32))
        zp = p_ref[:, 2 * W:].astype(F32)
        sig = _sigmoid(zp)
        sz = zp * sig
        dsz = sig * (1.0 + zp * (1.0 - sig))
        r = lax.rsqrt(jnp.mean(vg * vg, axis=-1, keepdims=True) + EPS)
        vhat = vg * r
        gv = gv_ref[...]
        vb = (vhat * gv).astype(BF16)
        dy = dy_ref[...].astype(F32)
        lane = lax.broadcasted_iota(jnp.int32, (CHUNK, LANES), 1)
        dbs = jnp.zeros((CHUNK, LANES), F32)
        for g in range(G):
            sl = slice(g * gd, (g + 1) * gd)
            vsl = vb[:, sl]
            mixed = jnp.dot(wc_ref[g], vsl, preferred_element_type=F32) + bs_ref[:, g:g + 1]
            dyg, gug, szg = dy[:, sl], gu[:, sl], sz[:, sl]
            dp_ref[:, sl] = (dyg * mixed * szg * dgu[:, sl]).astype(BF16)
            dp_ref[:, 2 * W + g * gd:2 * W + (g + 1) * gd] = (dyg * gug * mixed * dsz[:, sl]).astype(BF16)
            dm = dyg * gug * szg
            dmb = dm.astype(BF16)
            dws_ref[g] += lax.dot_general(dmb, vsl, _NT, preferred_element_type=F32)
            dbs = dbs + jnp.where(lane == g, jnp.sum(dm, axis=1, keepdims=True), 0.0)
            dv_scr[:, sl] = jnp.dot(wct_ref[g], dmb, preferred_element_type=F32)
        dbs_ref[...] += dbs
        dv = dv_scr[...]
        dgv_ref[...] += jnp.sum(dv * vhat, axis=0, keepdims=True)
        dvhat = dv * gv
        dvg = r * (dvhat - vhat * jnp.mean(dvhat * vhat, axis=-1, keepdims=True))
        dp_ref[:, W:2 * W] = (dvg * dvg_dv).astype(BF16)

    return pl.pallas_call(
        body, name=name, grid=(S // CHUNK,),
        in_specs=[pl.BlockSpec((CHUNK, W3), lambda i: (i, 0)), pl.BlockSpec((CHUNK, W), lambda i: (i, 0)),
                  pl.BlockSpec((1, W), lambda i: (0, 0)),
                  pl.BlockSpec((G, CHUNK, CHUNK), lambda i: (0, 0, 0)),
                  pl.BlockSpec((G, CHUNK, CHUNK), lambda i: (0, 0, 0)),
                  pl.BlockSpec((CHUNK, G), lambda i: (0, 0))],
        out_specs=[pl.BlockSpec((CHUNK, W3), lambda i: (i, 0)),
                   pl.BlockSpec((G, CHUNK, CHUNK), lambda i: (0, 0, 0)),
                   pl.BlockSpec((CHUNK, LANES), lambda i: (0, 0)),
                   pl.BlockSpec((1, W), lambda i: (0, 0))],
        out_shape=[jax.ShapeDtypeStruct((S, W3), BF16), jax.ShapeDtypeStruct((G, CHUNK, CHUNK), F32),
                   jax.ShapeDtypeStruct((CHUNK, LANES), F32), jax.ShapeDtypeStruct((1, W), F32)],
        scratch_shapes=[pltpu.VMEM((CHUNK, W), F32)],
        compiler_params=_params("arbitrary"),
    )(p, dy, v_gain.reshape(1, W), wc, wc_t, bs_t)


AUG = 2 * HEAD
LOG2E = 1.0 / math.log(2.0)
Q_SUM_LANE = HEAD + 3
K_SUM_LANE = HEAD


def _pieces(x, sign=1.0):
    hi, mid, lo = _split3(sign * x)
    return hi.astype(F32), mid.astype(F32), lo.astype(F32)


def _lanes(lane, start, vals, rest):
    out = rest
    for n, v in enumerate(vals):
        out = jnp.where(lane == start + n, v, out)
    return out


def _qkv_prep(name, proj, cum, q_gain, k_gain, H):
    S = proj.shape[0]
    HW = H * HEAD
    tr = _tile(S, 256, unit=8)
    sigma = (HEAD ** -0.5) * LOG2E

    def body(q_ref, k_ref, v_ref, c_ref, gq_ref, gk_ref, qa_ref, ka_ref, va_ref):
        lane = lax.broadcasted_iota(jnp.int32, (tr, HEAD), 1)
        zero = jnp.zeros((tr, HEAD), F32)
        v_aug = jnp.where(lane < 3, 1.0, zero).astype(BF16)
        for h in range(H):
            sl = slice(h * HEAD, (h + 1) * HEAD)
            a0 = h * AUG
            t = q_ref[:, sl].astype(F32)
            r = lax.rsqrt(jnp.mean(t * t, axis=-1, keepdims=True) + EPS)
            qa_ref[:, a0:a0 + HEAD] = (t * r * gq_ref[...] * sigma).astype(BF16)
            t = k_ref[:, sl].astype(F32)
            r = lax.rsqrt(jnp.mean(t * t, axis=-1, keepdims=True) + EPS)
            ka_ref[:, a0:a0 + HEAD] = (t * r * gk_ref[...]).astype(BF16)
            va_ref[:, a0:a0 + HEAD] = v_ref[:, sl]
            va_ref[:, a0 + HEAD:a0 + AUG] = v_aug
            c2 = c_ref[:, h:h + 1] * LOG2E
            qa_ref[:, a0 + HEAD:a0 + AUG] = _lanes(lane, 0, _pieces(c2) + (1.0, 1.0, 1.0), zero).astype(BF16)
            ka_ref[:, a0 + HEAD:a0 + AUG] = _lanes(
                lane, 0, (1.0, 1.0, 1.0) + _pieces(c2, -1.0) + (1.0, 1.0, 1.0), zero).astype(BF16)

    col = lambda c: pl.BlockSpec((tr, HW), lambda i: (i, c))
    vec = pl.BlockSpec((1, HEAD), lambda i: (0, 0))
    aug = pl.BlockSpec((tr, H * AUG), lambda i: (i, 0))
    return pl.pallas_call(
        body, name=name, grid=(S // tr,),
        in_specs=[col(0), col(1), col(2), pl.BlockSpec((tr, LANES), lambda i: (i, 0)), vec, vec],
        out_specs=[aug] * 3, out_shape=[jax.ShapeDtypeStruct((S, H * AUG), BF16)] * 3,
        compiler_params=_params("arbitrary"),
    )(proj, proj, proj, cum, q_gain.reshape(1, HEAD), k_gain.reshape(1, HEAD))


def _split3(x):
    hi = x.astype(BF16)
    r1 = x - hi.astype(F32)
    mid = r1.astype(BF16)
    lo = (r1 - mid.astype(F32)).astype(BF16)
    return hi, mid, lo


def _tri_sum(tri, x):
    hi, mid, lo = _split3(x)
    d = lambda t: jnp.dot(tri, t, preferred_element_type=F32)
    return d(hi) + (d(mid) + d(lo))


def _log_sigmoid(x):
    return jnp.minimum(x, 0.0) - jnp.log(1.0 + jnp.exp(-jnp.abs(x)))


def _fox_cum(name, f, bias):
    S = f.shape[0]
    tb = _tile(S, 256, unit=8)

    def body(f_ref, b_ref, c_ref):
        rr = lax.broadcasted_iota(jnp.int32, (tb, tb), 0)
        cc = lax.broadcasted_iota(jnp.int32, (tb, tb), 1)
        tri = (rr >= cc).astype(BF16)

        def step(t, carry):
            off = pl.multiple_of(t * tb, tb)
            lf = _log_sigmoid(f_ref[pl.ds(off, tb), :] + b_ref[...])
            c = _tri_sum(tri, lf) + carry
            c_ref[pl.ds(off, tb), :] = c
            return c[tb - 1:tb, :]

        lax.fori_loop(0, S // tb, step, jnp.zeros((1, LANES), F32))

    return pl.pallas_call(
        body, name=name, out_shape=jax.ShapeDtypeStruct((S, LANES), F32),
        in_specs=[pl.BlockSpec(memory_space=pltpu.VMEM)] * 2, out_specs=pl.BlockSpec(memory_space=pltpu.VMEM),
        compiler_params=pltpu.CompilerParams(vmem_limit_bytes=VMEM_LIMIT),
    )(f, bias)


def _fox_cum_bwd(name, dcum, f, bias):
    S = f.shape[0]
    tb = _tile(S, 256, unit=8)
    nb = S // tb

    def body(dc_ref, f_ref, b_ref, df_ref, db_ref):
        rr = lax.broadcasted_iota(jnp.int32, (tb, tb), 0)
        cc = lax.broadcasted_iota(jnp.int32, (tb, tb), 1)
        tri = (rr <= cc).astype(BF16)

        def step(t, carry):
            tail, dbias = carry
            off = pl.multiple_of((nb - 1 - t) * tb, tb)
            dlf = _tri_sum(tri, dc_ref[pl.ds(off, tb), :]) + tail
            d = dlf * _sigmoid(-(f_ref[pl.ds(off, tb), :] + b_ref[...]))
            df_ref[pl.ds(off, tb), :] = d.astype(BF16)
            return dlf[0:1, :], dbias + jnp.sum(d, axis=0, keepdims=True)

        z = jnp.zeros((1, LANES), F32)
        _, dbias = lax.fori_loop(0, nb, step, (z, z))
        db_ref[...] = dbias

    vm = pl.BlockSpec(memory_space=pltpu.VMEM)
    return pl.pallas_call(
        body, name=name, out_shape=[jax.ShapeDtypeStruct((S, LANES), BF16), jax.ShapeDtypeStruct((1, LANES), F32)],
        in_specs=[vm] * 3, out_specs=[vm] * 2,
        compiler_params=pltpu.CompilerParams(vmem_limit_bytes=VMEM_LIMIT),
    )(dcum, f, bias)


def _attn_fwd(name, qa, ka, va, proj, H, tq, carried=None):
    S = qa.shape[0]
    HW = H * HEAD
    nq = S // tq
    hp = 2 if H % 2 == 0 else 1
    tw = 2 * tq if S % (2 * tq) == 0 else tq
    kind, g_blocks, g_layer = carried if carried else (None, [], None)
    ng = len(g_blocks)

    def body(*refs):
        q_ref, k_ref, v_ref, z_ref = refs[:4]
        o_ref, y_ref, lse_ref = refs[4 + ng:7 + ng]
        i = pl.program_id(1)
        if carried:
            mine, start, finish = _carried_exchange(kind, refs[4:4 + ng], refs[7 + ng:7 + 2 * ng],
                                                    refs[7 + 2 * ng:], g_layer)
            hd = pl.program_id(0)

            @pl.when(jnp.logical_and(mine, jnp.logical_and(hd == 0, i == 0)))
            def _():
                start()

        def step(j, carry, masked):
            off = pl.multiple_of(j * tw, tw)
            out = []
            for n in range(hp):
                m, acc = carry[n]
                a = slice(n * AUG, (n + 1) * AUG)
                s = lax.dot_general(q_ref[:, a], k_ref[pl.ds(off, tw), a], _NT, preferred_element_type=F32)
                if masked:
                    qry = i * tq + lax.broadcasted_iota(jnp.int32, (tq, tw), 0)
                    key = j * tw + lax.broadcasted_iota(jnp.int32, (tq, tw), 1)
                    s = jnp.where(qry >= key, s, -jnp.inf)
                m_new = jnp.maximum(m, jnp.max(s, axis=1, keepdims=True))
                pr = jnp.exp2(s - m_new).astype(BF16)
                acc = jnp.exp2(m - m_new) * acc + jnp.dot(pr, v_ref[pl.ds(off, tw), a], preferred_element_type=F32)
                out.append((m_new, acc))
            return tuple(out)

        init = ((jnp.full((tq, 1), -jnp.inf, F32), jnp.zeros((tq, AUG), F32)),) * hp
        below = (i * tq) // tw
        carry = lax.fori_loop(0, below, lambda j, c: step(j, c, False), init)
        carry = step(below, carry, True)
        for n in range(hp):
            m, acc = carry[n]
            sl = slice(n * HEAD, (n + 1) * HEAD)
            l = acc[:, HEAD:HEAD + 1]
            o = acc[:, :HEAD] / l
            z = z_ref[:, sl].astype(F32)
            o_ref[:, sl] = o
            y_ref[:, sl] = (o * (z * _sigmoid(z))).astype(BF16)
            lse_ref[n] = m + jnp.log(l) * LOG2E

        if carried:
            @pl.when(jnp.logical_and(hd == H // hp - 1, i == nq - 1))
            def _():
                finish()

    qspec = pl.BlockSpec((tq, hp * AUG), lambda h, i: (i, h))
    kvspec = pl.BlockSpec((S, hp * AUG), lambda h, i: (0, h))
    ospec = pl.BlockSpec((tq, hp * HEAD), lambda h, i: (i, h))
    out = pl.pallas_call(
        body, name=name, grid=(H // hp, nq),
        in_specs=[qspec, kvspec, kvspec, pl.BlockSpec((tq, hp * HEAD), lambda h, i: (i, 3 * H // hp + h))]
        + [HBM_SPEC] * ng,
        out_specs=[ospec, ospec, pl.BlockSpec((hp, tq, 1), lambda h, i: (h, i, 0))] + [HBM_SPEC] * ng,
        out_shape=[jax.ShapeDtypeStruct((S, HW), F32), jax.ShapeDtypeStruct((S, HW), BF16),
                   jax.ShapeDtypeStruct((H, S, 1), F32)] + _carried_out_shapes(kind, g_blocks),
        scratch_shapes=_carried_sems(kind, ng),
        compiler_params=_params("arbitrary", "arbitrary"),
    )(qa, ka, va, proj, *g_blocks)
    return out[0], out[1], out[2], list(out[3:])


def _attn_bwd_prep(name, dy, o, proj, qa, lse, H):
    S, HW = o.shape
    tr = _tile(S, 256, unit=8)

    def body(dy_ref, o_ref, z_ref, qa_ref, lse_ref, doa_ref, dz_ref, qab_ref):
        lane = lax.broadcasted_iota(jnp.int32, (tr, HEAD), 1)
        zero = jnp.zeros((tr, HEAD), F32)
        for h in range(H):
            sl = slice(h * HEAD, (h + 1) * HEAD)
            a0 = h * AUG
            dy = dy_ref[:, sl].astype(F32)
            z = z_ref[:, sl].astype(F32)
            o = o_ref[:, sl]
            sig = _sigmoid(z)
            dob = (dy * (z * sig)).astype(BF16)
            dz_ref[:, sl] = (dy * o * (sig * (1.0 + z * (1.0 - sig)))).astype(BF16)
            delta = jnp.sum(dob.astype(F32) * o, axis=1, keepdims=True)
            doa_ref[:, a0:a0 + HEAD] = dob
            doa_ref[:, a0 + HEAD:a0 + AUG] = _lanes(lane, 0, _pieces(delta, -1.0), zero).astype(BF16)
            qab_ref[:, a0:a0 + HEAD] = qa_ref[:, a0:a0 + HEAD]
            qab_ref[:, a0 + HEAD:a0 + AUG] = _lanes(
                lane, 6, _pieces(lse_ref[:, h:h + 1], -1.0), qa_ref[:, a0 + HEAD:a0 + AUG].astype(F32)).astype(BF16)

    row = pl.BlockSpec((tr, HW), lambda i: (i, 0))
    aug = pl.BlockSpec((tr, H * AUG), lambda i: (i, 0))
    return pl.pallas_call(
        body, name=name, grid=(S // tr,),
        in_specs=[row, row, pl.BlockSpec((tr, HW), lambda i: (i, 3)), aug, pl.BlockSpec((tr, LANES), lambda i: (i, 0))],
        out_specs=[aug, row, aug],
        out_shape=[jax.ShapeDtypeStruct((S, H * AUG), BF16), jax.ShapeDtypeStruct((S, HW), BF16),
                   jax.ShapeDtypeStruct((S, H * AUG), BF16)],
        compiler_params=_params("arbitrary"),
    )(dy, o, proj, qa, lse)


def _attn_bwd(name, qab, doa, ka, va, H, tq, carried=None):
    S = qab.shape[0]
    nq = S // tq
    tw = 2 * tq if S % (2 * tq) == 0 else tq
    sums, owner = carried if carried else ([], None)
    nt = len(sums)

    def body(*refs):
        q_ref, do_ref, k_ref, v_ref = refs[:4]
        dq_ref, dk_ref, dv_ref = refs[4 + nt:7 + nt]
        j = pl.program_id(1)
        if carried:
            mine, start, wait = _scatter_copies(refs[4:4 + nt], refs[7 + nt:7 + 2 * nt], *refs[7 + 2 * nt:], owner)
            hd = pl.program_id(0)

            @pl.when(jnp.logical_and(mine, jnp.logical_and(hd == 0, j == 0)))
            def _():
                start()

        @pl.when(j == 0)
        def _():
            dq_ref[...] = jnp.zeros_like(dq_ref)

        k = k_ref[...]
        v = v_ref[...]

        def step(i, carry, masked):
            dk_acc, dv_acc = carry
            off = pl.multiple_of(i * tw, tw)
            q = q_ref[pl.ds(off, tw), :]
            do = do_ref[pl.ds(off, tw), :]
            st = lax.dot_general(k, q, _NT, preferred_element_type=F32)
            if masked:
                key = j * tq + lax.broadcasted_iota(jnp.int32, (tq, tw), 0)
                qry = i * tw + lax.broadcasted_iota(jnp.int32, (tq, tw), 1)
                st = jnp.where(qry >= key, st, -jnp.inf)
            pt = jnp.exp2(st)
            dst = pt * lax.dot_general(v, do, _NT, preferred_element_type=F32)
            dsb = dst.astype(BF16)
            dv_acc = dv_acc + jnp.dot(pt.astype(BF16), do[:, :HEAD], preferred_element_type=F32)
            dk_acc = dk_acc + jnp.dot(dsb, q, preferred_element_type=F32)
            dq_ref[pl.ds(off, tw), :] += lax.dot_general(dsb, k, _TN, preferred_element_type=F32)
            return dk_acc, dv_acc

        first = (j * tq) // tw
        carry = step(first, (jnp.zeros((tq, AUG), F32), jnp.zeros((tq, HEAD), F32)), True)
        dk_acc, dv_acc = lax.fori_loop(first + 1, S // tw, lambda i, c: step(i, c, False), carry)
        dk_ref[...] = dk_acc
        dv_ref[...] = dv_acc
        if carried:
            @pl.when(jnp.logical_and(mine, jnp.logical_and(hd == H - 1, j == nq - 1)))
            def _():
                wait()

    full = pl.BlockSpec((S, AUG), lambda h, j: (0, h))
    blk = pl.BlockSpec((tq, AUG), lambda h, j: (j, h))
    out = pl.pallas_call(
        body, name=name, grid=(H, nq),
        in_specs=[full, full, blk, blk] + [HBM_SPEC] * nt,
        out_specs=[full, blk, pl.BlockSpec((tq, HEAD), lambda h, j: (j, h))] + [HBM_SPEC] * nt,
        out_shape=[jax.ShapeDtypeStruct((S, H * AUG), F32), jax.ShapeDtypeStruct((S, H * AUG), F32),
                   jax.ShapeDtypeStruct((S, H * HEAD), F32)] + [jax.ShapeDtypeStruct(s.shape, F32) for s in sums],
        scratch_shapes=[pltpu.SemaphoreType.DMA((3 * nt,))] * 2 if carried else [],
        compiler_params=_params("arbitrary", "arbitrary"),
    )(qab, doa, ka, va, *sums)
    return out[0], out[1], out[2], list(out[3:])


def _qk_bwd(name, proj, dqa, dka, dv, dz, q_gain, k_gain, H):
    S = proj.shape[0]
    HW = H * HEAD
    tr = _tile(S, 256, unit=8)
    scale = HEAD ** -0.5
    factors = (scale, 1.0 / LOG2E)

    def body(q_ref, k_ref, dq_ref, dk_ref, dv_ref, dz_ref, gq_ref, gk_ref, dp_ref, dgq_ref, dgk_ref, dc_ref):
        i = pl.program_id(0)

        @pl.when(i == 0)
        def _():
            dgq_ref[...] = jnp.zeros_like(dgq_ref)
            dgk_ref[...] = jnp.zeros_like(dgk_ref)

        for n, (src, dsrc, gain, dgain) in enumerate(((q_ref, dq_ref, gq_ref, dgq_ref), (k_ref, dk_ref, gk_ref, dgk_ref))):
            acc = jnp.zeros((1, HEAD), F32)
            for h in range(H):
                sl = slice(h * HEAD, (h + 1) * HEAD)
                t = src[:, sl].astype(F32)
                r = lax.rsqrt(jnp.mean(t * t, axis=-1, keepdims=True) + EPS)
                that = t * r
                dn = dsrc[:, h * AUG:h * AUG + HEAD] * factors[n]
                acc = acc + jnp.sum(dn * that, axis=0, keepdims=True)
                dhat = dn * gain[...]
                dt = r * (dhat - that * jnp.mean(dhat * that, axis=-1, keepdims=True))
                dp_ref[:, n * HW + h * HEAD:n * HW + (h + 1) * HEAD] = dt.astype(BF16)
            dgain[...] += acc
        dp_ref[:, 2 * HW:3 * HW] = dv_ref[...].astype(BF16)
        dp_ref[:, 3 * HW:] = dz_ref[...]
        lane = lax.broadcasted_iota(jnp.int32, (tr, LANES), 1)
        dc = jnp.zeros((tr, LANES), F32)
        for h in range(H):
            qs = dq_ref[:, h * AUG + K_SUM_LANE:h * AUG + K_SUM_LANE + 1]
            ks = dk_ref[:, h * AUG + Q_SUM_LANE:h * AUG + Q_SUM_LANE + 1]
            dc = jnp.where(lane == h, qs - ks, dc)
        dc_ref[...] = dc

    col = lambda c: pl.BlockSpec((tr, HW), lambda i: (i, c))
    row = col(0)
    aug = pl.BlockSpec((tr, H * AUG), lambda i: (i, 0))
    vec = pl.BlockSpec((1, HEAD), lambda i: (0, 0))
    return pl.pallas_call(
        body, name=name, grid=(S // tr,),
        in_specs=[col(0), col(1), aug, aug, row, row, vec, vec],
        out_specs=[pl.BlockSpec((tr, 4 * HW), lambda i: (i, 0)), vec, vec, pl.BlockSpec((tr, LANES), lambda i: (i, 0))],
        out_shape=[jax.ShapeDtypeStruct((S, 4 * HW), BF16), jax.ShapeDtypeStruct((1, HEAD), F32),
                   jax.ShapeDtypeStruct((1, HEAD), F32), jax.ShapeDtypeStruct((S, LANES), F32)],
        compiler_params=_params("arbitrary"),
    )(proj, proj, dqa, dka, dv, dz, q_gain.reshape(1, HEAD), k_gain.reshape(1, HEAD))


def _row_tile(R, C, budget_bytes=1 << 20):
    cap = max(8, budget_bytes // (4 * C))
    t = (min(cap, R) // 8) * 8
    while t >= 8:
        if R % t == 0:
            return t
        t -= 8
    return R


def _add_if(name, a, b, active):
    R, C = a.shape
    tr = _row_tile(R, C)

    def body(act_ref, a_ref, b_ref, o_ref):
        @pl.when(act_ref[0] != 0)
        def _():
            o_ref[...] = a_ref[...] + b_ref[...]

    blk = pl.BlockSpec((tr, C), lambda i, s: (i * s[0], 0))
    grid_spec = pltpu.PrefetchScalarGridSpec(
        num_scalar_prefetch=1, grid=(R // tr,), in_specs=[blk, blk], out_specs=blk)
    return pl.pallas_call(
        body, name=name, grid_spec=grid_spec, out_shape=jax.ShapeDtypeStruct((R, C), F32),
        compiler_params=_params("arbitrary"),
    )(active, a, b)


def _sum_slots(name, slots, active):
    n, R, C = slots.shape
    tr = _row_tile(R, C, budget_bytes=(1 << 20) // 2)

    def body(act_ref, s_ref, o_ref):
        @pl.when(act_ref[0] != 0)
        def _():
            acc = s_ref[0]
            for k in range(1, n):
                acc = acc + s_ref[k]
            o_ref[...] = acc

    grid_spec = pltpu.PrefetchScalarGridSpec(
        num_scalar_prefetch=1, grid=(R // tr,),
        in_specs=[pl.BlockSpec((n, tr, C), lambda i, s: (0, i * s[0], 0))],
        out_specs=pl.BlockSpec((tr, C), lambda i, s: (i * s[0], 0)))
    return pl.pallas_call(
        body, name=name, grid_spec=grid_spec, out_shape=jax.ShapeDtypeStruct((R, C), F32),
        compiler_params=_params("arbitrary"),
    )(active, slots)


def _adamw(name, w, g, m, v):
    R, C = w.shape
    tr = _row_tile(R, C, budget_bytes=(1 << 20) // 2)
    c1 = 1.0 - ADAM_B1 ** ADAM_STEP
    c2 = 1.0 - ADAM_B2 ** ADAM_STEP

    def body(w_ref, g_ref, m_ref, v_ref, d_ref, nm_ref, nv_ref):
        gv = g_ref[...]
        nm = ADAM_B1 * m_ref[...] + (1.0 - ADAM_B1) * gv
        nv = ADAM_B2 * v_ref[...] + (1.0 - ADAM_B2) * (gv * gv)
        m_hat = nm / c1
        v_hat = nv / c2
        d_ref[...] = -ADAM_LR * (m_hat / (jnp.sqrt(v_hat) + ADAM_EPS) + ADAM_WD * w_ref[...])
        nm_ref[...] = nm
        nv_ref[...] = nv

    blk = pl.BlockSpec((tr, C), lambda i: (i, 0))
    return pl.pallas_call(
        body, name=name, grid=(R // tr,), in_specs=[blk] * 4, out_specs=[blk] * 3,
        out_shape=[jax.ShapeDtypeStruct((R, C), F32)] * 3,
        compiler_params=_params("arbitrary"),
    )(w, g, m, v)


def _place():
    x, y, c = lax.axis_index("x"), lax.axis_index("y"), lax.axis_index("c")
    chips = [(1 - x, y), (x, 1 - y), (1 - x, 1 - y)]
    return x, y, c, chips


def _gather_copies(ins, outs, sems, layer):
    s_send, s_recv, f_send, f_recv = sems
    nt = len(ins)
    x, y, c, chips = _place()
    me = 2 * x + y
    ids = [2 * cx + cy for cx, cy in chips]
    pairs = [(t, k) for t in range(nt) for k in range(3)]

    def over_ici(t, k, block):
        return pltpu.make_async_remote_copy(
            src_ref=ins[t], dst_ref=outs[t].at[block], send_sem=s_send.at[3 * t + k],
            recv_sem=s_recv.at[3 * t + k], device_id=(*chips[k], layer), device_id_type=MESH)

    def over_d2d(t, k):
        blk = outs[t].at[ids[k]]
        return pltpu.make_async_remote_copy(
            src_ref=blk, dst_ref=blk, send_sem=f_send.at[3 * t + k], recv_sem=f_recv.at[3 * t + k],
            device_id=(x, y, 1 - c), device_id_type=MESH)

    def start():
        for t, k in pairs:
            over_ici(t, k, me).start()

    def finish():
        @pl.when(c == layer)
        def _():
            for t, k in pairs:
                over_ici(t, k, ids[k]).wait_recv()
                over_d2d(t, k).start()
            for t, k in pairs:
                over_ici(t, k, me).wait_send()
                over_d2d(t, k).wait_send()

        @pl.when(c != layer)
        def _():
            for t, k in pairs:
                over_d2d(t, k).wait_recv()

    return c == layer, start, finish


def _gather_out_shapes(blocks):
    return [jax.ShapeDtypeStruct((N_CHIPS,) + b.shape, b.dtype) for b in blocks]


def _gather_sems(n):
    return [pltpu.SemaphoreType.DMA((3 * n,))] * 4


def _gather_weights(name, blocks, layer):
    nt = len(blocks)

    def body(*refs):
        mine, start, finish = _gather_copies(refs[:nt], refs[nt:2 * nt], refs[2 * nt:], layer)

        @pl.when(mine)
        def _():
            start()

        finish()

    return pl.pallas_call(
        body, name=name, out_shape=_gather_out_shapes(blocks),
        in_specs=[HBM_SPEC] * nt, out_specs=[HBM_SPEC] * nt, scratch_shapes=_gather_sems(nt),
    )(*blocks)


def _send_copies(ins, outs, s_send, s_recv, owner):
    nt = len(ins)
    x, y, c, _ = _place()

    def copy(t):
        return pltpu.make_async_remote_copy(
            src_ref=ins[t], dst_ref=outs[t], send_sem=s_send.at[t], recv_sem=s_recv.at[t],
            device_id=(x, y, owner), device_id_type=MESH)

    def start():
        for t in range(nt):
            copy(t).start()

    def finish():
        @pl.when(c != owner)
        def _():
            for t in range(nt):
                copy(t).wait_send()

        @pl.when(c == owner)
        def _():
            for t in range(nt):
                copy(t).wait_recv()

    return c != owner, start, finish


def _send_to_owner(name, grads, owner):
    nt = len(grads)

    def body(*refs):
        mine, start, finish = _send_copies(refs[:nt], refs[nt:2 * nt], *refs[2 * nt:], owner)

        @pl.when(mine)
        def _():
            start()

        finish()

    return pl.pallas_call(
        body, name=name,
        out_shape=[jax.ShapeDtypeStruct(g.shape, F32) for g in grads],
        in_specs=[HBM_SPEC] * nt, out_specs=[HBM_SPEC] * nt,
        scratch_shapes=[pltpu.SemaphoreType.DMA((nt,))] * 2,
    )(*grads)


def _scatter_copies(ins, outs, s_send, s_recv, owner):
    nt = len(ins)
    x, y, c, chips = _place()
    me = 2 * x + y
    ids = [2 * cx + cy for cx, cy in chips]
    pairs = [(t, k) for t in range(nt) for k in range(3)]

    def copy(t, k, slot):
        return pltpu.make_async_remote_copy(
            src_ref=ins[t].at[ids[k]], dst_ref=outs[t].at[slot], send_sem=s_send.at[3 * t + k],
            recv_sem=s_recv.at[3 * t + k], device_id=(*chips[k], owner), device_id_type=MESH)

    def start():
        for t, k in pairs:
            copy(t, k, me).start()

    def wait():
        for t, k in pairs:
            copy(t, k, ids[k]).wait()

    return c == owner, start, wait


def _carried_exchange(kind, ins, outs, sems, layer):
    if kind == "gather":
        return _gather_copies(ins, outs, sems, layer)
    if kind == "send":
        return _send_copies(ins, outs, *sems, layer)
    mine, start, wait = _scatter_copies(ins, outs, *sems, layer)
    return mine, start, lambda: pl.when(mine)(wait)


def _carried_out_shapes(kind, arrays):
    if kind == "gather":
        return _gather_out_shapes(arrays)
    return [jax.ShapeDtypeStruct(a.shape, F32) for a in arrays]


def _carried_sems(kind, n):
    if kind is None:
        return []
    if kind == "gather":
        return _gather_sems(n)
    return [pltpu.SemaphoreType.DMA((n if kind == "send" else 3 * n,))] * 2


def _last_exchanges(reduced, small):
    nt = len(reduced)

    def body(*refs):
        ins, small_ref = refs[:nt], refs[nt]
        outs, slots_ref = refs[nt + 1:2 * nt + 1], refs[2 * nt + 1]
        s_send, s_recv, a_send, a_recv = refs[2 * nt + 2:]
        x, y, c, _ = _place()
        flips = [(fx, fy, fc) for fx in (0, 1) for fy in (0, 1) for fc in (0, 1)][1:]

        def peer(f):
            return tuple(1 - a if flip else a for a, flip in zip((x, y, c), f))

        def slot(p):
            return 4 * p[0] + 2 * p[1] + p[2]

        def swap(t):
            return pltpu.make_async_remote_copy(
                src_ref=ins[t], dst_ref=outs[t], send_sem=s_send.at[t], recv_sem=s_recv.at[t],
                device_id=(x, y, 1 - c), device_id_type=MESH)

        def to_all(k, owner):
            return pltpu.make_async_remote_copy(
                src_ref=small_ref, dst_ref=slots_ref.at[slot(owner)], send_sem=a_send.at[k], recv_sem=a_recv.at[k],
                device_id=peer(flips[k]), device_id_type=MESH)

        for t in range(nt):
            swap(t).start()
        for k in range(7):
            to_all(k, (x, y, c)).start()
        for t in range(nt):
            swap(t).wait()
        for k in range(7):
            to_all(k, peer(flips[k])).wait()

    out = pl.pallas_call(
        body, name="last_exchanges",
        out_shape=[jax.ShapeDtypeStruct(r.shape, F32) for r in reduced]
        + [jax.ShapeDtypeStruct((8,) + small.shape, F32)],
        in_specs=[HBM_SPEC] * (nt + 1), out_specs=[HBM_SPEC] * (nt + 1),
        scratch_shapes=[pltpu.SemaphoreType.DMA((nt,))] * 2 + [pltpu.SemaphoreType.DMA((7,))] * 2,
    )(*reduced, small)
    return list(out[:nt]), out[nt]


def _rows128(a):
    flat = a.reshape(-1)
    rows = -(-flat.shape[0] // LANES)
    rows8 = -(-rows // 8) * 8
    flat = jnp.pad(flat, (0, rows8 * LANES - flat.shape[0]))
    return flat.reshape(rows8, LANES)


def _pack(parts):
    return jnp.concatenate([_rows128(p) for p in parts], axis=0)


def _unpack(buf, shapes):
    out, r = [], 0
    for shp in shapes:
        n = int(np.prod(shp))
        rows8 = -(-(-(-n // LANES)) // 8) * 8
        out.append(buf[r:r + rows8].reshape(-1)[:n].reshape(shp))
        r += rows8
    return out


def kernel(x, a_norm_g, a_w_in, a_v_norm_g, a_w_s, a_b_s, a_w_out, b_norm_g, b_w_in, b_f_bias, b_q_norm_g, b_k_norm_g, b_w_out, loss_target, m_a_norm_g, m_a_w_in, m_a_v_norm_g, m_a_w_s, m_a_b_s, m_a_w_out, m_b_norm_g, m_b_w_in, m_b_f_bias, m_b_q_norm_g, m_b_k_norm_g, m_b_w_out, v_a_norm_g, v_a_w_in, v_a_v_norm_g, v_a_w_s, v_a_b_s, v_a_w_out, v_b_norm_g, v_b_w_in, v_b_f_bias, v_b_q_norm_g, v_b_k_norm_g, v_b_w_out):
    xs = x[0]
    target = loss_target[0]
    S, D = xs.shape
    n_layers = a_w_in.shape[0]
    assert n_layers == 2
    W = a_v_norm_g.shape[1]
    G = a_w_s.shape[1]
    H = b_f_bias.shape[1]
    HW = H * HEAD
    tq_fwd = _tile(S, 512)
    tq_bwd = _tile(S, 512)
    core = lax.axis_index("c")
    chip = 2 * lax.axis_index("x") + lax.axis_index("y")

    own = dict(a_w_in=a_w_in.astype(BF16), a_w_out=a_w_out.astype(BF16), b_w_in=b_w_in.astype(BF16),
               b_w_out=b_w_out.astype(BF16), b_norm_g=b_norm_g.reshape(n_layers, 1, -1))
    cb = b_w_in.shape[2]
    w_ain, w_aout, w_bmain, w_bf, w_bout, b_norm_full = ([None] * n_layers for _ in range(6))

    def blocks_of(tensors, layer):
        return [own[n][layer] for n in tensors]

    def take(tensors, layer, arrived):
        for n, got, mine in zip(tensors, arrived, blocks_of(tensors, layer)):
            full = lax.dynamic_update_slice(got, mine[None], (chip, 0, 0))
            if n == "a_w_in":
                w_ain[layer] = full
            elif n == "a_w_out":
                w_aout[layer] = full
            elif n == "b_w_out":
                w_bout[layer] = full
            elif n == "b_norm_g":
                b_norm_full[layer] = full.reshape(D)
            else:
                cols = jnp.transpose(full, (1, 0, 2)).reshape(D, N_CHIPS * cb)
                w_bmain[layer] = cols[:, :4 * HW]
                w_bf[layer] = jnp.pad(cols[:, 4 * HW:], ((0, 0), (0, LANES - H)))

    first_a = ("a_w_in", "a_w_out")
    first_b = ("b_w_in", "b_w_out", "b_norm_g")
    take(first_a, 0, _gather_weights("gather_a0", blocks_of(first_a, 0), 0))
    causal = jnp.tril(jnp.ones((CHUNK, CHUNK), dtype=bool))
    wc = jnp.where(causal[None, None], a_w_s, 0).astype(BF16)
    wc_t = jnp.swapaxes(wc, 2, 3)
    bs_t = jnp.swapaxes(a_b_s, 1, 2)
    f_bias = jnp.pad(b_f_bias, ((0, 0), (0, LANES - H))).reshape(n_layers, 1, LANES)

    def view_ain(l):
        return _View(w_ain[l], "col")

    def view_aout(l):
        return _View(w_aout[l], "row")

    def view_bout(l):
        return _View(w_bout[l], "row")

    saved = []
    cur = xs
    for i in range(2 * n_layers):
        l = i // 2
        if i % 2 == 0:
            h, h_t = _rmsnorm_fwd(f"a{l}_norm", cur, a_norm_g[l])
            if i == 0:
                p, arrived = _matmul(f"a{l}_in", _View(h), view_ain(l), out_dtype=BF16, tm=1024, tn=1024, tk=2048,
                                     carried=("gather", blocks_of(first_b, 0), 0))
                take(first_b, 0, arrived)
            else:
                p = _matmul(f"a{l}_in", _View(h), view_ain(l), out_dtype=BF16, tm=1024, tn=1024, tk=2048)
            y = _gate_fwd(f"a{l}_gate", p, a_v_norm_g[l], wc[l], bs_t[l])
            nxt = _matmul(f"a{l}_out", _View(y), view_aout(l), tm=1024, tn=1024, tk=1024, residual=cur)
            saved.append((cur, h_t, p, y))
        else:
            h, h_t = _rmsnorm_fwd(f"b{l}_norm", cur, b_norm_full[l])
            proj = _matmul(f"b{l}_in", _View(h), _View(w_bmain[l]), out_dtype=BF16, tm=1024, tn=1024, tk=2048)
            f = _matmul(f"b{l}_inf", _View(h), _View(w_bf[l]), tm=1024, tn=LANES, tk=2048)
            cum = _fox_cum(f"b{l}_cum", f, f_bias[l])
            qa, ka, va = _qkv_prep(f"b{l}_qkv", proj, cum, b_q_norm_g[l], b_k_norm_g[l], H)
            if i == 1:
                o, y, lse, arrived = _attn_fwd(f"b{l}_attn", qa, ka, va, proj, H, tq_fwd,
                                               carried=("gather", blocks_of(first_a + first_b, 1), 1))
                take(first_a + first_b, 1, arrived)
            else:
                o, y, lse, _ = _attn_fwd(f"b{l}_attn", qa, ka, va, proj, H, tq_fwd)
            nxt = _matmul(f"b{l}_out", _View(y), view_bout(l), tm=1024, tn=1024, tk=1024, residual=cur)
            saved.append((cur, h_t, proj, f, qa, ka, va, o, y, lse))
        cur = nxt

    g, gb, lcols = _loss_grad(cur, target)
    loss = lax.psum(0.5 * jnp.sum(lcols) / D, ("x", "y", "c"))

    big = {"a_w_in": [None] * n_layers, "a_w_out": [None] * n_layers,
           "b_w_in": [None] * n_layers, "b_w_out": [None] * n_layers}
    small = {k: [None] * n_layers for k in
             ("a_norm_g", "a_v_norm_g", "a_w_s", "a_b_s", "b_norm_g", "b_f_bias", "b_q_norm_g", "b_k_norm_g")}
    names = ["a_w_in", "a_w_out", "b_w_in", "b_w_out"]
    reduced = [{} for _ in range(n_layers)]

    def chip_sums_of(tag, tensors, layer, arrived=()):
        arrived = dict(arrived)
        rest = [n for n in tensors if n not in arrived]
        arrived.update(zip(rest, _send_to_owner(f"to_owner{tag}", [big[n][layer] for n in rest], layer)))
        mine = [big[n][layer] for n in tensors]
        got = [arrived[n] for n in tensors]
        active = (core == layer).astype(jnp.int32).reshape(1)
        sums = []
        for n, a, b in zip(tensors, mine, got):
            shp = a.shape
            flat = lambda t: t.reshape(shp[0] * shp[1], shp[2])
            sums.append(_add_if(f"chipsum{tag}_{n}", flat(a), flat(b), active).reshape(shp))
        return sums, active

    def reduce_slots(tag, tensors, layer, sums, got, active):
        for n, g_, s_ in zip(tensors, got, sums):
            slots = lax.dynamic_update_slice(g_, lax.dynamic_index_in_dim(s_, chip, keepdims=True), (chip, 0, 0))
            reduced[layer][n] = _sum_slots(f"reduce{tag}_{n}", slots, active)

    for i in reversed(range(2 * n_layers)):
        l = i // 2
        if i % 2 == 0:
            x_in, h_t, p, y = saved[i]
            dims_dy = dict(tb=True, out_dtype=BF16, tm=1024, tn=1024, tk=2048)
            if i == 0:
                sent = ["b_w_in", "b_w_out"]
                dy, got = _matmul(f"a{l}_dy", _View(gb), view_aout(l), carried=("send", [big[n][0] for n in sent], 0),
                                  **dims_dy)
                sent0 = list(zip(sent, got))
            else:
                dy = _matmul(f"a{l}_dy", _View(gb), view_aout(l), **dims_dy)
            d_wout = _matmul(f"a{l}_dwout", _View(y), _View(gb), ta=True, tm=2048, tn=1024, tk=1024)
            big["a_w_out"][l] = d_wout.reshape(N_CHIPS, W // N_CHIPS, D)
            dp, d_ws, d_bs, d_gv = _gate_bwd(f"a{l}_dgate", p, dy, a_v_norm_g[l], wc[l], wc_t[l], bs_t[l])
            dims_dh = dict(tb=True, tm=1024, tn=1024, tk=3072)
            dims_dwin = dict(tm=2048, tn=1024, tk=1024, out_colblocks=N_CHIPS)
            if i == 0:
                early, late = ["b_w_in", "b_w_out", "a_w_out"], ["a_w_in"]
                sums_e, active0 = chip_sums_of("0e", early, 0, sent0)
                d_win, got_e = _matmul(f"a{l}_dwin", _View(h_t), _View(dp), carried=("scatter", sums_e[:2], 0),
                                       **dims_dwin)
                big["a_w_in"][l] = d_win
                sums_l, _ = chip_sums_of("0l", late, 0)
                dh, got_l = _matmul(f"a{l}_dh", _View(dp), view_ain(l), carried=("scatter", sums_e[2:] + sums_l, 0),
                                    **dims_dh)
                reduce_slots("0e", early, 0, sums_e, got_e + got_l[:1], active0)
                reduce_slots("0l", late, 0, sums_l, got_l[1:], active0)
            else:
                sent = ["b_w_in", "b_w_out", "a_w_out"]
                dh, got = _matmul(f"a{l}_dh", _View(dp), view_ain(l), carried=("send", [big[n][1] for n in sent], 1),
                                  **dims_dh)
                sent1 = list(zip(sent, got))
                d_win = _matmul(f"a{l}_dwin", _View(h_t), _View(dp), **dims_dwin)
                big["a_w_in"][l] = d_win
            g, gb, d_gn = _rmsnorm_bwd(f"a{l}_dnorm", x_in, a_norm_g[l], dh, g)
            small["a_norm_g"][l] = d_gn.reshape(D)
            small["a_v_norm_g"][l] = d_gv.reshape(W)
            small["a_w_s"][l] = jnp.where(causal[None], d_ws, 0.0)
            small["a_b_s"][l] = d_bs[:, :G].T
        else:
            x_in, h_t, proj, f, qa, ka, va, o, y, lse = saved[i]
            dy = _matmul(f"b{l}_dy", _View(gb), view_bout(l), tb=True, out_dtype=BF16, tm=1024, tn=1024, tk=2048)
            d_wout = _matmul(f"b{l}_dwout", _View(y), _View(gb), ta=True, tm=2048, tn=1024, tk=1024)
            lse_lanes = jnp.pad(lse.reshape(H, S).T, ((0, 0), (0, LANES - H)))
            doa, dz, qab = _attn_bwd_prep(f"b{l}_dprep", dy, o, proj, qa, lse_lanes, H)
            early = (i == 1)
            if early:
                sums1, active1 = chip_sums_of("1", names, 1, sent1)
            dqa, dka, dv, got1 = _attn_bwd(f"b{l}_dattn", qab, doa, ka, va, H, tq_bwd,
                                           carried=(sums1, 1) if early else None)
            if early:
                reduce_slots("1", names, 1, sums1, got1, active1)
            dproj, d_gq, d_gk, dcum = _qk_bwd(f"b{l}_dqk", proj, dqa, dka, dv, dz, b_q_norm_g[l], b_k_norm_g[l], H)
            df, d_fb = _fox_cum_bwd(f"b{l}_dcum", dcum, f, f_bias[l])
            dh_f = _matmul(f"b{l}_dhf", _View(df), _View(w_bf[l]), tb=True, tm=1024, tn=1024, tk=LANES)
            dh = _matmul(f"b{l}_dh", _View(dproj), _View(w_bmain[l]), tb=True, tm=1024, tn=1024, tk=2048,
                         residual=dh_f)
            d_wmain = _matmul(f"b{l}_dwin", _View(h_t), _View(dproj), tm=2048, tn=1024, tk=1024)
            d_wf = _matmul(f"b{l}_dwinf", _View(h_t), _View(df), tm=2048, tn=LANES, tk=1024)
            d_win = jnp.concatenate([d_wmain, d_wf[:, :H]], axis=1)
            g, gb, d_gn = _rmsnorm_bwd(f"b{l}_dnorm", x_in, b_norm_full[l], dh, g)
            big["b_w_in"][l] = jnp.transpose(d_win.reshape(D, N_CHIPS, cb), (1, 0, 2))
            big["b_w_out"][l] = d_wout.reshape(N_CHIPS, HW // N_CHIPS, D)
            small["b_norm_g"][l] = d_gn.reshape(D)
            small["b_f_bias"][l] = d_fb[0, :H]
            small["b_q_norm_g"][l] = d_gq.reshape(HEAD)
            small["b_k_norm_g"][l] = d_gk.reshape(HEAD)
    grad_x = g[None]

    small_names = ["a_norm_g", "a_v_norm_g", "a_w_s", "a_b_s", "b_norm_g", "b_f_bias", "b_q_norm_g", "b_k_norm_g"]
    small_parts = [jnp.stack(small[n]) for n in small_names]
    small_mine = _pack(small_parts)
    mine = [jnp.where(core == 0, reduced[0][n], reduced[1][n]) for n in names]
    others, small_all = _last_exchanges(mine, small_mine)
    grads = {n: jnp.where(core == 0, jnp.stack([m_, o_]), jnp.stack([o_, m_]))
             for n, m_, o_ in zip(names, mine, others)}
    small_all = lax.dynamic_update_slice(small_all, small_mine[None], (2 * chip + core, 0, 0))
    small_sum = _sum_slots("reduce_small", small_all, jnp.ones((1,), jnp.int32))
    for n, a in zip(small_names, _unpack(small_sum, [p.shape for p in small_parts])):
        grads[n] = a
    nb = b_norm_g.shape[1]
    grads["b_norm_g"] = lax.dynamic_slice_in_dim(grads["b_norm_g"], chip * nb, nb, axis=1)

    weights = dict(a_norm_g=a_norm_g, a_w_in=a_w_in, a_v_norm_g=a_v_norm_g, a_w_s=a_w_s, a_b_s=a_b_s,
                   a_w_out=a_w_out, b_norm_g=b_norm_g, b_w_in=b_w_in, b_f_bias=b_f_bias,
                   b_q_norm_g=b_q_norm_g, b_k_norm_g=b_k_norm_g, b_w_out=b_w_out)
    mom1 = dict(a_norm_g=m_a_norm_g, a_w_in=m_a_w_in, a_v_norm_g=m_a_v_norm_g, a_w_s=m_a_w_s, a_b_s=m_a_b_s,
                a_w_out=m_a_w_out, b_norm_g=m_b_norm_g, b_w_in=m_b_w_in, b_f_bias=m_b_f_bias,
                b_q_norm_g=m_b_q_norm_g, b_k_norm_g=m_b_k_norm_g, b_w_out=m_b_w_out)
    mom2 = dict(a_norm_g=v_a_norm_g, a_w_in=v_a_w_in, a_v_norm_g=v_a_v_norm_g, a_w_s=v_a_w_s, a_b_s=v_a_b_s,
                a_w_out=v_a_w_out, b_norm_g=v_b_norm_g, b_w_in=v_b_w_in, b_f_bias=v_b_f_bias,
                b_q_norm_g=v_b_q_norm_g, b_k_norm_g=v_b_k_norm_g, b_w_out=v_b_w_out)
    order = ["a_norm_g", "a_w_in", "a_v_norm_g", "a_w_s", "a_b_s", "a_w_out", "b_norm_g", "b_w_in", "b_f_bias",
             "b_q_norm_g", "b_k_norm_g", "b_w_out"]
    delta, new_m, new_v = {}, {}, {}
    for n in names:
        shp = weights[n].shape
        flat = lambda a: a.reshape(shp[0] * shp[1], shp[2])
        d, nm, nv = _adamw(f"adamw_{n}", flat(weights[n]), flat(grads[n]), flat(mom1[n]), flat(mom2[n]))
        delta[n], new_m[n], new_v[n] = d.reshape(shp), nm.reshape(shp), nv.reshape(shp)
    small_shapes = [weights[n].shape for n in small_names]
    pack_w, pack_g, pack_m, pack_v = (_pack([d[n] for n in small_names]) for d in (weights, grads, mom1, mom2))
    d, nm, nv = _adamw("adamw_small", pack_w, pack_g, pack_m, pack_v)
    for dst, buf in ((delta, d), (new_m, nm), (new_v, nv)):
        for n, a in zip(small_names, _unpack(buf, small_shapes)):
            dst[n] = a

    return (loss, grad_x, *[grads[n] for n in order], *[delta[n] for n in order],
            *[new_m[n] for n in order], *[new_v[n] for n in order])
```

```python
import functools
import math

import numpy as np
import jax
import jax.numpy as jnp
from jax import lax
from jax.experimental import pallas as pl
from jax.experimental.pallas import tpu as pltpu

F32 = jnp.float32
BF16 = jnp.bfloat16
MESH = pl.DeviceIdType.MESH

EPS = 1e-6
CHUNK = 128
HEAD = 128
LANES = 128
N_CHIPS = 4
VMEM_LIMIT = 56 * 1024 * 1024

ADAM_LR = 0.001
ADAM_B1 = 0.9
ADAM_B2 = 0.999
ADAM_EPS = 1e-08
ADAM_WD = 0.01
ADAM_STEP = 10

_NT = (((1,), (1,)), ((), ()))
_TN = (((0,), (0,)), ((), ()))
_GELU_C = math.sqrt(2.0 / math.pi)

HBM_SPEC = pl.BlockSpec(memory_space=pltpu.HBM)


def _params(*sem):
    return pltpu.CompilerParams(dimension_semantics=sem, vmem_limit_bytes=VMEM_LIMIT)


def _tile(dim, pref, unit=LANES):
    t = (min(pref, dim) // unit) * unit
    while t >= unit:
        if dim % t == 0:
            return t
        t -= unit
    return dim


def _gelu(x):
    return 0.5 * x * (1.0 + jnp.tanh(_GELU_C * (x + 0.044715 * (x * x * x))))


def _gelu_and_grad(x):
    x2 = x * x
    t = jnp.tanh(_GELU_C * (x + 0.044715 * (x2 * x)))
    val = 0.5 * x * (1.0 + t)
    grad = 0.5 * (1.0 + t) + 0.5 * x * (1.0 - t * t) * (_GELU_C * (1.0 + 3.0 * 0.044715 * x2))
    return val, grad


def _sigmoid(x):
    return 1.0 / (1.0 + jnp.exp(-x))


class _View:
    def __init__(self, arr, kind="2d", lead=()):
        self.arr, self.kind, self.lead = arr, kind, tuple(lead)
        shp = arr.shape[len(self.lead):]
        if kind == "2d":
            self.R, self.C = shp
        elif kind == "col":
            self.nb, self.R, self.cb = shp
            self.C = self.nb * self.cb
        else:
            self.nb, self.rb, self.C = shp
            self.R = self.nb * self.rb

    def fit(self, tr, tc):
        if self.kind == "col":
            tc = _tile(self.cb, tc)
        elif self.kind == "row":
            tr = _tile(self.rb, tr, unit=8)
        return tr, tc

    def spec(self, tr, tc, rc_of_grid):
        lead = self.lead
        sq = (None,) * len(lead)
        if self.kind == "2d":
            return pl.BlockSpec(sq + (tr, tc), lambda *g: lead + tuple(rc_of_grid(*g)))
        if self.kind == "col":
            q = self.cb // tc

            def im(*g):
                r, c = rc_of_grid(*g)
                return lead + (c // q, r, c % q)

            return pl.BlockSpec(sq + (None, tr, tc), im)
        q = self.rb // tr

        def im(*g):
            r, c = rc_of_grid(*g)
            return lead + (r // q, r % q, c)

        return pl.BlockSpec(sq + (None, tr, tc), im)


def _matmul(name, a, b, *, ta=False, tb=False, out_dtype=F32, tm=1024, tn=1024, tk=1024,
            out_colblocks=None, residual=None, carried=None):
    M, K = (a.C, a.R) if ta else (a.R, a.C)
    N, K2 = (b.R, b.C) if tb else (b.C, b.R)
    assert K == K2, (name, K, K2)
    tm, tn, tk = _tile(M, tm), _tile(N, tn), _tile(K, tk)
    if ta:
        tk, tm = a.fit(tk, tm)
    else:
        tm, tk = a.fit(tm, tk)
    if tb:
        tn, tk2 = b.fit(tn, tk)
    else:
        tk2, tn = b.fit(tk, tn)
    if tk2 != tk:
        tk = min(tk, tk2)
        if ta:
            tk, tm = a.fit(tk, tm)
        else:
            tm, tk = a.fit(tm, tk)
    if out_colblocks:
        tn = _tile(N // out_colblocks, tn)
    assert M % tm == 0 and N % tn == 0 and K % tk == 0, (name, M, N, K, tm, tn, tk)
    nk = K // tk
    assert nk == 1 or out_dtype == F32, name
    dims = (((0 if ta else 1,), (1 if tb else 0,)), ((), ()))
    grid = (M // tm, N // tn, nk)
    kind, g_blocks, g_layer = carried if carried else (None, [], None)
    ng = len(g_blocks)
    n_in = 2 + (residual is not None)

    def body(*refs):
        a_ref, b_ref = refs[:2]
        r_ref = refs[2] if residual is not None else None
        o_ref = refs[n_in + ng]
        k = pl.program_id(2)
        if carried:
            mine, start, finish = _carried_exchange(kind, refs[n_in:n_in + ng], refs[n_in + ng + 1:n_in + 2 * ng + 1],
                                                    refs[n_in + 2 * ng + 1:], g_layer)
            at = lambda step: functools.reduce(
                jnp.logical_and, [pl.program_id(d) == (0 if step == "first" else grid[d] - 1) for d in range(3)])

            @pl.when(jnp.logical_and(mine, at("first")))
            def _():
                start()

        def product():
            return lax.dot_general(a_ref[...], b_ref[...], dims, preferred_element_type=F32)

        if nk == 1:
            total = product()
            if r_ref is not None:
                total = total + r_ref[...]
            o_ref[...] = total.astype(out_dtype)
        else:
            @pl.when(k == 0)
            def _():
                o_ref[...] = product() + r_ref[...] if r_ref is not None else product()

            @pl.when(k > 0)
            def _():
                o_ref[...] += product()

        if carried:
            @pl.when(at("last"))
            def _():
                finish()

    a_spec = a.spec(tk, tm, lambda i, j, k: (k, i)) if ta else a.spec(tm, tk, lambda i, j, k: (i, k))
    b_spec = b.spec(tn, tk, lambda i, j, k: (j, k)) if tb else b.spec(tk, tn, lambda i, j, k: (k, j))
    in_specs, args = [a_spec, b_spec], [a.arr, b.arr]
    if residual is not None:
        in_specs.append(pl.BlockSpec((tm, tn), lambda i, j, k: (i, j)))
        args.append(residual)
    in_specs += [HBM_SPEC] * ng
    args += list(g_blocks)
    if out_colblocks:
        q = (N // out_colblocks) // tn
        out_shape = jax.ShapeDtypeStruct((out_colblocks, M, N // out_colblocks), out_dtype)
        out_spec = pl.BlockSpec((None, tm, tn), lambda i, j, k: (j // q, i, j % q))
    else:
        out_shape = jax.ShapeDtypeStruct((M, N), out_dtype)
        out_spec = pl.BlockSpec((tm, tn), lambda i, j, k: (i, j))
    out = pl.pallas_call(
        body, name=name, grid=grid, in_specs=in_specs, out_specs=[out_spec] + [HBM_SPEC] * ng,
        out_shape=[out_shape] + _carried_out_shapes(kind, g_blocks),
        scratch_shapes=_carried_sems(kind, ng),
        compiler_params=_params("arbitrary", "arbitrary", "arbitrary"),
    )(*args)
    return (out[0], list(out[1:])) if carried else out[0]


def _rmsnorm_fwd(name, x, gain):
    S, D = x.shape
    tr = _tile(S, 512)

    def body(x_ref, g_ref, h_ref, ht_ref):
        xv = x_ref[...]
        r = lax.rsqrt(jnp.mean(xv * xv, axis=-1, keepdims=True) + EPS)
        h = xv * r * g_ref[...]
        h_ref[...] = h.astype(BF16)
        ht_ref[...] = h.T.astype(BF16)

    return pl.pallas_call(
        body, name=name, grid=(S // tr,),
        in_specs=[pl.BlockSpec((tr, D), lambda i: (i, 0)), pl.BlockSpec((1, D), lambda i: (0, 0))],
        out_specs=[pl.BlockSpec((tr, D), lambda i: (i, 0)), pl.BlockSpec((D, tr), lambda i: (0, i))],
        out_shape=[jax.ShapeDtypeStruct((S, D), BF16), jax.ShapeDtypeStruct((D, S), BF16)],
        compiler_params=_params("arbitrary"),
    )(x, gain.reshape(1, D))


def _rmsnorm_bwd(name, x, gain, dh, g_res):
    S, D = x.shape
    tr = _tile(S, 256, unit=8)

    def body(x_ref, g_ref, dh_ref, res_ref, dx_ref, dxb_ref, dg_ref):
        i = pl.program_id(0)
        xv = x_ref[...]
        r = lax.rsqrt(jnp.mean(xv * xv, axis=-1, keepdims=True) + EPS)
        xhat = xv * r
        dhv = dh_ref[...]
        part = jnp.sum(dhv * xhat, axis=0, keepdims=True)

        @pl.when(i == 0)
        def _():
            dg_ref[...] = part

        @pl.when(i > 0)
        def _():
            dg_ref[...] += part

        dxhat = dhv * g_ref[...]
        dx = res_ref[...] + r * (dxhat - xhat * jnp.mean(dxhat * xhat, axis=-1, keepdims=True))
        dx_ref[...] = dx
        dxb_ref[...] = dx.astype(BF16)

    row = pl.BlockSpec((tr, D), lambda i: (i, 0))
    vec = pl.BlockSpec((1, D), lambda i: (0, 0))
    return pl.pallas_call(
        body, name=name, grid=(S // tr,), in_specs=[row, vec, row, row], out_specs=[row, row, vec],
        out_shape=[jax.ShapeDtypeStruct((S, D), F32), jax.ShapeDtypeStruct((S, D), BF16),
                   jax.ShapeDtypeStruct((1, D), F32)],
        compiler_params=_params("arbitrary"),
    )(x, gain.reshape(1, D), dh, g_res)


def _loss_grad(x, target):
    S, D = x.shape
    tr = _tile(S, 512, unit=8)

    def body(x_ref, t_ref, g_ref, gb_ref, l_ref):
        i = pl.program_id(0)
        e = x_ref[...] - t_ref[...]
        g = e * (1.0 / D)
        g_ref[...] = g
        gb_ref[...] = g.astype(BF16)
        part = jnp.sum(e * e, axis=0, keepdims=True)

        @pl.when(i == 0)
        def _():
            l_ref[...] = part

        @pl.when(i > 0)
        def _():
            l_ref[...] += part

    row = pl.BlockSpec((tr, D), lambda i: (i, 0))
    vec = pl.BlockSpec((1, D), lambda i: (0, 0))
    return pl.pallas_call(
        body, name="loss_grad", grid=(S // tr,), in_specs=[row, row], out_specs=[row, row, vec],
        out_shape=[jax.ShapeDtypeStruct((S, D), F32), jax.ShapeDtypeStruct((S, D), BF16),
                   jax.ShapeDtypeStruct((1, D), F32)],
        compiler_params=_params("arbitrary"),
    )(x, target)


def _gate_fwd(name, p, v_gain, wc, bs_t):
    S, W3 = p.shape
    W = W3 // 3
    G = wc.shape[0]
    gd = W // G

    def body(p_ref, gv_ref, wc_ref, bs_ref, y_ref):
        vg = _gelu(p_ref[:, W:2 * W].astype(F32))
        r = lax.rsqrt(jnp.mean(vg * vg, axis=-1, keepdims=True) + EPS)
        vb = (vg * r * gv_ref[...]).astype(BF16)
        zp = p_ref[:, 2 * W:].astype(F32)
        gate = _gelu(p_ref[:, :W].astype(F32)) * (zp * _sigmoid(zp))
        for g in range(G):
            sl = slice(g * gd, (g + 1) * gd)
            mixed = jnp.dot(wc_ref[g], vb[:, sl], preferred_element_type=F32) + bs_ref[:, g:g + 1]
            y_ref[:, sl] = (gate[:, sl] * mixed).astype(BF16)

    return pl.pallas_call(
        body, name=name, grid=(S // CHUNK,),
        in_specs=[pl.BlockSpec((CHUNK, W3), lambda i: (i, 0)), pl.BlockSpec((1, W), lambda i: (0, 0)),
                  pl.BlockSpec((G, CHUNK, CHUNK), lambda i: (0, 0, 0)), pl.BlockSpec((CHUNK, G), lambda i: (0, 0))],
        out_specs=pl.BlockSpec((CHUNK, W), lambda i: (i, 0)),
        out_shape=jax.ShapeDtypeStruct((S, W), BF16),
        compiler_params=_params("arbitrary"),
    )(p, v_gain.reshape(1, W), wc, bs_t)


def _gate_bwd(name, p, dy, v_gain, wc, wc_t, bs_t):
    S, W3 = p.shape
    W = W3 // 3
    G = wc.shape[0]
    gd = W // G

    def body(p_ref, dy_ref, gv_ref, wc_ref, wct_ref, bs_ref, dp_ref, dws_ref, dbs_ref, dgv_ref, dv_scr):
        i = pl.program_id(0)

        @pl.when(i == 0)
        def _():
            dws_ref[...] = jnp.zeros_like(dws_ref)
            dbs_ref[...] = jnp.zeros_like(dbs_ref)
            dgv_ref[...] = jnp.zeros_like(dgv_ref)

        gu, dgu = _gelu_and_grad(p_ref[:, :W].astype(F32))
        vg, dvg_dv = _gelu_and_grad(p_ref[:, W:2 * W].astype(F32))
        zp = p_ref[:, 2 * W:].astype(F32)
        sig = _sigmoid(zp)
        sz = zp * sig
        dsz = sig * (1.0 + zp * (1.0 - sig))
        r = lax.rsqrt(jnp.mean(vg * vg, axis=-1, keepdims=True) + EPS)
        vhat = vg * r
        gv = gv_ref[...]
        vb = (vhat * gv).astype(BF16)
        dy = dy_ref[...].astype(F32)
        lane = lax.broadcasted_iota(jnp.int32, (CHUNK, LANES), 1)
        dbs = jnp.zeros((CHUNK, LANES), F32)
        for g in range(G):
            sl = slice(g * gd, (g + 1) * gd)
            vsl = vb[:, sl]
            mixed = jnp.dot(wc_ref[g], vsl, preferred_element_type=F32) + bs_ref[:, g:g + 1]
            dyg, gug, szg = dy[:, sl], gu[:, sl], sz[:, sl]
            dp_ref[:, sl] = (dyg * mixed * szg * dgu[:, sl]).astype(BF16)
            dp_ref[:, 2 * W + g * gd:2 * W + (g + 1) * gd] = (dyg * gug * mixed * dsz[:, sl]).astype(BF16)
            dm = dyg * gug * szg
            dmb = dm.astype(BF16)
            dws_ref[g] += lax.dot_general(dmb, vsl, _NT, preferred_element_type=F32)
            dbs = dbs + jnp.where(lane == g, jnp.sum(dm, axis=1, keepdims=True), 0.0)
            dv_scr[:, sl] = jnp.dot(wct_ref[g], dmb, preferred_element_type=F32)
        dbs_ref[...] += dbs
        dv = dv_scr[...]
        dgv_ref[...] += jnp.sum(dv * vhat, axis=0, keepdims=True)
        dvhat = dv * gv
        dvg = r * (dvhat - vhat * jnp.mean(dvhat * vhat, axis=-1, keepdims=True))
        dp_ref[:, W:2 * W] = (dvg * dvg_dv).astype(BF16)

    return pl.pallas_call(
        body, name=name, grid=(S // CHUNK,),
        in_specs=[pl.BlockSpec((CHUNK, W3), lambda i: (i, 0)), pl.BlockSpec((CHUNK, W), lambda i: (i, 0)),
                  pl.BlockSpec((1, W), lambda i: (0, 0)),
                  pl.BlockSpec((G, CHUNK, CHUNK), lambda i: (0, 0, 0)),
                  pl.BlockSpec((G, CHUNK, CHUNK), lambda i: (0, 0, 0)),
                  pl.BlockSpec((CHUNK, G), lambda i: (0, 0))],
        out_specs=[pl.BlockSpec((CHUNK, W3), lambda i: (i, 0)),
                   pl.BlockSpec((G, CHUNK, CHUNK), lambda i: (0, 0, 0)),
                   pl.BlockSpec((CHUNK, LANES), lambda i: (0, 0)),
                   pl.BlockSpec((1, W), lambda i: (0, 0))],
        out_shape=[jax.ShapeDtypeStruct((S, W3), BF16), jax.ShapeDtypeStruct((G, CHUNK, CHUNK), F32),
                   jax.ShapeDtypeStruct((CHUNK, LANES), F32), jax.ShapeDtypeStruct((1, W), F32)],
        scratch_shapes=[pltpu.VMEM((CHUNK, W), F32)],
        compiler_params=_params("arbitrary"),
    )(p, dy, v_gain.reshape(1, W), wc, wc_t, bs_t)


AUG = 2 * HEAD
LOG2E = 1.0 / math.log(2.0)
Q_SUM_LANE = HEAD + 3
K_SUM_LANE = HEAD


def _pieces(x, sign=1.0):
    hi, mid, lo = _split3(sign * x)
    return hi.astype(F32), mid.astype(F32), lo.astype(F32)


def _lanes(lane, start, vals, rest):
    out = rest
    for n, v in enumerate(vals):
        out = jnp.where(lane == start + n, v, out)
    return out


def _qkv_prep(name, proj, cum, q_gain, k_gain, H):
    S = proj.shape[0]
    HW = H * HEAD
    tr = _tile(S, 256, unit=8)
    sigma = (HEAD ** -0.5) * LOG2E

    def body(q_ref, k_ref, v_ref, c_ref, gq_ref, gk_ref, qa_ref, ka_ref, va_ref):
        lane = lax.broadcasted_iota(jnp.int32, (tr, HEAD), 1)
        zero = jnp.zeros((tr, HEAD), F32)
        v_aug = jnp.where(lane < 3, 1.0, zero).astype(BF16)
        for h in range(H):
            sl = slice(h * HEAD, (h + 1) * HEAD)
            a0 = h * AUG
            t = q_ref[:, sl].astype(F32)
            r = lax.rsqrt(jnp.mean(t * t, axis=-1, keepdims=True) + EPS)
            qa_ref[:, a0:a0 + HEAD] = (t * r * gq_ref[...] * sigma).astype(BF16)
            t = k_ref[:, sl].astype(F32)
            r = lax.rsqrt(jnp.mean(t * t, axis=-1, keepdims=True) + EPS)
            ka_ref[:, a0:a0 + HEAD] = (t * r * gk_ref[...]).astype(BF16)
            va_ref[:, a0:a0 + HEAD] = v_ref[:, sl]
            va_ref[:, a0 + HEAD:a0 + AUG] = v_aug
            c2 = c_ref[:, h:h + 1] * LOG2E
            qa_ref[:, a0 + HEAD:a0 + AUG] = _lanes(lane, 0, _pieces(c2) + (1.0, 1.0, 1.0), zero).astype(BF16)
            ka_ref[:, a0 + HEAD:a0 + AUG] = _lanes(
                lane, 0, (1.0, 1.0, 1.0) + _pieces(c2, -1.0) + (1.0, 1.0, 1.0), zero).astype(BF16)

    col = lambda c: pl.BlockSpec((tr, HW), lambda i: (i, c))
    vec = pl.BlockSpec((1, HEAD), lambda i: (0, 0))
    aug = pl.BlockSpec((tr, H * AUG), lambda i: (i, 0))
    return pl.pallas_call(
        body, name=name, grid=(S // tr,),
        in_specs=[col(0), col(1), col(2), pl.BlockSpec((tr, LANES), lambda i: (i, 0)), vec, vec],
        out_specs=[aug] * 3, out_shape=[jax.ShapeDtypeStruct((S, H * AUG), BF16)] * 3,
        compiler_params=_params("arbitrary"),
    )(proj, proj, proj, cum, q_gain.reshape(1, HEAD), k_gain.reshape(1, HEAD))


def _split3(x):
    hi = x.astype(BF16)
    r1 = x - hi.astype(F32)
    mid = r1.astype(BF16)
    lo = (r1 - mid.astype(F32)).astype(BF16)
    return hi, mid, lo


def _tri_sum(tri, x):
    hi, mid, lo = _split3(x)
    d = lambda t: jnp.dot(tri, t, preferred_element_type=F32)
    return d(hi) + (d(mid) + d(lo))


def _log_sigmoid(x):
    return jnp.minimum(x, 0.0) - jnp.log(1.0 + jnp.exp(-jnp.abs(x)))


def _fox_cum(name, f, bias):
    S = f.shape[0]
    tb = _tile(S, 256, unit=8)

    def body(f_ref, b_ref, c_ref):
        rr = lax.broadcasted_iota(jnp.int32, (tb, tb), 0)
        cc = lax.broadcasted_iota(jnp.int32, (tb, tb), 1)
        tri = (rr >= cc).astype(BF16)

        def step(t, carry):
            off = pl.multiple_of(t * tb, tb)
            lf = _log_sigmoid(f_ref[pl.ds(off, tb), :] + b_ref[...])
            c = _tri_sum(tri, lf) + carry
            c_ref[pl.ds(off, tb), :] = c
            return c[tb - 1:tb, :]

        lax.fori_loop(0, S // tb, step, jnp.zeros((1, LANES), F32))

    return pl.pallas_call(
        body, name=name, out_shape=jax.ShapeDtypeStruct((S, LANES), F32),
        in_specs=[pl.BlockSpec(memory_space=pltpu.VMEM)] * 2, out_specs=pl.BlockSpec(memory_space=pltpu.VMEM),
        compiler_params=pltpu.CompilerParams(vmem_limit_bytes=VMEM_LIMIT),
    )(f, bias)


def _fox_cum_bwd(name, dcum, f, bias):
    S = f.shape[0]
    tb = _tile(S, 256, unit=8)
    nb = S // tb

    def body(dc_ref, f_ref, b_ref, df_ref, db_ref):
        rr = lax.broadcasted_iota(jnp.int32, (tb, tb), 0)
        cc = lax.broadcasted_iota(jnp.int32, (tb, tb), 1)
        tri = (rr <= cc).astype(BF16)

        def step(t, carry):
            tail, dbias = carry
            off = pl.multiple_of((nb - 1 - t) * tb, tb)
            dlf = _tri_sum(tri, dc_ref[pl.ds(off, tb), :]) + tail
            d = dlf * _sigmoid(-(f_ref[pl.ds(off, tb), :] + b_ref[...]))
            df_ref[pl.ds(off, tb), :] = d.astype(BF16)
            return dlf[0:1, :], dbias + jnp.sum(d, axis=0, keepdims=True)

        z = jnp.zeros((1, LANES), F32)
        _, dbias = lax.fori_loop(0, nb, step, (z, z))
        db_ref[...] = dbias

    vm = pl.BlockSpec(memory_space=pltpu.VMEM)
    return pl.pallas_call(
        body, name=name, out_shape=[jax.ShapeDtypeStruct((S, LANES), BF16), jax.ShapeDtypeStruct((1, LANES), F32)],
        in_specs=[vm] * 3, out_specs=[vm] * 2,
        compiler_params=pltpu.CompilerParams(vmem_limit_bytes=VMEM_LIMIT),
    )(dcum, f, bias)


def _attn_fwd(name, qa, ka, va, proj, H, tq, carried=None):
    S = qa.shape[0]
    HW = H * HEAD
    nq = S // tq
    hp = 2 if H % 2 == 0 else 1
    tw = 2 * tq if S % (2 * tq) == 0 else tq
    kind, g_blocks, g_layer = carried if carried else (None, [], None)
    ng = len(g_blocks)

    def body(*refs):
        q_ref, k_ref, v_ref, z_ref = refs[:4]
        o_ref, y_ref, lse_ref = refs[4 + ng:7 + ng]
        i = pl.program_id(1)
        if carried:
            mine, start, finish = _carried_exchange(kind, refs[4:4 + ng], refs[7 + ng:7 + 2 * ng],
                                                    refs[7 + 2 * ng:], g_layer)
            hd = pl.program_id(0)

            @pl.when(jnp.logical_and(mine, jnp.logical_and(hd == 0, i == 0)))
            def _():
                start()

        def step(j, carry, masked):
            off = pl.multiple_of(j * tw, tw)
            out = []
            for n in range(hp):
                m, acc = carry[n]
                a = slice(n * AUG, (n + 1) * AUG)
                s = lax.dot_general(q_ref[:, a], k_ref[pl.ds(off, tw), a], _NT, preferred_element_type=F32)
                if masked:
                    qry = i * tq + lax.broadcasted_iota(jnp.int32, (tq, tw), 0)
                    key = j * tw + lax.broadcasted_iota(jnp.int32, (tq, tw), 1)
                    s = jnp.where(qry >= key, s, -jnp.inf)
                m_new = jnp.maximum(m, jnp.max(s, axis=1, keepdims=True))
                pr = jnp.exp2(s - m_new).astype(BF16)
                acc = jnp.exp2(m - m_new) * acc + jnp.dot(pr, v_ref[pl.ds(off, tw), a], preferred_element_type=F32)
                out.append((m_new, acc))
            return tuple(out)

        init = ((jnp.full((tq, 1), -jnp.inf, F32), jnp.zeros((tq, AUG), F32)),) * hp
        below = (i * tq) // tw
        carry = lax.fori_loop(0, below, lambda j, c: step(j, c, False), init)
        carry = step(below, carry, True)
        for n in range(hp):
            m, acc = carry[n]
            sl = slice(n * HEAD, (n + 1) * HEAD)
            l = acc[:, HEAD:HEAD + 1]
            o = acc[:, :HEAD] / l
            z = z_ref[:, sl].astype(F32)
            o_ref[:, sl] = o
            y_ref[:, sl] = (o * (z * _sigmoid(z))).astype(BF16)
            lse_ref[n] = m + jnp.log(l) * LOG2E

        if carried:
            @pl.when(jnp.logical_and(hd == H // hp - 1, i == nq - 1))
            def _():
                finish()

    qspec = pl.BlockSpec((tq, hp * AUG), lambda h, i: (i, h))
    kvspec = pl.BlockSpec((S, hp * AUG), lambda h, i: (0, h))
    ospec = pl.BlockSpec((tq, hp * HEAD), lambda h, i: (i, h))
    out = pl.pallas_call(
        body, name=name, grid=(H // hp, nq),
        in_specs=[qspec, kvspec, kvspec, pl.BlockSpec((tq, hp * HEAD), lambda h, i: (i, 3 * H // hp + h))]
        + [HBM_SPEC] * ng,
        out_specs=[ospec, ospec, pl.BlockSpec((hp, tq, 1), lambda h, i: (h, i, 0))] + [HBM_SPEC] * ng,
        out_shape=[jax.ShapeDtypeStruct((S, HW), F32), jax.ShapeDtypeStruct((S, HW), BF16),
                   jax.ShapeDtypeStruct((H, S, 1), F32)] + _carried_out_shapes(kind, g_blocks),
        scratch_shapes=_carried_sems(kind, ng),
        compiler_params=_params("arbitrary", "arbitrary"),
    )(qa, ka, va, proj, *g_blocks)
    return out[0], out[1], out[2], list(out[3:])


def _attn_bwd_prep(name, dy, o, proj, qa, lse, H):
    S, HW = o.shape
    tr = _tile(S, 256, unit=8)

    def body(dy_ref, o_ref, z_ref, qa_ref, lse_ref, doa_ref, dz_ref, qab_ref):
        lane = lax.broadcasted_iota(jnp.int32, (tr, HEAD), 1)
        zero = jnp.zeros((tr, HEAD), F32)
        for h in range(H):
            sl = slice(h * HEAD, (h + 1) * HEAD)
            a0 = h * AUG
            dy = dy_ref[:, sl].astype(F32)
            z = z_ref[:, sl].astype(F32)
            o = o_ref[:, sl]
            sig = _sigmoid(z)
            dob = (dy * (z * sig)).astype(BF16)
            dz_ref[:, sl] = (dy * o * (sig * (1.0 + z * (1.0 - sig)))).astype(BF16)
            delta = jnp.sum(dob.astype(F32) * o, axis=1, keepdims=True)
            doa_ref[:, a0:a0 + HEAD] = dob
            doa_ref[:, a0 + HEAD:a0 + AUG] = _lanes(lane, 0, _pieces(delta, -1.0), zero).astype(BF16)
            qab_ref[:, a0:a0 + HEAD] = qa_ref[:, a0:a0 + HEAD]
            qab_ref[:, a0 + HEAD:a0 + AUG] = _lanes(
                lane, 6, _pieces(lse_ref[:, h:h + 1], -1.0), qa_ref[:, a0 + HEAD:a0 + AUG].astype(F32)).astype(BF16)

    row = pl.BlockSpec((tr, HW), lambda i: (i, 0))
    aug = pl.BlockSpec((tr, H * AUG), lambda i: (i, 0))
    return pl.pallas_call(
        body, name=name, grid=(S // tr,),
        in_specs=[row, row, pl.BlockSpec((tr, HW), lambda i: (i, 3)), aug, pl.BlockSpec((tr, LANES), lambda i: (i, 0))],
        out_specs=[aug, row, aug],
        out_shape=[jax.ShapeDtypeStruct((S, H * AUG), BF16), jax.ShapeDtypeStruct((S, HW), BF16),
                   jax.ShapeDtypeStruct((S, H * AUG), BF16)],
        compiler_params=_params("arbitrary"),
    )(dy, o, proj, qa, lse)


def _attn_bwd(name, qab, doa, ka, va, H, tq, carried=None):
    S = qab.shape[0]
    nq = S // tq
    tw = 2 * tq if S % (2 * tq) == 0 else tq
    sums, owner = carried if carried else ([], None)
    nt = len(sums)

    def body(*refs):
        q_ref, do_ref, k_ref, v_ref = refs[:4]
        dq_ref, dk_ref, dv_ref = refs[4 + nt:7 + nt]
        j = pl.program_id(1)
        if carried:
            mine, start, wait = _scatter_copies(refs[4:4 + nt], refs[7 + nt:7 + 2 * nt], *refs[7 + 2 * nt:], owner)
            hd = pl.program_id(0)

            @pl.when(jnp.logical_and(mine, jnp.logical_and(hd == 0, j == 0)))
            def _():
                start()

        @pl.when(j == 0)
        def _():
            dq_ref[...] = jnp.zeros_like(dq_ref)

        k = k_ref[...]
        v = v_ref[...]

        def step(i, carry, masked):
            dk_acc, dv_acc = carry
            off = pl.multiple_of(i * tw, tw)
            q = q_ref[pl.ds(off, tw), :]
            do = do_ref[pl.ds(off, tw), :]
            st = lax.dot_general(k, q, _NT, preferred_element_type=F32)
            if masked:
                key = j * tq + lax.broadcasted_iota(jnp.int32, (tq, tw), 0)
                qry = i * tw + lax.broadcasted_iota(jnp.int32, (tq, tw), 1)
                st = jnp.where(qry >= key, st, -jnp.inf)
            pt = jnp.exp2(st)
            dst = pt * lax.dot_general(v, do, _NT, preferred_element_type=F32)
            dsb = dst.astype(BF16)
            dv_acc = dv_acc + jnp.dot(pt.astype(BF16), do[:, :HEAD], preferred_element_type=F32)
            dk_acc = dk_acc + jnp.dot(dsb, q, preferred_element_type=F32)
            dq_ref[pl.ds(off, tw), :] += lax.dot_general(dsb, k, _TN, preferred_element_type=F32)
            return dk_acc, dv_acc

        first = (j * tq) // tw
        carry = step(first, (jnp.zeros((tq, AUG), F32), jnp.zeros((tq, HEAD), F32)), True)
        dk_acc, dv_acc = lax.fori_loop(first + 1, S // tw, lambda i, c: step(i, c, False), carry)
        dk_ref[...] = dk_acc
        dv_ref[...] = dv_acc
        if carried:
            @pl.when(jnp.logical_and(mine, jnp.logical_and(hd == H - 1, j == nq - 1)))
            def _():
                wait()

    full = pl.BlockSpec((S, AUG), lambda h, j: (0, h))
    blk = pl.BlockSpec((tq, AUG), lambda h, j: (j, h))
    out = pl.pallas_call(
        body, name=name, grid=(H, nq),
        in_specs=[full, full, blk, blk] + [HBM_SPEC] * nt,
        out_specs=[full, blk, pl.BlockSpec((tq, HEAD), lambda h, j: (j, h))] + [HBM_SPEC] * nt,
        out_shape=[jax.ShapeDtypeStruct((S, H * AUG), F32), jax.ShapeDtypeStruct((S, H * AUG), F32),
                   jax.ShapeDtypeStruct((S, H * HEAD), F32)] + [jax.ShapeDtypeStruct(s.shape, F32) for s in sums],
        scratch_shapes=[pltpu.SemaphoreType.DMA((3 * nt,))] * 2 if carried else [],
        compiler_params=_params("arbitrary", "arbitrary"),
    )(qab, doa, ka, va, *sums)
    return out[0], out[1], out[2], list(out[3:])


def _qk_bwd(name, proj, dqa, dka, dv, dz, q_gain, k_gain, H):
    S = proj.shape[0]
    HW = H * HEAD
    tr = _tile(S, 256, unit=8)
    scale = HEAD ** -0.5
    factors = (scale, 1.0 / LOG2E)

    def body(q_ref, k_ref, dq_ref, dk_ref, dv_ref, dz_ref, gq_ref, gk_ref, dp_ref, dgq_ref, dgk_ref, dc_ref):
        i = pl.program_id(0)

        @pl.when(i == 0)
        def _():
            dgq_ref[...] = jnp.zeros_like(dgq_ref)
            dgk_ref[...] = jnp.zeros_like(dgk_ref)

        for n, (src, dsrc, gain, dgain) in enumerate(((q_ref, dq_ref, gq_ref, dgq_ref), (k_ref, dk_ref, gk_ref, dgk_ref))):
            acc = jnp.zeros((1, HEAD), F32)
            for h in range(H):
                sl = slice(h * HEAD, (h + 1) * HEAD)
                t = src[:, sl].astype(F32)
                r = lax.rsqrt(jnp.mean(t * t, axis=-1, keepdims=True) + EPS)
                that = t * r
                dn = dsrc[:, h * AUG:h * AUG + HEAD] * factors[n]
                acc = acc + jnp.sum(dn * that, axis=0, keepdims=True)
                dhat = dn * gain[...]
                dt = r * (dhat - that * jnp.mean(dhat * that, axis=-1, keepdims=True))
                dp_ref[:, n * HW + h * HEAD:n * HW + (h + 1) * HEAD] = dt.astype(BF16)
            dgain[...] += acc
        dp_ref[:, 2 * HW:3 * HW] = dv_ref[...].astype(BF16)
        dp_ref[:, 3 * HW:] = dz_ref[...]
        lane = lax.broadcasted_iota(jnp.int32, (tr, LANES), 1)
        dc = jnp.zeros((tr, LANES), F32)
        for h in range(H):
            qs = dq_ref[:, h * AUG + K_SUM_LANE:h * AUG + K_SUM_LANE + 1]
            ks = dk_ref[:, h * AUG + Q_SUM_LANE:h * AUG + Q_SUM_LANE + 1]
            dc = jnp.where(lane == h, qs - ks, dc)
        dc_ref[...] = dc

    col = lambda c: pl.BlockSpec((tr, HW), lambda i: (i, c))
    row = col(0)
    aug = pl.BlockSpec((tr, H * AUG), lambda i: (i, 0))
    vec = pl.BlockSpec((1, HEAD), lambda i: (0, 0))
    return pl.pallas_call(
        body, name=name, grid=(S // tr,),
        in_specs=[col(0), col(1), aug, aug, row, row, vec, vec],
        out_specs=[pl.BlockSpec((tr, 4 * HW), lambda i: (i, 0)), vec, vec, pl.BlockSpec((tr, LANES), lambda i: (i, 0))],
        out_shape=[jax.ShapeDtypeStruct((S, 4 * HW), BF16), jax.ShapeDtypeStruct((1, HEAD), F32),
                   jax.ShapeDtypeStruct((1, HEAD), F32), jax.ShapeDtypeStruct((S, LANES), F32)],
        compiler_params=_params("arbitrary"),
    )(proj, proj, dqa, dka, dv, dz, q_gain.reshape(1, HEAD), k_gain.reshape(1, HEAD))


def _row_tile(R, C, budget_bytes=1 << 20):
    cap = max(8, budget_bytes // (4 * C))
    t = (min(cap, R) // 8) * 8
    while t >= 8:
        if R % t == 0:
            return t
        t -= 8
    return R


def _add_if(name, a, b, active):
    R, C = a.shape
    tr = _row_tile(R, C)

    def body(act_ref, a_ref, b_ref, o_ref):
        @pl.when(act_ref[0] != 0)
        def _():
            o_ref[...] = a_ref[...] + b_ref[...]

    blk = pl.BlockSpec((tr, C), lambda i, s: (i * s[0], 0))
    grid_spec = pltpu.PrefetchScalarGridSpec(
        num_scalar_prefetch=1, grid=(R // tr,), in_specs=[blk, blk], out_specs=blk)
    return pl.pallas_call(
        body, name=name, grid_spec=grid_spec, out_shape=jax.ShapeDtypeStruct((R, C), F32),
        compiler_params=_params("arbitrary"),
    )(active, a, b)


def _sum_slots(name, slots, active):
    n, R, C = slots.shape
    tr = _row_tile(R, C, budget_bytes=(1 << 20) // 2)

    def body(act_ref, s_ref, o_ref):
        @pl.when(act_ref[0] != 0)
        def _():
            acc = s_ref[0]
            for k in range(1, n):
                acc = acc + s_ref[k]
            o_ref[...] = acc

    grid_spec = pltpu.PrefetchScalarGridSpec(
        num_scalar_prefetch=1, grid=(R // tr,),
        in_specs=[pl.BlockSpec((n, tr, C), lambda i, s: (0, i * s[0], 0))],
        out_specs=pl.BlockSpec((tr, C), lambda i, s: (i * s[0], 0)))
    return pl.pallas_call(
        body, name=name, grid_spec=grid_spec, out_shape=jax.ShapeDtypeStruct((R, C), F32),
        compiler_params=_params("arbitrary"),
    )(active, slots)


def _adamw(name, w, g, m, v):
    R, C = w.shape
    tr = _row_tile(R, C, budget_bytes=(1 << 20) // 2)
    c1 = 1.0 - ADAM_B1 ** ADAM_STEP
    c2 = 1.0 - ADAM_B2 ** ADAM_STEP

    def body(w_ref, g_ref, m_ref, v_ref, d_ref, nm_ref, nv_ref):
        gv = g_ref[...]
        nm = ADAM_B1 * m_ref[...] + (1.0 - ADAM_B1) * gv
        nv = ADAM_B2 * v_ref[...] + (1.0 - ADAM_B2) * (gv * gv)
        m_hat = nm / c1
        v_hat = nv / c2
        d_ref[...] = -ADAM_LR * (m_hat / (jnp.sqrt(v_hat) + ADAM_EPS) + ADAM_WD * w_ref[...])
        nm_ref[...] = nm
        nv_ref[...] = nv

    blk = pl.BlockSpec((tr, C), lambda i: (i, 0))
    return pl.pallas_call(
        body, name=name, grid=(R // tr,), in_specs=[blk] * 4, out_specs=[blk] * 3,
        out_shape=[jax.ShapeDtypeStruct((R, C), F32)] * 3,
        compiler_params=_params("arbitrary"),
    )(w, g, m, v)


def _place():
    x, y, c = lax.axis_index("x"), lax.axis_index("y"), lax.axis_index("c")
    chips = [(1 - x, y), (x, 1 - y), (1 - x, 1 - y)]
    return x, y, c, chips


def _gather_copies(ins, outs, sems, layer):
    s_send, s_recv, f_send, f_recv = sems
    nt = len(ins)
    x, y, c, chips = _place()
    me = 2 * x + y
    ids = [2 * cx + cy for cx, cy in chips]
    pairs = [(t, k) for t in range(nt) for k in range(3)]

    def over_ici(t, k, block):
        return pltpu.make_async_remote_copy(
            src_ref=ins[t], dst_ref=outs[t].at[block], send_sem=s_send.at[3 * t + k],
            recv_sem=s_recv.at[3 * t + k], device_id=(*chips[k], layer), device_id_type=MESH)

    def over_d2d(t, k):
        blk = outs[t].at[ids[k]]
        return pltpu.make_async_remote_copy(
            src_ref=blk, dst_ref=blk, send_sem=f_send.at[3 * t + k], recv_sem=f_recv.at[3 * t + k],
            device_id=(x, y, 1 - c), device_id_type=MESH)

    def start():
        for t, k in pairs:
            over_ici(t, k, me).start()

    def finish():
        @pl.when(c == layer)
        def _():
            for t, k in pairs:
                over_ici(t, k, ids[k]).wait_recv()
                over_d2d(t, k).start()
            for t, k in pairs:
                over_ici(t, k, me).wait_send()
                over_d2d(t, k).wait_send()

        @pl.when(c != layer)
        def _():
            for t, k in pairs:
                over_d2d(t, k).wait_recv()

    return c == layer, start, finish


def _gather_out_shapes(blocks):
    return [jax.ShapeDtypeStruct((N_CHIPS,) + b.shape, b.dtype) for b in blocks]


def _gather_sems(n):
    return [pltpu.SemaphoreType.DMA((3 * n,))] * 4


def _gather_weights(name, blocks, layer):
    nt = len(blocks)

    def body(*refs):
        mine, start, finish = _gather_copies(refs[:nt], refs[nt:2 * nt], refs[2 * nt:], layer)

        @pl.when(mine)
        def _():
            start()

        finish()

    return pl.pallas_call(
        body, name=name, out_shape=_gather_out_shapes(blocks),
        in_specs=[HBM_SPEC] * nt, out_specs=[HBM_SPEC] * nt, scratch_shapes=_gather_sems(nt),
    )(*blocks)


def _send_copies(ins, outs, s_send, s_recv, owner):
    nt = len(ins)
    x, y, c, _ = _place()

    def copy(t):
        return pltpu.make_async_remote_copy(
            src_ref=ins[t], dst_ref=outs[t], send_sem=s_send.at[t], recv_sem=s_recv.at[t],
            device_id=(x, y, owner), device_id_type=MESH)

    def start():
        for t in range(nt):
            copy(t).start()

    def finish():
        @pl.when(c != owner)
        def _():
            for t in range(nt):
                copy(t).wait_send()

        @pl.when(c == owner)
        def _():
            for t in range(nt):
                copy(t).wait_recv()

    return c != owner, start, finish


def _send_to_owner(name, grads, owner):
    nt = len(grads)

    def body(*refs):
        mine, start, finish = _send_copies(refs[:nt], refs[nt:2 * nt], *refs[2 * nt:], owner)

        @pl.when(mine)
        def _():
            start()

        finish()

    return pl.pallas_call(
        body, name=name,
        out_shape=[jax.ShapeDtypeStruct(g.shape, F32) for g in grads],
        in_specs=[HBM_SPEC] * nt, out_specs=[HBM_SPEC] * nt,
        scratch_shapes=[pltpu.SemaphoreType.DMA((nt,))] * 2,
    )(*grads)


def _scatter_copies(ins, outs, s_send, s_recv, owner):
    nt = len(ins)
    x, y, c, chips = _place()
    me = 2 * x + y
    ids = [2 * cx + cy for cx, cy in chips]
    pairs = [(t, k) for t in range(nt) for k in range(3)]

    def copy(t, k, slot):
        return pltpu.make_async_remote_copy(
            src_ref=ins[t].at[ids[k]], dst_ref=outs[t].at[slot], send_sem=s_send.at[3 * t + k],
            recv_sem=s_recv.at[3 * t + k], device_id=(*chips[k], owner), device_id_type=MESH)

    def start():
        for t, k in pairs:
            copy(t, k, me).start()

    def wait():
        for t, k in pairs:
            copy(t, k, ids[k]).wait()

    return c == owner, start, wait


def _carried_exchange(kind, ins, outs, sems, layer):
    if kind == "gather":
        return _gather_copies(ins, outs, sems, layer)
    if kind == "send":
        return _send_copies(ins, outs, *sems, layer)
    mine, start, wait = _scatter_copies(ins, outs, *sems, layer)
    return mine, start, lambda: pl.when(mine)(wait)


def _carried_out_shapes(kind, arrays):
    if kind == "gather":
        return _gather_out_shapes(arrays)
    return [jax.ShapeDtypeStruct(a.shape, F32) for a in arrays]


def _carried_sems(kind, n):
    if kind is None:
        return []
    if kind == "gather":
        return _gather_sems(n)
    return [pltpu.SemaphoreType.DMA((n if kind == "send" else 3 * n,))] * 2


def _last_exchanges(reduced, small):
    nt = len(reduced)

    def body(*refs):
        ins, small_ref = refs[:nt], refs[nt]
        outs, slots_ref = refs[nt + 1:2 * nt + 1], refs[2 * nt + 1]
        s_send, s_recv, a_send, a_recv = refs[2 * nt + 2:]
        x, y, c, _ = _place()
        flips = [(fx, fy, fc) for fx in (0, 1) for fy in (0, 1) for fc in (0, 1)][1:]

        def peer(f):
            return tuple(1 - a if flip else a for a, flip in zip((x, y, c), f))

        def slot(p):
            return 4 * p[0] + 2 * p[1] + p[2]

        def swap(t):
            return pltpu.make_async_remote_copy(
                src_ref=ins[t], dst_ref=outs[t], send_sem=s_send.at[t], recv_sem=s_recv.at[t],
                device_id=(x, y, 1 - c), device_id_type=MESH)

        def to_all(k, owner):
            return pltpu.make_async_remote_copy(
                src_ref=small_ref, dst_ref=slots_ref.at[slot(owner)], send_sem=a_send.at[k], recv_sem=a_recv.at[k],
                device_id=peer(flips[k]), device_id_type=MESH)

        for t in range(nt):
            swap(t).start()
        for k in range(7):
            to_all(k, (x, y, c)).start()
        for t in range(nt):
            swap(t).wait()
        for k in range(7):
            to_all(k, peer(flips[k])).wait()

    out = pl.pallas_call(
        body, name="last_exchanges",
        out_shape=[jax.ShapeDtypeStruct(r.shape, F32) for r in reduced]
        + [jax.ShapeDtypeStruct((8,) + small.shape, F32)],
        in_specs=[HBM_SPEC] * (nt + 1), out_specs=[HBM_SPEC] * (nt + 1),
        scratch_shapes=[pltpu.SemaphoreType.DMA((nt,))] * 2 + [pltpu.SemaphoreType.DMA((7,))] * 2,
    )(*reduced, small)
    return list(out[:nt]), out[nt]


def _rows128(a):
    flat = a.reshape(-1)
    rows = -(-flat.shape[0] // LANES)
    rows8 = -(-rows // 8) * 8
    flat = jnp.pad(flat, (0, rows8 * LANES - flat.shape[0]))
    return flat.reshape(rows8, LANES)


def _pack(parts):
    return jnp.concatenate([_rows128(p) for p in parts], axis=0)


def _unpack(buf, shapes):
    out, r = [], 0
    for shp in shapes:
        n = int(np.prod(shp))
        rows8 = -(-(-(-n // LANES)) // 8) * 8
        out.append(buf[r:r + rows8].reshape(-1)[:n].reshape(shp))
        r += rows8
    return out


def kernel(x, a_norm_g, a_w_in, a_v_norm_g, a_w_s, a_b_s, a_w_out, b_norm_g, b_w_in, b_f_bias, b_q_norm_g, b_k_norm_g, b_w_out, loss_target, m_a_norm_g, m_a_w_in, m_a_v_norm_g, m_a_w_s, m_a_b_s, m_a_w_out, m_b_norm_g, m_b_w_in, m_b_f_bias, m_b_q_norm_g, m_b_k_norm_g, m_b_w_out, v_a_norm_g, v_a_w_in, v_a_v_norm_g, v_a_w_s, v_a_b_s, v_a_w_out, v_b_norm_g, v_b_w_in, v_b_f_bias, v_b_q_norm_g, v_b_k_norm_g, v_b_w_out):
    xs = x[0]
    target = loss_target[0]
    S, D = xs.shape
    n_layers = a_w_in.shape[0]
    assert n_layers == 2
    W = a_v_norm_g.shape[1]
    G = a_w_s.shape[1]
    H = b_f_bias.shape[1]
    HW = H * HEAD
    tq_fwd = _tile(S, 512)
    tq_bwd = _tile(S, 512)
    core = lax.axis_index("c")
    chip = 2 * lax.axis_index("x") + lax.axis_index("y")

    own = dict(a_w_in=a_w_in.astype(BF16), a_w_out=a_w_out.astype(BF16), b_w_in=b_w_in.astype(BF16),
               b_w_out=b_w_out.astype(BF16), b_norm_g=b_norm_g.reshape(n_layers, 1, -1))
    cb = b_w_in.shape[2]
    w_ain, w_aout, w_bmain, w_bf, w_bout, b_norm_full = ([None] * n_layers for _ in range(6))

    def blocks_of(tensors, layer):
        return [own[n][layer] for n in tensors]

    def take(tensors, layer, arrived):
        for n, got, mine in zip(tensors, arrived, blocks_of(tensors, layer)):
            full = lax.dynamic_update_slice(got, mine[None], (chip, 0, 0))
            if n == "a_w_in":
                w_ain[layer] = full
            elif n == "a_w_out":
                w_aout[layer] = full
            elif n == "b_w_out":
                w_bout[layer] = full
            elif n == "b_norm_g":
                b_norm_full[layer] = full.reshape(D)
            else:
                cols = jnp.transpose(full, (1, 0, 2)).reshape(D, N_CHIPS * cb)
                w_bmain[layer] = cols[:, :4 * HW]
                w_bf[layer] = jnp.pad(cols[:, 4 * HW:], ((0, 0), (0, LANES - H)))

    first_a = ("a_w_in", "a_w_out")
    first_b = ("b_w_in", "b_w_out", "b_norm_g")
    at_once = ("a_w_in",)
    on_a0_in = ("a_w_out",) + first_b
    take(at_once, 0, _gather_weights("gather_a0", blocks_of(at_once, 0), 0))
    causal = jnp.tril(jnp.ones((CHUNK, CHUNK), dtype=bool))
    wc = jnp.where(causal[None, None], a_w_s, 0).astype(BF16)
    wc_t = jnp.swapaxes(wc, 2, 3)
    bs_t = jnp.swapaxes(a_b_s, 1, 2)
    f_bias = jnp.pad(b_f_bias, ((0, 0), (0, LANES - H))).reshape(n_layers, 1, LANES)

    def view_ain(l):
        return _View(w_ain[l], "col")

    def view_aout(l):
        return _View(w_aout[l], "row")

    def view_bout(l):
        return _View(w_bout[l], "row")

    saved = []
    cur = xs
    for i in range(2 * n_layers):
        l = i // 2
        if i % 2 == 0:
            h, h_t = _rmsnorm_fwd(f"a{l}_norm", cur, a_norm_g[l])
            if i == 0:
                p, arrived = _matmul(f"a{l}_in", _View(h), view_ain(l), out_dtype=BF16, tm=1024, tn=1024, tk=2048,
                                     carried=("gather", blocks_of(on_a0_in, 0), 0))
                take(on_a0_in, 0, arrived)
            else:
                p = _matmul(f"a{l}_in", _View(h), view_ain(l), out_dtype=BF16, tm=1024, tn=1024, tk=2048)
            y = _gate_fwd(f"a{l}_gate", p, a_v_norm_g[l], wc[l], bs_t[l])
            nxt = _matmul(f"a{l}_out", _View(y), view_aout(l), tm=1024, tn=1024, tk=1024, residual=cur)
            saved.append((cur, h_t, p, y))
        else:
            h, h_t = _rmsnorm_fwd(f"b{l}_norm", cur, b_norm_full[l])
            proj = _matmul(f"b{l}_in", _View(h), _View(w_bmain[l]), out_dtype=BF16, tm=1024, tn=1024, tk=2048)
            f = _matmul(f"b{l}_inf", _View(h), _View(w_bf[l]), tm=1024, tn=LANES, tk=2048)
            cum = _fox_cum(f"b{l}_cum", f, f_bias[l])
            qa, ka, va = _qkv_prep(f"b{l}_qkv", proj, cum, b_q_norm_g[l], b_k_norm_g[l], H)
            if i == 1:
                o, y, lse, arrived = _attn_fwd(f"b{l}_attn", qa, ka, va, proj, H, tq_fwd,
                                               carried=("gather", blocks_of(first_a + first_b, 1), 1))
                take(first_a + first_b, 1, arrived)
            else:
                o, y, lse, _ = _attn_fwd(f"b{l}_attn", qa, ka, va, proj, H, tq_fwd)
            nxt = _matmul(f"b{l}_out", _View(y), view_bout(l), tm=1024, tn=1024, tk=1024, residual=cur)
            saved.append((cur, h_t, proj, f, qa, ka, va, o, y, lse))
        cur = nxt

    g, gb, lcols = _loss_grad(cur, target)
    loss = lax.psum(0.5 * jnp.sum(lcols) / D, ("x", "y", "c"))

    big = {"a_w_in": [None] * n_layers, "a_w_out": [None] * n_layers,
           "b_w_in": [None] * n_layers, "b_w_out": [None] * n_layers}
    small = {k: [None] * n_layers for k in
             ("a_norm_g", "a_v_norm_g", "a_w_s", "a_b_s", "b_norm_g", "b_f_bias", "b_q_norm_g", "b_k_norm_g")}
    names = ["a_w_in", "a_w_out", "b_w_in", "b_w_out"]
    reduced = [{} for _ in range(n_layers)]

    def chip_sums_of(tag, tensors, layer, arrived=()):
        arrived = dict(arrived)
        rest = [n for n in tensors if n not in arrived]
        arrived.update(zip(rest, _send_to_owner(f"to_owner{tag}", [big[n][layer] for n in rest], layer)))
        mine = [big[n][layer] for n in tensors]
        got = [arrived[n] for n in tensors]
        active = (core == layer).astype(jnp.int32).reshape(1)
        sums = []
        for n, a, b in zip(tensors, mine, got):
            shp = a.shape
            flat = lambda t: t.reshape(shp[0] * shp[1], shp[2])
            sums.append(_add_if(f"chipsum{tag}_{n}", flat(a), flat(b), active).reshape(shp))
        return sums, active

    def reduce_slots(tag, tensors, layer, sums, got, active):
        for n, g_, s_ in zip(tensors, got, sums):
            slots = lax.dynamic_update_slice(g_, lax.dynamic_index_in_dim(s_, chip, keepdims=True), (chip, 0, 0))
            reduced[layer][n] = _sum_slots(f"reduce{tag}_{n}", slots, active)

    for i in reversed(range(2 * n_layers)):
        l = i // 2
        if i % 2 == 0:
            x_in, h_t, p, y = saved[i]
            dims_dy = dict(tb=True, out_dtype=BF16, tm=1024, tn=1024, tk=2048)
            if i == 0:
                sent = ["b_w_in", "b_w_out"]
                dy, got = _matmul(f"a{l}_dy", _View(gb), view_aout(l), carried=("send", [big[n][0] for n in sent], 0),
                                  **dims_dy)
                sent0 = list(zip(sent, got))
            else:
                dy = _matmul(f"a{l}_dy", _View(gb), view_aout(l), **dims_dy)
            d_wout = _matmul(f"a{l}_dwout", _View(y), _View(gb), ta=True, tm=2048, tn=1024, tk=1024)
            big["a_w_out"][l] = d_wout.reshape(N_CHIPS, W // N_CHIPS, D)
            dp, d_ws, d_bs, d_gv = _gate_bwd(f"a{l}_dgate", p, dy, a_v_norm_g[l], wc[l], wc_t[l], bs_t[l])
            dims_dh = dict(tb=True, tm=1024, tn=1024, tk=3072)
            dims_dwin = dict(tm=2048, tn=1024, tk=1024, out_colblocks=N_CHIPS)
            if i == 0:
                early, late = ["b_w_in", "b_w_out", "a_w_out"], ["a_w_in"]
                sums_e, active0 = chip_sums_of("0e", early, 0, sent0)
                d_win, got_e = _matmul(f"a{l}_dwin", _View(h_t), _View(dp), carried=("scatter", sums_e[:2], 0),
                                       **dims_dwin)
                big["a_w_in"][l] = d_win
                sums_l, _ = chip_sums_of("0l", late, 0)
                dh, got_l = _matmul(f"a{l}_dh", _View(dp), view_ain(l), carried=("scatter", sums_e[2:] + sums_l, 0),
                                    **dims_dh)
                reduce_slots("0e", early, 0, sums_e, got_e + got_l[:1], active0)
                reduce_slots("0l", late, 0, sums_l, got_l[1:], active0)
            else:
                sent = ["b_w_in", "b_w_out", "a_w_out"]
                dh, got = _matmul(f"a{l}_dh", _View(dp), view_ain(l), carried=("send", [big[n][1] for n in sent], 1),
                                  **dims_dh)
                sent1 = list(zip(sent, got))
                d_win = _matmul(f"a{l}_dwin", _View(h_t), _View(dp), **dims_dwin)
                big["a_w_in"][l] = d_win
            g, gb, d_gn = _rmsnorm_bwd(f"a{l}_dnorm", x_in, a_norm_g[l], dh, g)
            small["a_norm_g"][l] = d_gn.reshape(D)
            small["a_v_norm_g"][l] = d_gv.reshape(W)
            small["a_w_s"][l] = jnp.where(causal[None], d_ws, 0.0)
            small["a_b_s"][l] = d_bs[:, :G].T
        else:
            x_in, h_t, proj, f, qa, ka, va, o, y, lse = saved[i]
            dy = _matmul(f"b{l}_dy", _View(gb), view_bout(l), tb=True, out_dtype=BF16, tm=1024, tn=1024, tk=2048)
            d_wout = _matmul(f"b{l}_dwout", _View(y), _View(gb), ta=True, tm=2048, tn=1024, tk=1024)
            lse_lanes = jnp.pad(lse.reshape(H, S).T, ((0, 0), (0, LANES - H)))
            doa, dz, qab = _attn_bwd_prep(f"b{l}_dprep", dy, o, proj, qa, lse_lanes, H)
            early = (i == 1)
            if early:
                sums1, active1 = chip_sums_of("1", names, 1, sent1)
            dqa, dka, dv, got1 = _attn_bwd(f"b{l}_dattn", qab, doa, ka, va, H, tq_bwd,
                                           carried=(sums1, 1) if early else None)
            if early:
                reduce_slots("1", names, 1, sums1, got1, active1)
            dproj, d_gq, d_gk, dcum = _qk_bwd(f"b{l}_dqk", proj, dqa, dka, dv, dz, b_q_norm_g[l], b_k_norm_g[l], H)
            df, d_fb = _fox_cum_bwd(f"b{l}_dcum", dcum, f, f_bias[l])
            dh_f = _matmul(f"b{l}_dhf", _View(df), _View(w_bf[l]), tb=True, tm=1024, tn=1024, tk=LANES)
            dh = _matmul(f"b{l}_dh", _View(dproj), _View(w_bmain[l]), tb=True, tm=1024, tn=1024, tk=2048,
                         residual=dh_f)
            d_wmain = _matmul(f"b{l}_dwin", _View(h_t), _View(dproj), tm=2048, tn=1024, tk=1024)
            d_wf = _matmul(f"b{l}_dwinf", _View(h_t), _View(df), tm=2048, tn=LANES, tk=1024)
            d_win = jnp.concatenate([d_wmain, d_wf[:, :H]], axis=1)
            g, gb, d_gn = _rmsnorm_bwd(f"b{l}_dnorm", x_in, b_norm_full[l], dh, g)
            big["b_w_in"][l] = jnp.transpose(d_win.reshape(D, N_CHIPS, cb), (1, 0, 2))
            big["b_w_out"][l] = d_wout.reshape(N_CHIPS, HW // N_CHIPS, D)
            small["b_norm_g"][l] = d_gn.reshape(D)
            small["b_f_bias"][l] = d_fb[0, :H]
            small["b_q_norm_g"][l] = d_gq.reshape(HEAD)
            small["b_k_norm_g"][l] = d_gk.reshape(HEAD)
    grad_x = g[None]

    small_names = ["a_norm_g", "a_v_norm_g", "a_w_s", "a_b_s", "b_norm_g", "b_f_bias", "b_q_norm_g", "b_k_norm_g"]
    small_parts = [jnp.stack(small[n]) for n in small_names]
    small_mine = _pack(small_parts)
    mine = [jnp.where(core == 0, reduced[0][n], reduced[1][n]) for n in names]
    others, small_all = _last_exchanges(mine, small_mine)
    grads = {n: jnp.where(core == 0, jnp.stack([m_, o_]), jnp.stack([o_, m_]))
             for n, m_, o_ in zip(names, mine, others)}
    small_all = lax.dynamic_update_slice(small_all, small_mine[None], (2 * chip + core, 0, 0))
    small_sum = _sum_slots("reduce_small", small_all, jnp.ones((1,), jnp.int32))
    for n, a in zip(small_names, _unpack(small_sum, [p.shape for p in small_parts])):
        grads[n] = a
    nb = b_norm_g.shape[1]
    grads["b_norm_g"] = lax.dynamic_slice_in_dim(grads["b_norm_g"], chip * nb, nb, axis=1)

    weights = dict(a_norm_g=a_norm_g, a_w_in=a_w_in, a_v_norm_g=a_v_norm_g, a_w_s=a_w_s, a_b_s=a_b_s,
                   a_w_out=a_w_out, b_norm_g=b_norm_g, b_w_in=b_w_in, b_f_bias=b_f_bias,
                   b_q_norm_g=b_q_norm_g, b_k_norm_g=b_k_norm_g, b_w_out=b_w_out)
    mom1 = dict(a_norm_g=m_a_norm_g, a_w_in=m_a_w_in, a_v_norm_g=m_a_v_norm_g, a_w_s=m_a_w_s, a_b_s=m_a_b_s,
                a_w_out=m_a_w_out, b_norm_g=m_b_norm_g, b_w_in=m_b_w_in, b_f_bias=m_b_f_bias,
                b_q_norm_g=m_b_q_norm_g, b_k_norm_g=m_b_k_norm_g, b_w_out=m_b_w_out)
    mom2 = dict(a_norm_g=v_a_norm_g, a_w_in=v_a_w_in, a_v_norm_g=v_a_v_norm_g, a_w_s=v_a_w_s, a_b_s=v_a_b_s,
                a_w_out=v_a_w_out, b_norm_g=v_b_norm_g, b_w_in=v_b_w_in, b_f_bias=v_b_f_bias,
                b_q_norm_g=v_b_q_norm_g, b_k_norm_g=v_b_k_norm_g, b_w_out=v_b_w_out)
    order = ["a_norm_g", "a_w_in", "a_v_norm_g", "a_w_s", "a_b_s", "a_w_out", "b_norm_g", "b_w_in", "b_f_bias",
             "b_q_norm_g", "b_k_norm_g", "b_w_out"]
    delta, new_m, new_v = {}, {}, {}
    for n in names:
        shp = weights[n].shape
        flat = lambda a: a.reshape(shp[0] * shp[1], shp[2])
        d, nm, nv = _adamw(f"adamw_{n}", flat(weights[n]), flat(grads[n]), flat(mom1[n]), flat(mom2[n]))
        delta[n], new_m[n], new_v[n] = d.reshape(shp), nm.reshape(shp), nv.reshape(shp)
    small_shapes = [weights[n].shape for n in small_names]
    pack_w, pack_g, pack_m, pack_v = (_pack([d[n] for n in small_names]) for d in (weights, grads, mom1, mom2))
    d, nm, nv = _adamw("adamw_small", pack_w, pack_g, pack_m, pack_v)
    for dst, buf in ((delta, d), (new_m, nm), (new_v, nv)):
        for n, a in zip(small_names, _unpack(buf, small_shapes)):
            dst[n] = a

    return (loss, grad_x, *[grads[n] for n in order], *[delta[n] for n in order],
            *[new_m[n] for n in order], *[new_v[n] for n in order])
```
